```python
import jax, jax.numpy as jnp
from jax import lax
import numpy as np

D_MODEL = 2048
BATCH = 4
SEQ = 2048
DEPTH = 4
DEC_BATCH = 8
DEC_SEQ = 4
PAST_LEN = 16384
PAGE_SIZE = 128

N_A_LAYERS = DEPTH // 2
N_B_LAYERS = DEPTH - N_A_LAYERS
D_FF = 5632
D_GATE = D_MODEL
CHUNK = 128
N_GROUPS_A = 8
GROUP_DIM_A = D_GATE // N_GROUPS_A
HEAD_DIM = 128
N_HEADS = D_MODEL // HEAD_DIM
N_KV = 4
HPG = N_HEADS // N_KV
ROT_DIM = HEAD_DIM // 4
ROPE_THETA = 500000.0
CMP_BLOCK = 64
SEL_BLOCK = 64
TOP_N = 16
WINDOW = 512
Q_BLOCK = 128
SEL_Q_BLOCK = 64
N_BRANCH = 3
ALPHA = (2 * DEPTH) ** 0.25
BETA = (8 * DEPTH) ** -0.25
LN_EPS = 1e-5
NEG = -1e30
FORCED = 1e4

kernel_name = 'yoco_gmlp_nsa_macaron_deepnorm_step'


def layer_norm(x, g, b):
    xf = x.astype(jnp.float32)
    mu = xf.mean(-1, keepdims=True)
    var = jnp.square(xf - mu).mean(-1, keepdims=True)
    return ((xf - mu) * lax.rsqrt(var + LN_EPS) * g + b).astype(x.dtype)


def post_norm(x, f, g, b):
    return layer_norm(ALPHA * x + f, g, b)


def swiglu(x, w_in, w_out):
    gate, up = jnp.split(x @ w_in, 2, axis=-1)
    return (jax.nn.silu(gate) * up) @ w_out


def rotary(x, pos):
    half = ROT_DIM // 2
    inv = ROPE_THETA ** (-jnp.arange(half, dtype=jnp.float32) / half)
    ang = pos.astype(jnp.float32)[:, None] * inv[None, :]
    ang = ang.reshape(ang.shape[:1] + (1,) * (x.ndim - 3) + (half,))
    cos, sin = jnp.cos(ang), jnp.sin(ang)
    xr = x[..., :ROT_DIM].astype(jnp.float32)
    x1, x2 = xr[..., :half], xr[..., half:]
    rot = jnp.concatenate([x1 * cos - x2 * sin, x2 * cos + x1 * sin], -1).astype(x.dtype)
    return jnp.concatenate([rot, x[..., ROT_DIM:]], -1)


def gmlp_mixer(x, w_in, b_in, ln_g, ln_b, w_s, b_s, w_out):
    B, T, _ = x.shape
    c = min(T, CHUNK)
    z = jax.nn.gelu(x @ w_in + b_in, approximate=False)
    u, v = jnp.split(z, 2, axis=-1)
    v = layer_norm(v, ln_g, ln_b)
    vc = v.reshape(B, T // c, c, N_GROUPS_A, GROUP_DIM_A)
    w = jnp.tril(w_s[:, :c, :c])
    mixed = jnp.einsum('gts,bnsgd->bntgd', w, vc) + b_s[:, :c].T[None, None, :, :, None]
    y = u * mixed.reshape(B, T, D_GATE)
    return y @ w_out, v


def shared_kv(h, w_kv, pos):
    B, T, _ = h.shape
    kv = (h @ w_kv).reshape(B, T, N_BRANCH, 2, N_KV, HEAD_DIM)
    k = rotary(kv[:, :, :, 0], pos)
    kv = jnp.stack([k, kv[:, :, :, 1]], axis=3)
    return kv[:, :, 0], kv[:, :, 1], kv[:, :, 2]


def gather_pages(cache, page_table):
    pages = cache[page_table]
    return pages.reshape((page_table.shape[0], -1) + cache.shape[2:])


def compress(cmp_kv, cmp_pos, w_cmp):
    B, L = cmp_kv.shape[:2]
    nc = L // CMP_BLOCK
    blk = cmp_kv[:, :nc * CMP_BLOCK].reshape(B, nc, CMP_BLOCK, 2, N_KV, HEAD_DIM)
    blk = blk + cmp_pos[None, None, :, :, None, :]
    c = jnp.einsum('bnlcgd,lcde->bncge', blk, w_cmp)
    return c[:, :, 0], c[:, :, 1]


def select_blocks(slc_kv):
    B, L = slc_kv.shape[:2]
    nb = -(-L // SEL_BLOCK)
    kv = jnp.pad(slc_kv, ((0, 0), (0, nb * SEL_BLOCK - L), (0, 0), (0, 0), (0, 0)))
    kv = kv.reshape(B, nb, SEL_BLOCK, 2, N_KV, HEAD_DIM)
    return kv.transpose(0, 4, 1, 2, 3, 5)


def window_context_prompt(win_kv):
    B, T = win_kv.shape[:2]
    nqb = T // Q_BLOCK
    padded = jnp.pad(win_kv, ((0, 0), (WINDOW, 0), (0, 0), (0, 0), (0, 0)))
    idx = jnp.arange(nqb)[:, None] * Q_BLOCK + jnp.arange(WINDOW + Q_BLOCK)[None, :]
    return padded[:, idx], idx - WINDOW


def selected_attention(q, pos, idx, sel_kv, scale):
    B, T = q.shape[:2]
    tc = SEL_Q_BLOCK if T % SEL_Q_BLOCK == 0 else T
    nq = T // tc
    bi = jnp.arange(B)[:, None, None, None]
    gi = jnp.arange(N_KV)[None, :, None, None]
    s_off = jnp.arange(SEL_BLOCK)

    def one(args):
        qc, pc, ic = args
        kv = sel_kv[bi, gi, ic]
        s = jnp.einsum('btghd,bgtnsd->bghtns', qc, kv[..., 0, :]).astype(jnp.float32) * scale
        tok = ic[..., None] * SEL_BLOCK + s_off
        ok = tok <= pc[None, None, :, None, None]
        s = jnp.where(ok[:, :, None], s, NEG)
        p = jax.nn.softmax(s.reshape(s.shape[:4] + (-1,)), axis=-1).reshape(s.shape)
        return jnp.einsum('bghtns,bgtnsd->btghd', p.astype(kv.dtype), kv[..., 1, :])

    qs = q.reshape(B, nq, tc, N_KV, HPG, HEAD_DIM).transpose(1, 0, 2, 3, 4, 5)
    ps = pos.reshape(nq, tc)
    ids = idx.reshape(B, N_KV, nq, tc, -1).transpose(2, 0, 1, 3, 4)
    out = lax.map(one, (qs, ps, ids))
    return out.transpose(1, 0, 2, 3, 4, 5).reshape(B, T, N_KV, HPG, HEAD_DIM)


def window_attention(q, pos, ctx, kpos, scale):
    B, T = q.shape[:2]
    nqb = kpos.shape[0]
    qb = T // nqb
    qr = q.reshape(B, nqb, qb, N_KV, HPG, HEAD_DIM)
    s = jnp.einsum('bntghd,bnkgd->bnghtk', qr, ctx[:, :, :, 0]).astype(jnp.float32) * scale
    d = pos.reshape(nqb, qb)[:, :, None] - kpos[:, None, :]
    ok = (d >= 0) & (d < WINDOW) & (kpos[:, None, :] >= 0)
    p = jax.nn.softmax(jnp.where(ok[None, :, None, None], s, NEG), axis=-1)
    o = jnp.einsum('bnghtk,bnkgd->bntghd', p.astype(ctx.dtype), ctx[:, :, :, 1])
    return o.reshape(B, T, N_KV, HPG, HEAD_DIM)


def nsa_mixer(x, pos, w_qg, w_o, ck, cv, sel_kv, win_ctx, win_kpos):
    B, T, _ = x.shape
    scale = HEAD_DIM ** -0.5
    proj = x @ w_qg
    q = rotary(proj[..., :N_HEADS * HEAD_DIM].reshape(B, T, N_KV, HPG, HEAD_DIM), pos)
    gates = jax.nn.sigmoid(proj[..., N_HEADS * HEAD_DIM:].astype(jnp.float32)).reshape(B, T, N_KV, HPG, N_BRANCH)
    nc = ck.shape[1]
    s = jnp.einsum('btghd,bngd->bghtn', q, ck).astype(jnp.float32) * scale
    vis = (jnp.arange(nc)[None, :] + 1) * CMP_BLOCK - 1 <= pos[:, None]
    p = jax.nn.softmax(jnp.where(vis, s, NEG), axis=-1) * jnp.any(vis, axis=-1)[:, None]
    o_cmp = jnp.einsum('bghtn,bngd->btghd', p.astype(cv.dtype), cv)
    nb = sel_kv.shape[2]
    imp = jnp.pad(p.sum(2), ((0, 0), (0, 0), (0, 0), (0, nb - nc)))
    cur = (pos // SEL_BLOCK)[:, None]
    j = jnp.arange(nb)[None, :]
    forced = (j == 0) | (j == cur) | (j == cur - 1)
    score = jnp.where(forced, FORCED, jnp.where(j <= cur, imp, -1.0))
    _, idx = lax.top_k(score, min(TOP_N, nb))
    o_slc = selected_attention(q, pos, idx, sel_kv, scale)
    o_win = window_attention(q, pos, win_ctx, win_kpos, scale)
    o = (gates[..., 0:1] * o_cmp + gates[..., 1:2] * o_slc + gates[..., 2:3] * o_win).astype(x.dtype)
    return o.reshape(B, T, N_HEADS * HEAD_DIM) @ w_o


def run_trunk(x, pos, build_context, ln_g, ln_b, ffn_w_in, ffn_w_out, gmlp_w_in, gmlp_b_in,
              gmlp_ln_g, gmlp_ln_b, gmlp_w_s, gmlp_b_s, gmlp_w_out, nsa_w_qg, nsa_w_o):
    v_rows = []
    ctx, new_rows = None, None
    for l in range(DEPTH):
        if l == N_A_LAYERS:
            ctx, new_rows = build_context(x)
        x = post_norm(x, 0.5 * swiglu(x, ffn_w_in[l, 0], ffn_w_out[l, 0]), ln_g[l, 0], ln_b[l, 0])
        if l < N_A_LAYERS:
            m, v = gmlp_mixer(x, gmlp_w_in[l], gmlp_b_in[l], gmlp_ln_g[l], gmlp_ln_b[l],
                              gmlp_w_s[l], gmlp_b_s[l], gmlp_w_out[l])
            v_rows.append(v)
        else:
            b = l - N_A_LAYERS
            m = nsa_mixer(x, pos, nsa_w_qg[b], nsa_w_o[b], *ctx)
        x = post_norm(x, m, ln_g[l, 1], ln_b[l, 1])
        x = post_norm(x, 0.5 * swiglu(x, ffn_w_in[l, 1], ffn_w_out[l, 1]), ln_g[l, 2], ln_b[l, 2])
    return x, new_rows, v_rows


def setup_inputs(seed: int = 0) -> dict:
    key = jax.random.key(seed)
    ks = jax.random.split(key, 24)

    def nrm(k, shape, scale):
        return jax.random.normal(k, shape, jnp.float32) * scale

    n_pages = PAST_LEN // PAGE_SIZE
    n_used = DEC_BATCH * n_pages
    n_pool = n_used + n_used // 4
    w_buf = min(WINDOW, PAST_LEN)
    page_table = jax.random.permutation(ks[0], n_pool)[:n_used].reshape(DEC_BATCH, n_pages).astype(jnp.int32)
    return {
        'x_prompt': nrm(ks[1], (BATCH, SEQ, D_MODEL), 1.0),
        'x_sample': nrm(ks[2], (DEC_BATCH, DEC_SEQ, D_MODEL), 1.0),
        'cache_cmp_kv': nrm(ks[3], (n_pool, PAGE_SIZE, 2, N_KV, HEAD_DIM), 1.0),
        'cache_slc_kv': nrm(ks[4], (n_pool, PAGE_SIZE, 2, N_KV, HEAD_DIM), 1.0),
        'cache_win_kv': nrm(ks[5], (DEC_BATCH, w_buf, 2, N_KV, HEAD_DIM), 1.0),
        'page_table': page_table,
        'ln_g': 1.0 + nrm(ks[6], (DEPTH, 3, D_MODEL), 0.02),
        'ln_b': nrm(ks[7], (DEPTH, 3, D_MODEL), 0.02),
        'ffn_w_in': nrm(ks[8], (DEPTH, 2, D_MODEL, 2 * D_FF), D_MODEL ** -0.5),
        'ffn_w_out': nrm(ks[9], (DEPTH, 2, D_FF, D_MODEL), BETA * D_FF ** -0.5),
        'gmlp_w_in': nrm(ks[10], (N_A_LAYERS, D_MODEL, 2 * D_GATE), D_MODEL ** -0.5),
        'gmlp_b_in': nrm(ks[11], (N_A_LAYERS, 2 * D_GATE), 0.02),
        'gmlp_ln_g': 1.0 + nrm(ks[12], (N_A_LAYERS, D_GATE), 0.02),
        'gmlp_ln_b': nrm(ks[13], (N_A_LAYERS, D_GATE), 0.02),
        'gmlp_w_s': nrm(ks[14], (N_A_LAYERS, N_GROUPS_A, CHUNK, CHUNK), 0.5 * CHUNK ** -0.5),
        'gmlp_b_s': 1.0 + nrm(ks[15], (N_A_LAYERS, N_GROUPS_A, CHUNK), 0.02),
        'gmlp_w_out': nrm(ks[16], (N_A_LAYERS, D_GATE, D_MODEL), BETA * D_GATE ** -0.5),
        'nsa_w_qg': nrm(ks[17], (N_B_LAYERS, D_MODEL, N_HEADS * HEAD_DIM + N_HEADS * N_BRANCH), D_MODEL ** -0.5),
        'nsa_w_o': nrm(ks[18], (N_B_LAYERS, N_HEADS * HEAD_DIM, D_MODEL), BETA * (N_HEADS * HEAD_DIM) ** -0.5),
        'w_kv': nrm(ks[19], (D_MODEL, N_BRANCH * 2 * N_KV * HEAD_DIM), D_MODEL ** -0.5),
        'cmp_pos': nrm(ks[20], (CMP_BLOCK, 2, HEAD_DIM), 0.1),
        'w_cmp': nrm(ks[21], (CMP_BLOCK, 2, HEAD_DIM, HEAD_DIM), (CMP_BLOCK * HEAD_DIM) ** -0.5),
    }


def reference(x_prompt, x_sample, cache_cmp_kv, cache_slc_kv, cache_win_kv, page_table,
              ln_g, ln_b, ffn_w_in, ffn_w_out, gmlp_w_in, gmlp_b_in, gmlp_ln_g, gmlp_ln_b,
              gmlp_w_s, gmlp_b_s, gmlp_w_out, nsa_w_qg, nsa_w_o, w_kv, cmp_pos, w_cmp):
    pos_p = jnp.arange(SEQ, dtype=jnp.int32)
    pos_s = PAST_LEN + jnp.arange(DEC_SEQ, dtype=jnp.int32)

    def prompt_context(h):
        cmp_kv, slc_kv, win_kv = shared_kv(h, w_kv, pos_p)
        ck, cv = compress(cmp_kv, cmp_pos, w_cmp)
        wctx, wpos = window_context_prompt(win_kv)
        new = (cmp_kv, slc_kv, win_kv[:, -min(WINDOW, SEQ):])
        return (ck, cv, select_blocks(slc_kv), wctx, wpos), new

    def sample_context(h):
        cmp_new, slc_new, win_new = shared_kv(h, w_kv, pos_s)
        cmp_kv = jnp.concatenate([gather_pages(cache_cmp_kv, page_table), cmp_new], axis=1)
        slc_kv = jnp.concatenate([gather_pages(cache_slc_kv, page_table), slc_new], axis=1)
        win_kv = jnp.concatenate([cache_win_kv, win_new], axis=1)
        ck, cv = compress(cmp_kv, cmp_pos, w_cmp)
        w_buf = cache_win_kv.shape[1]
        wpos = (PAST_LEN - w_buf + jnp.arange(w_buf + DEC_SEQ))[None]
        new = (cmp_new, slc_new, win_kv[:, -min(WINDOW, PAST_LEN + DEC_SEQ):])
        return (ck, cv, select_blocks(slc_kv), win_kv[:, None], wpos), new

    y_prompt, (p_cmp, p_slc, p_win), _ = run_trunk(
        x_prompt, pos_p, prompt_context, ln_g, ln_b, ffn_w_in, ffn_w_out, gmlp_w_in, gmlp_b_in,
        gmlp_ln_g, gmlp_ln_b, gmlp_w_s, gmlp_b_s, gmlp_w_out, nsa_w_qg, nsa_w_o)
    y_sample, (s_cmp, s_slc, s_win), s_v = run_trunk(
        x_sample, pos_s, sample_context, ln_g, ln_b, ffn_w_in, ffn_w_out, gmlp_w_in, gmlp_b_in,
        gmlp_ln_g, gmlp_ln_b, gmlp_w_s, gmlp_b_s, gmlp_w_out, nsa_w_qg, nsa_w_o)
    return (y_prompt, y_sample, p_cmp, p_slc, p_win, s_cmp, s_slc, s_win, jnp.stack(s_v))
```

```python
import functools
import math

import numpy as np
import jax
import jax.numpy as jnp
from jax import lax
from jax.experimental import pallas as pl
from jax.experimental.pallas import tpu as pltpu

D_MODEL = 2048
BATCH = 4
SEQ = 2048
DEPTH = 4
DEC_BATCH = 8
DEC_SEQ = 4
PAST_LEN = 16384
PAGE_SIZE = 128

N_A_LAYERS = DEPTH // 2
N_B_LAYERS = DEPTH - N_A_LAYERS
D_FF = 5632
D_GATE = D_MODEL
CHUNK = 128
N_GROUPS_A = 8
GROUP_DIM_A = D_GATE // N_GROUPS_A
HEAD_DIM = 128
N_HEADS = D_MODEL // HEAD_DIM
N_KV = 4
HPG = N_HEADS // N_KV
ROT_DIM = HEAD_DIM // 4
ROPE_THETA = 500000.0
CMP_BLOCK = 64
SEL_BLOCK = 64
TOP_N = 16
WINDOW = 512
N_BRANCH = 3
ALPHA = (2 * DEPTH) ** 0.25
LN_EPS = 1e-5
NEG = -1e30
FORCED = 1e4

F32 = jnp.float32
BF16 = jnp.bfloat16

V7X_VMEM_LIMIT_BYTES = 56 * 1024 * 1024
LANES = 128
KV_COLS = 2 * N_KV * HEAD_DIM
GATE_COLS = HPG * N_BRANCH
KEY_CHUNK = 512
PAGES_PER_STEP = 16
Q_TILE = 128


def _params(*sem):
    return pltpu.CompilerParams(dimension_semantics=sem, vmem_limit_bytes=V7X_VMEM_LIMIT_BYTES)


def _row_tile(m, cap=512):
    return m if m <= cap else cap


def _dot(a, b):
    return jnp.dot(a, b, preferred_element_type=F32)


def _dot_nt(a, b):
    return lax.dot_general(a, b, (((1,), (1,)), ((), ())), preferred_element_type=F32)


def _dot_tn(a, b):
    return lax.dot_general(a, b, (((0,), (0,)), ((), ())), preferred_element_type=F32)


def _layer_norm(x, g, b):
    mu = jnp.mean(x, axis=-1, keepdims=True)
    d = x - mu
    var = jnp.mean(d * d, axis=-1, keepdims=True)
    return d * lax.rsqrt(var + LN_EPS) * g + b


def _sigmoid(x):
    return 1.0 / (1.0 + jnp.exp(-x))


def _softmax_rows(s):
    m = jnp.max(s, axis=-1, keepdims=True)
    p = jnp.exp(s - m)
    return p / jnp.sum(p, axis=-1, keepdims=True)


def _rotate(x, cos_t, sin_lo, sin_hi):
    half = ROT_DIM // 2
    return (x * cos_t + pltpu.roll(x, LANES - half, 1) * sin_lo + pltpu.roll(x, half, 1) * sin_hi)


def _rotary_tables(pos):
    half = ROT_DIM // 2
    inv = ROPE_THETA ** (-jnp.arange(half, dtype=F32) / half)
    ang = jnp.asarray(pos).astype(F32)[:, None] * inv[None, :]
    cos, sin = jnp.cos(ang), jnp.sin(ang)
    rows = ang.shape[0]
    pad = jnp.zeros((rows, LANES - ROT_DIM), F32)
    zero = jnp.zeros((rows, half), F32)
    cos_t = jnp.concatenate([cos, cos, pad + 1.0], axis=1)
    sin_lo = jnp.concatenate([-sin, zero, pad], axis=1)
    sin_hi = jnp.concatenate([zero, sin, pad], axis=1)
    return cos_t, sin_lo, sin_hi


def _ffn_kernel(xb_ref, x_ref, wg_ref, wu_ref, wo_ref, g_ref, b_ref, y_ref, yb_ref, acc_ref):
    f = pl.program_id(1)

    @pl.when(f == 0)
    def _():
        acc_ref[...] = jnp.zeros_like(acc_ref)

    xb = xb_ref[...]
    gate = _dot(xb, wg_ref[...])
    up = _dot(xb, wu_ref[...])
    h = gate * _sigmoid(gate) * up
    acc_ref[...] += _dot(h.astype(BF16), wo_ref[...])

    @pl.when(f == pl.num_programs(1) - 1)
    def _():
        y = _layer_norm(ALPHA * x_ref[...] + 0.5 * acc_ref[...], g_ref[...], b_ref[...])
        y_ref[...] = y
        yb_ref[...] = y.astype(BF16)


def _ffn(x, xb, w_in, w_out, l, s, g, b):
    m = x.shape[0]
    tm = _row_tile(m)
    tf = 512
    nf = D_FF // tf
    row = lambda i, f: (i, 0)
    return pl.pallas_call(
        _ffn_kernel,
        grid=(m // tm, nf),
        in_specs=[
            pl.BlockSpec((tm, D_MODEL), row),
            pl.BlockSpec((tm, D_MODEL), row),
            pl.BlockSpec((None, None, D_MODEL, tf), lambda i, f: (l, s, 0, f)),
            pl.BlockSpec((None, None, D_MODEL, tf), lambda i, f: (l, s, 0, nf + f)),
            pl.BlockSpec((None, None, tf, D_MODEL), lambda i, f: (l, s, f, 0)),
            pl.BlockSpec((1, D_MODEL), lambda i, f: (0, 0)),
            pl.BlockSpec((1, D_MODEL), lambda i, f: (0, 0)),
        ],
        out_specs=[pl.BlockSpec((tm, D_MODEL), row), pl.BlockSpec((tm, D_MODEL), row)],
        out_shape=[jax.ShapeDtypeStruct((m, D_MODEL), F32), jax.ShapeDtypeStruct((m, D_MODEL), BF16)],
        scratch_shapes=[pltpu.VMEM((tm, D_MODEL), F32)],
        compiler_params=_params("parallel", "arbitrary"),
        name="ffn",
    )(xb, x, w_in, w_in, w_out, g, b)


def _gmlp_in_kernel(xb_ref, w_ref, bias_ref, lg_ref, lb_ref, z_ref):
    n = pl.program_id(0)
    z = _dot(xb_ref[...], w_ref[...]) + bias_ref[...]
    z = 0.5 * z * (1.0 + lax.erf(z * (1.0 / math.sqrt(2.0))))

    @pl.when(n == 0)
    def _():
        z_ref[...] = z

    @pl.when(n == 1)
    def _():
        z_ref[...] = _layer_norm(z, lg_ref[...], lb_ref[...])


def _gmlp_in(xb, w, bias, lg, lb):
    m = xb.shape[0]
    tm = _row_tile(m, 256)
    return pl.pallas_call(
        _gmlp_in_kernel,
        grid=(2, m // tm),
        in_specs=[
            pl.BlockSpec((tm, D_MODEL), lambda n, i: (i, 0)),
            pl.BlockSpec((D_MODEL, D_GATE), lambda n, i: (0, n)),
            pl.BlockSpec((1, D_GATE), lambda n, i: (0, n)),
            pl.BlockSpec((1, D_GATE), lambda n, i: (0, 0)),
            pl.BlockSpec((1, D_GATE), lambda n, i: (0, 0)),
        ],
        out_specs=pl.BlockSpec((tm, D_GATE), lambda n, i: (i, n)),
        out_shape=jax.ShapeDtypeStruct((m, 2 * D_GATE), F32),
        compiler_params=_params("arbitrary", "arbitrary"),
        name="gmlp_in",
    )(xb, w, bias, lg, lb)


def _gmlp_out_kernel(u_ref, v_ref, x_ref, ws_ref, bs_ref, wo_ref, g_ref, b_ref, y_ref, yb_ref, *, rows, causal):
    tm = u_ref.shape[0]
    r_i = lax.broadcasted_iota(jnp.int32, (rows, rows), 0)
    c_i = lax.broadcasted_iota(jnp.int32, (rows, rows), 1)
    keep = ((r_i // causal) == (c_i // causal)) & (c_i <= r_i)
    w_sp = [jnp.where(keep, ws_ref[g], 0.0).astype(BF16) for g in range(N_GROUPS_A)]
    bs = bs_ref[...]
    chunks = []
    for ch in range(tm // rows):
        u = u_ref[ch * rows:(ch + 1) * rows, :]
        v = v_ref[ch * rows:(ch + 1) * rows, :]
        parts = []
        for g in range(N_GROUPS_A):
            lo, hi = g * GROUP_DIM_A, (g + 1) * GROUP_DIM_A
            mixed = _dot(w_sp[g], v[:, lo:hi].astype(BF16)) + bs[:, g:g + 1]
            parts.append((u[:, lo:hi] * mixed).astype(BF16))
        chunks.append(jnp.concatenate(parts, axis=1))
    y = chunks[0] if len(chunks) == 1 else jnp.concatenate(chunks, axis=0)
    f = _dot(y, wo_ref[...])
    out = _layer_norm(ALPHA * x_ref[...] + f, g_ref[...], b_ref[...])
    y_ref[...] = out
    yb_ref[...] = out.astype(BF16)


def _gmlp_out(z, x, w_sp, b_sp, w_out, g, b, rows, causal):
    m = x.shape[0]
    tm = _row_tile(m)
    row = lambda i: (i, 0)
    kern = functools.partial(_gmlp_out_kernel, rows=rows, causal=causal)
    return pl.pallas_call(
        kern,
        grid=(m // tm,),
        in_specs=[
            pl.BlockSpec((tm, D_GATE), lambda i: (i, 0)),
            pl.BlockSpec((tm, D_GATE), lambda i: (i, 1)),
            pl.BlockSpec((tm, D_MODEL), row),
            pl.BlockSpec((N_GROUPS_A, rows, rows), lambda i: (0, 0, 0)),
            pl.BlockSpec((rows, N_GROUPS_A), lambda i: (0, 0)),
            pl.BlockSpec((D_GATE, D_MODEL), lambda i: (0, 0)),
            pl.BlockSpec((1, D_MODEL), lambda i: (0, 0)),
            pl.BlockSpec((1, D_MODEL), lambda i: (0, 0)),
        ],
        out_specs=[pl.BlockSpec((tm, D_MODEL), row), pl.BlockSpec((tm, D_MODEL), row)],
        out_shape=[jax.ShapeDtypeStruct((m, D_MODEL), F32), jax.ShapeDtypeStruct((m, D_MODEL), BF16)],
        compiler_params=_params("parallel"),
        name="gmlp_out",
    )(z, z, x, w_sp, b_sp, w_out, g, b)


def _proj_norm_kernel(a_ref, x_ref, w_ref, g_ref, b_ref, y_ref, yb_ref):
    out = _layer_norm(ALPHA * x_ref[...] + _dot(a_ref[...], w_ref[...]), g_ref[...], b_ref[...])
    y_ref[...] = out
    yb_ref[...] = out.astype(BF16)


def _proj_norm(a, x, w, g, b):
    m = x.shape[0]
    tm = _row_tile(m)
    row = lambda i: (i, 0)
    return pl.pallas_call(
        _proj_norm_kernel,
        grid=(m // tm,),
        in_specs=[
            pl.BlockSpec((tm, D_MODEL), row),
            pl.BlockSpec((tm, D_MODEL), row),
            pl.BlockSpec((D_MODEL, D_MODEL), lambda i: (0, 0)),
            pl.BlockSpec((1, D_MODEL), lambda i: (0, 0)),
            pl.BlockSpec((1, D_MODEL), lambda i: (0, 0)),
        ],
        out_specs=[pl.BlockSpec((tm, D_MODEL), row), pl.BlockSpec((tm, D_MODEL), row)],
        out_shape=[jax.ShapeDtypeStruct((m, D_MODEL), F32), jax.ShapeDtypeStruct((m, D_MODEL), BF16)],
        compiler_params=_params("parallel"),
        name="proj_norm",
    )(a, x, w, g, b)


def _kv_proj_kernel(xb_ref, w_ref, cos_ref, slo_ref, shi_ref, kv_ref, kvb_ref):
    n = pl.program_id(1)
    acc = _dot(xb_ref[...], w_ref[...])

    @pl.when(n % 2 == 0)
    def _():
        cos_t, s_lo, s_hi = cos_ref[...], slo_ref[...], shi_ref[...]
        heads = [_rotate(acc[:, h * HEAD_DIM:(h + 1) * HEAD_DIM], cos_t, s_lo, s_hi) for h in range(N_KV)]
        rot = jnp.concatenate(heads, axis=1)
        kv_ref[...] = rot
        kvb_ref[...] = rot.astype(BF16)

    @pl.when(n % 2 == 1)
    def _():
        kv_ref[...] = acc
        kvb_ref[...] = acc.astype(BF16)


def _kv_proj(xb, w, tables, pos_rows):
    m = xb.shape[0]
    tm = _row_tile(m)
    half = N_KV * HEAD_DIM
    ncol = N_BRANCH * 2
    nt = pos_rows // tm
    tab = lambda i, n: (i % nt, 0)
    return pl.pallas_call(
        _kv_proj_kernel,
        grid=(m // tm, ncol),
        in_specs=[
            pl.BlockSpec((tm, D_MODEL), lambda i, n: (i, 0)),
            pl.BlockSpec((D_MODEL, half), lambda i, n: (0, n)),
            pl.BlockSpec((tm, LANES), tab),
            pl.BlockSpec((tm, LANES), tab),
            pl.BlockSpec((tm, LANES), tab),
        ],
        out_specs=[pl.BlockSpec((tm, half), lambda i, n: (i, n)), pl.BlockSpec((tm, half), lambda i, n: (i, n))],
        out_shape=[jax.ShapeDtypeStruct((m, ncol * half), F32), jax.ShapeDtypeStruct((m, ncol * half), BF16)],
        compiler_params=_params("parallel", "arbitrary"),
        name="kv_proj",
    )(xb, w, *tables)


def _qg_proj_kernel(xb_ref, wq_ref, wg_ref, cos_ref, slo_ref, shi_ref, q_ref, gt_ref):
    xb = xb_ref[...]
    acc = _dot(xb, wq_ref[...])
    cos_t, s_lo, s_hi = cos_ref[...], slo_ref[...], shi_ref[...]
    for h in range(N_HEADS):
        lo, hi = h * HEAD_DIM, (h + 1) * HEAD_DIM
        q_ref[:, lo:hi] = _rotate(acc[:, lo:hi], cos_t, s_lo, s_hi).astype(BF16)
    gt_ref[...] = _sigmoid(_dot(xb, wg_ref[...]))


def _qg_proj(xb, wq, wg, tables, pos_rows):
    m = xb.shape[0]
    tm = _row_tile(m)
    nt = pos_rows // tm
    tab = lambda i: (i % nt, 0)
    gcols = N_KV * LANES
    return pl.pallas_call(
        _qg_proj_kernel,
        grid=(m // tm,),
        in_specs=[
            pl.BlockSpec((tm, D_MODEL), lambda i: (i, 0)),
            pl.BlockSpec((D_MODEL, D_MODEL), lambda i: (0, 0)),
            pl.BlockSpec((D_MODEL, gcols), lambda i: (0, 0)),
            pl.BlockSpec((tm, LANES), tab),
            pl.BlockSpec((tm, LANES), tab),
            pl.BlockSpec((tm, LANES), tab),
        ],
        out_specs=[pl.BlockSpec((tm, D_MODEL), lambda i: (i, 0)), pl.BlockSpec((tm, gcols), lambda i: (i, 0))],
        out_shape=[jax.ShapeDtypeStruct((m, D_MODEL), BF16), jax.ShapeDtypeStruct((m, gcols), F32)],
        compiler_params=_params("parallel"),
        name="qg_proj",
    )(xb, wq, wg, *tables)


def _compress_kernel(tbl_ref, *refs):
    del tbl_ref
    npg = PAGES_PER_STEP
    slabs = refs[:npg]
    pos_ref, w_ref, out_ref, buf_ref = refs[npg:]
    for p in range(npg):
        for cb in range(2 * N_KV):
            lanes = slice(cb * HEAD_DIM, (cb + 1) * HEAD_DIM)
            buf_ref[cb, p * PAGE_SIZE:(p + 1) * PAGE_SIZE, :] = slabs[p][:, lanes] + pos_ref[:, lanes]
    nblk = npg * (PAGE_SIZE // CMP_BLOCK)

    def body(l, accs):
        new = []
        for c in range(2):
            lhs = jnp.concatenate(
                [buf_ref[c * N_KV + g, pl.ds(l, nblk, stride=CMP_BLOCK), :] for g in range(N_KV)],
                axis=0)
            new.append(accs[c] + _dot(lhs.astype(BF16), w_ref[l, c]))
        return tuple(new)

    zero = jnp.zeros((N_KV * nblk, HEAD_DIM), F32)
    accs = lax.fori_loop(0, CMP_BLOCK, body, (zero, zero))
    for c in range(2):
        for g in range(N_KV):
            col = (c * N_KV + g) * HEAD_DIM
            out_ref[:, col:col + HEAD_DIM] = accs[c][g * nblk:(g + 1) * nblk, :]


def _compress(rows, table, pos_t, w_cmp_b):
    npg = PAGES_PER_STEP
    nslab = table.shape[0]
    nblk = npg * (PAGE_SIZE // CMP_BLOCK)

    def slab_spec(p):
        return pl.BlockSpec((PAGE_SIZE, KV_COLS), lambda i, tbl: (tbl[i * npg + p], 0))

    grid_spec = pltpu.PrefetchScalarGridSpec(
        num_scalar_prefetch=1,
        grid=(nslab // npg,),
        in_specs=[slab_spec(p) for p in range(npg)] + [
            pl.BlockSpec((PAGE_SIZE, KV_COLS), lambda i, tbl: (0, 0)),
            pl.BlockSpec((CMP_BLOCK, 2, HEAD_DIM, HEAD_DIM), lambda i, tbl: (0, 0, 0, 0)),
        ],
        out_specs=pl.BlockSpec((nblk, KV_COLS), lambda i, tbl: (i, 0)),
        scratch_shapes=[pltpu.VMEM((2 * N_KV, npg * PAGE_SIZE, HEAD_DIM), F32)],
    )
    return pl.pallas_call(
        _compress_kernel,
        grid_spec=grid_spec,
        out_shape=jax.ShapeDtypeStruct((nslab * (PAGE_SIZE // CMP_BLOCK), KV_COLS), F32),
        compiler_params=_params("arbitrary"),
        name="compress",
    )(table, *([rows] * npg), pos_t, w_cmp_b)


def _rank_select(score_t, n_blocks):
    idx = lax.broadcasted_iota(jnp.int32, score_t.shape, 0)
    rank = jnp.zeros(score_t.shape, F32)
    for i in range(n_blocks):
        row = score_t[i:i + 1, :]
        rank = rank + jnp.where(idx > i, jnp.where(row >= score_t, 1.0, 0.0), jnp.where(row > score_t, 1.0, 0.0))
    return jnp.where(rank < float(min(TOP_N, n_blocks)), 1.0, 0.0)


def _nsa_prompt_kernel(q_ref, gt_ref, ck_ref, cv_ref, sk_ref, sv_ref, wk_ref, wv_ref, e_ref, o_ref, mask_ref):
    tq = q_ref.shape[0]
    t_len = sk_ref.shape[0]
    nblk = ck_ref.shape[0]
    i = pl.program_id(2)
    scale = HEAD_DIM ** -0.5
    q = q_ref[...]
    q4 = jnp.concatenate([q[:, h * HEAD_DIM:(h + 1) * HEAD_DIM] for h in range(HPG)], axis=0)
    pos_c = i * tq + lax.broadcasted_iota(jnp.int32, (tq, 1), 0)
    pos4_c = jnp.concatenate([pos_c] * HPG, axis=0)
    pos_r = i * tq + lax.broadcasted_iota(jnp.int32, (1, tq), 1)
    pos4_r = jnp.concatenate([pos_r] * HPG, axis=1)

    ck = ck_ref[...].astype(BF16)
    cv = cv_ref[...].astype(BF16)
    blk_r = lax.broadcasted_iota(jnp.int32, (1, nblk), 1)
    vis = (blk_r + 1) * CMP_BLOCK - 1 <= pos4_c
    p = _softmax_rows(jnp.where(vis, _dot_nt(q4, ck) * scale, NEG))
    p = p * jnp.where(pos4_c >= CMP_BLOCK - 1, 1.0, 0.0)
    o_cmp = _dot(p.astype(BF16), cv)

    blk_c = lax.broadcasted_iota(jnp.int32, (nblk, 1), 0)
    vis_t = (blk_c + 1) * CMP_BLOCK - 1 <= pos4_r
    s_t = jnp.where(vis_t, _dot_nt(ck, q4) * scale, NEG)
    p_t = jnp.exp(s_t - jnp.max(s_t, axis=0, keepdims=True))
    p_t = p_t / jnp.sum(p_t, axis=0, keepdims=True)
    p_t = p_t * jnp.where(pos4_r >= CMP_BLOCK - 1, 1.0, 0.0)
    imp_t = p_t[:, 0:tq]
    for h in range(1, HPG):
        imp_t = imp_t + p_t[:, h * tq:(h + 1) * tq]
    cur = pos_r // SEL_BLOCK
    forced = (blk_c == 0) | (blk_c == cur) | (blk_c == cur - 1)
    score_t = jnp.where(forced, FORCED, jnp.where(blk_c <= cur, imp_t, -1.0))
    sel_t = _rank_select(score_t, nblk)
    tok_mask = _dot_tn(sel_t.astype(BF16), e_ref[...])
    key_r = lax.broadcasted_iota(jnp.int32, (1, t_len), 1)
    tok_mask = jnp.where(key_r <= pos_c, tok_mask, 0.0)
    n_chunks_all = t_len // KEY_CHUNK
    for c in range(n_chunks_all):
        mask_ref[c] = tok_mask[:, c * KEY_CHUNK:(c + 1) * KEY_CHUNK]

    def body(kc, carry):
        m, l, acc = carry
        start = pl.multiple_of(kc * KEY_CHUNK, KEY_CHUNK)
        k = sk_ref[pl.ds(start, KEY_CHUNK), :]
        v = sv_ref[pl.ds(start, KEY_CHUNK), :]
        msk = mask_ref[kc]
        msk4 = jnp.concatenate([msk] * HPG, axis=0)
        s = jnp.where(msk4 > 0.5, _dot_nt(q4, k) * scale, NEG)
        m_new = jnp.maximum(m, jnp.max(s, axis=-1, keepdims=True))
        a = jnp.exp(m - m_new)
        pr = jnp.exp(s - m_new)
        l = a * l + jnp.sum(pr, axis=-1, keepdims=True)
        acc = a * acc + _dot(pr.astype(BF16), v)
        return m_new, l, acc

    n_chunks = (i * tq + tq + KEY_CHUNK - 1) // KEY_CHUNK
    init = (jnp.full((HPG * tq, 1), NEG, F32), jnp.zeros((HPG * tq, 1), F32), jnp.zeros((HPG * tq, HEAD_DIM), F32))
    _, l_s, acc_s = lax.fori_loop(0, n_chunks, body, init)
    o_slc = acc_s / l_s

    n_win = WINDOW + tq
    start = pl.multiple_of(jnp.maximum(i * tq - WINDOW, 0), tq)
    kw = wk_ref[pl.ds(start, n_win), :]
    vw = wv_ref[pl.ds(start, n_win), :]
    kpos = start + lax.broadcasted_iota(jnp.int32, (1, n_win), 1)
    d = pos4_c - kpos
    ok = (d >= 0) & (d < WINDOW)
    pw = _softmax_rows(jnp.where(ok, _dot_nt(q4, kw) * scale, NEG))
    o_win = _dot(pw.astype(BF16), vw)

    gt = gt_ref[...]
    for h in range(HPG):
        r0, r1 = h * tq, (h + 1) * tq
        c0 = h * N_BRANCH
        og = (gt[:, c0:c0 + 1] * o_cmp[r0:r1] + gt[:, c0 + 1:c0 + 2] * o_slc[r0:r1]
              + gt[:, c0 + 2:c0 + 3] * o_win[r0:r1])
        o_ref[:, h * HEAD_DIM:(h + 1) * HEAD_DIM] = og.astype(BF16)


def _nsa_prompt(q, gates, cmp_c, kvb, expand, batch, t_len):
    tq = Q_TILE
    nq = t_len // tq
    nblk = t_len // CMP_BLOCK
    gw = HPG * HEAD_DIM
    qrow = lambda b, g, i: (b * nq + i, g)

    def kv_spec(col0):
        return pl.BlockSpec((t_len, HEAD_DIM), lambda b, g, i: (b, col0 + g))

    return pl.pallas_call(
        _nsa_prompt_kernel,
        grid=(batch, N_KV, nq),
        in_specs=[
            pl.BlockSpec((tq, gw), qrow),
            pl.BlockSpec((tq, LANES), qrow),
            pl.BlockSpec((nblk, HEAD_DIM), lambda b, g, i: (b, g)),
            pl.BlockSpec((nblk, HEAD_DIM), lambda b, g, i: (b, N_KV + g)),
            kv_spec(2 * N_KV), kv_spec(3 * N_KV), kv_spec(4 * N_KV), kv_spec(5 * N_KV),
            pl.BlockSpec((nblk, t_len), lambda b, g, i: (0, 0)),
        ],
        out_specs=pl.BlockSpec((tq, gw), qrow),
        out_shape=jax.ShapeDtypeStruct((batch * t_len, D_MODEL), BF16),
        scratch_shapes=[pltpu.VMEM((t_len // KEY_CHUNK, tq, KEY_CHUNK), F32)],
        compiler_params=_params("parallel", "parallel", "arbitrary"),
        name="nsa_prompt",
    )(q, gates, cmp_c, cmp_c, kvb, kvb, kvb, kvb, expand)


def _smp_cmp_kernel(q_ref, ck_ref, cv_ref, o_ref, imp_ref):
    nblk = ck_ref.shape[0]
    scale = HEAD_DIM ** -0.5
    q = q_ref[0, 0]
    rows = q.shape[0]
    tok = lax.broadcasted_iota(jnp.int32, (rows, 1), 0) % DEC_SEQ
    pos = PAST_LEN + tok
    blk = lax.broadcasted_iota(jnp.int32, (1, nblk), 1)
    vis = (blk + 1) * CMP_BLOCK - 1 <= pos
    p = _softmax_rows(jnp.where(vis, _dot_nt(q, ck_ref[...].astype(BF16)) * scale, NEG))
    p = p * jnp.where(pos >= CMP_BLOCK - 1, 1.0, 0.0)
    o_ref[0, 0] = _dot(p.astype(BF16), cv_ref[...].astype(BF16))
    imp = p[0:DEC_SEQ]
    for h in range(1, HPG):
        imp = imp + p[h * DEC_SEQ:(h + 1) * DEC_SEQ]
    imp_ref[0, 0] = imp


def _smp_cmp(qs, cmp_c, nblk):
    rows = HPG * DEC_SEQ
    return pl.pallas_call(
        _smp_cmp_kernel,
        grid=(DEC_BATCH, N_KV),
        in_specs=[
            pl.BlockSpec((1, 1, rows, HEAD_DIM), lambda b, g: (b, g, 0, 0)),
            pl.BlockSpec((nblk, HEAD_DIM), lambda b, g: (b, g)),
            pl.BlockSpec((nblk, HEAD_DIM), lambda b, g: (b, N_KV + g)),
        ],
        out_specs=[pl.BlockSpec((1, 1, rows, HEAD_DIM), lambda b, g: (b, g, 0, 0)),
                   pl.BlockSpec((1, 1, DEC_SEQ, nblk), lambda b, g: (b, g, 0, 0))],
        out_shape=[jax.ShapeDtypeStruct((DEC_BATCH, N_KV, rows, HEAD_DIM), F32),
                   jax.ShapeDtypeStruct((DEC_BATCH, N_KV, DEC_SEQ, nblk), F32)],
        compiler_params=_params("parallel", "parallel"),
        name="smp_cmp",
    )(qs, cmp_c, cmp_c)


def _smp_topk_kernel(imp_ref, idx_ref, *, n_sel_blocks):
    imp = imp_ref[...]
    rows, ncmp = imp.shape
    width = ncmp + LANES
    imp = jnp.concatenate([imp, jnp.zeros((rows, LANES), F32)], axis=1)
    j = lax.broadcasted_iota(jnp.int32, (rows, width), 1)
    tok = lax.broadcasted_iota(jnp.int32, (rows, 1), 0) % DEC_SEQ
    cur = (PAST_LEN + tok) // SEL_BLOCK
    forced = (j == 0) | (j == cur) | (j == cur - 1)
    score = jnp.where(forced, FORCED, jnp.where(j <= cur, imp, -1.0))
    score = jnp.where(j < n_sel_blocks, score, -jnp.inf)
    lane = lax.broadcasted_iota(jnp.int32, (rows, LANES), 1)
    out = jnp.zeros((rows, LANES), jnp.int32)
    jf = j.astype(F32)
    for n in range(min(TOP_N, n_sel_blocks)):
        m = jnp.max(score, axis=-1, keepdims=True)
        pick = jnp.min(jnp.where(score == m, jf, float(width)), axis=-1, keepdims=True)
        out = jnp.where(lane == n, pick.astype(jnp.int32), out)
        score = jnp.where(jf == pick, -jnp.inf, score)
    idx_ref[...] = out


def _smp_topk(imp, n_sel_blocks):
    rows = imp.shape[0]
    return pl.pallas_call(
        functools.partial(_smp_topk_kernel, n_sel_blocks=n_sel_blocks),
        out_shape=jax.ShapeDtypeStruct((rows, LANES), jnp.int32),
        name="smp_topk",
    )(imp)


def _smp_attn_kernel(idx_ref, pt_ref, *refs, n_cache_blocks):
    del pt_ref
    n_sel = TOP_N
    q_ref = refs[0]
    kblk = refs[1:1 + n_sel]
    vblk = refs[1 + n_sel:1 + 2 * n_sel]
    (nsk_ref, nsv_ref, nwk_ref, nwv_ref, cwk_ref, cwv_ref,
     oslc_ref, owin_ref, ks_ref, vs_ref, kw_ref, vw_ref) = refs[1 + 2 * n_sel:]
    b, g, t = pl.program_id(0), pl.program_id(1), pl.program_id(2)
    scale = HEAD_DIM ** -0.5
    pos = PAST_LEN + t
    q = q_ref[0, 0, 0]
    n_new = nsk_ref.shape[1]
    base = ((b * N_KV + g) * DEC_SEQ + t) * n_sel

    n_sel_keys = n_sel * SEL_BLOCK
    blk_lane = lax.broadcasted_iota(jnp.int32, (1, n_sel_keys + LANES), 1)
    ids = jnp.zeros((1, n_sel_keys + LANES), jnp.int32)
    has_new = jnp.zeros((1, 1), jnp.int32)
    for n in range(n_sel):
        bid = idx_ref[base + n]
        ks_ref[n * SEL_BLOCK:(n + 1) * SEL_BLOCK, :] = kblk[n][0]
        vs_ref[n * SEL_BLOCK:(n + 1) * SEL_BLOCK, :] = vblk[n][0]
        ids = jnp.where(blk_lane // SEL_BLOCK == n, bid, ids)
        has_new = jnp.maximum(has_new, jnp.where(bid == n_cache_blocks, 1, 0))
    ks_ref[n_sel_keys:, :] = jnp.zeros((LANES, HEAD_DIM), F32)
    vs_ref[n_sel_keys:, :] = jnp.zeros((LANES, HEAD_DIM), F32)
    ks_ref[n_sel_keys:n_sel_keys + n_new, :] = nsk_ref[0]
    vs_ref[n_sel_keys:n_sel_keys + n_new, :] = nsv_ref[0]
    in_cache = blk_lane < n_sel_keys
    off_new = blk_lane - n_sel_keys
    tok = jnp.where(in_cache, ids * SEL_BLOCK + blk_lane % SEL_BLOCK, n_cache_blocks * SEL_BLOCK + off_new)
    src_ok = jnp.where(in_cache, jnp.where(ids != n_cache_blocks, 1, 0),
                       jnp.where(off_new < DEC_SEQ, 1, 0) * has_new)
    ok = (src_ok > 0) & (tok <= pos)
    ps = _softmax_rows(jnp.where(ok, _dot_nt(q, ks_ref[...].astype(BF16)) * scale, NEG))
    oslc_ref[0, 0, 0] = _dot(ps.astype(BF16), vs_ref[...].astype(BF16))

    n_buf = cwk_ref.shape[1]
    kw_ref[0:n_buf, :] = cwk_ref[0]
    vw_ref[0:n_buf, :] = cwv_ref[0]
    kw_ref[n_buf:, :] = jnp.zeros((LANES, HEAD_DIM), F32)
    vw_ref[n_buf:, :] = jnp.zeros((LANES, HEAD_DIM), F32)
    kw_ref[n_buf:n_buf + n_new, :] = nwk_ref[0]
    vw_ref[n_buf:n_buf + n_new, :] = nwv_ref[0]
    lane = lax.broadcasted_iota(jnp.int32, (1, n_buf + LANES), 1)
    kpos = jnp.where(lane < n_buf, PAST_LEN - n_buf + lane, PAST_LEN + lane - n_buf)
    real = (lane < n_buf) | (lane - n_buf < DEC_SEQ)
    d = pos - kpos
    okw = real & (d >= 0) & (d < WINDOW)
    pw = _softmax_rows(jnp.where(okw, _dot_nt(q, kw_ref[...].astype(BF16)) * scale, NEG))
    owin_ref[0, 0, 0] = _dot(pw.astype(BF16), vw_ref[...].astype(BF16))


def _smp_attn(idx_flat, pt_flat, q8, cache_slc_blocks, new_pad, cache_win, n_pages, n_cache_blocks):
    n_sel = TOP_N
    per_page = PAGE_SIZE // SEL_BLOCK
    n_buf = cache_win.shape[1]

    def blk_spec(n, col0):
        def imap(b, g, t, idx, pt):
            bid = jnp.minimum(idx[((b * N_KV + g) * DEC_SEQ + t) * n_sel + n], n_cache_blocks - 1)
            return (pt[b * n_pages + bid // per_page] * per_page + bid % per_page, 0, col0 + g)
        return pl.BlockSpec((1, SEL_BLOCK, HEAD_DIM), imap)

    def new_spec(col0):
        return pl.BlockSpec((1, new_pad.shape[1], HEAD_DIM), lambda b, g, t, idx, pt: (b, 0, col0 + g))

    def win_spec(col0):
        return pl.BlockSpec((1, n_buf, HEAD_DIM), lambda b, g, t, idx, pt: (b, 0, col0 + g))

    o_spec = pl.BlockSpec((1, 1, 1, 8, HEAD_DIM), lambda b, g, t, idx, pt: (b, g, t, 0, 0))
    grid_spec = pltpu.PrefetchScalarGridSpec(
        num_scalar_prefetch=2,
        grid=(DEC_BATCH, N_KV, DEC_SEQ),
        in_specs=([o_spec] + [blk_spec(n, 0) for n in range(n_sel)] + [blk_spec(n, N_KV) for n in range(n_sel)]
                  + [new_spec(2 * N_KV), new_spec(3 * N_KV), new_spec(4 * N_KV), new_spec(5 * N_KV),
                     win_spec(0), win_spec(N_KV)]),
        out_specs=[o_spec, o_spec],
        scratch_shapes=[pltpu.VMEM((n_sel * SEL_BLOCK + LANES, HEAD_DIM), F32),
                        pltpu.VMEM((n_sel * SEL_BLOCK + LANES, HEAD_DIM), F32),
                        pltpu.VMEM((n_buf + LANES, HEAD_DIM), F32),
                        pltpu.VMEM((n_buf + LANES, HEAD_DIM), F32)],
    )
    shape = jax.ShapeDtypeStruct((DEC_BATCH, N_KV, DEC_SEQ, 8, HEAD_DIM), F32)
    return pl.pallas_call(
        functools.partial(_smp_attn_kernel, n_cache_blocks=n_cache_blocks),
        grid_spec=grid_spec,
        out_shape=[shape, shape],
        compiler_params=_params("arbitrary", "arbitrary", "arbitrary"),
        name="smp_attn",
    )(idx_flat, pt_flat, q8, *([cache_slc_blocks] * (2 * n_sel)), new_pad, new_pad, new_pad, new_pad,
      cache_win, cache_win)


def _gate_combine_kernel(gt_ref, oc_ref, os_ref, ow_ref, o_ref):
    gt = gt_ref[...]
    for g in range(N_KV):
        for h in range(HPG):
            c0 = g * LANES + h * N_BRANCH
            lo = (g * HPG + h) * HEAD_DIM
            hi = lo + HEAD_DIM
            og = (gt[:, c0:c0 + 1] * oc_ref[:, lo:hi] + gt[:, c0 + 1:c0 + 2] * os_ref[:, lo:hi]
                  + gt[:, c0 + 2:c0 + 3] * ow_ref[:, lo:hi])
            o_ref[:, lo:hi] = og.astype(BF16)


def _gate_combine(gates, o_cmp, o_slc, o_win):
    return pl.pallas_call(
        _gate_combine_kernel,
        out_shape=jax.ShapeDtypeStruct(o_cmp.shape, BF16),
        name="gate_combine",
    )(gates, o_cmp, o_slc, o_win)


def _prep_weights(ln_g, ln_b, ffn_w_in, ffn_w_out, gmlp_w_in, gmlp_b_in, gmlp_ln_g, gmlp_ln_b, gmlp_w_s,
                  gmlp_b_s, gmlp_w_out, nsa_w_qg, nsa_w_o, w_kv, cmp_pos, w_cmp):
    nq = N_HEADS * HEAD_DIM
    wg = nsa_w_qg[:, :, nq:].reshape(N_B_LAYERS, D_MODEL, N_KV, GATE_COLS)
    wg = jnp.pad(wg, ((0, 0), (0, 0), (0, 0), (0, LANES - GATE_COLS))).reshape(N_B_LAYERS, D_MODEL, N_KV * LANES)
    pos_t = jnp.broadcast_to(cmp_pos[:, :, None, :], (CMP_BLOCK, 2, N_KV, HEAD_DIM)).reshape(CMP_BLOCK, KV_COLS)
    return dict(
        ln_g=ln_g.reshape(DEPTH, 3, 1, D_MODEL), ln_b=ln_b.reshape(DEPTH, 3, 1, D_MODEL),
        ffn_w_in=ffn_w_in.astype(BF16), ffn_w_out=ffn_w_out.astype(BF16),
        gmlp_w_in=gmlp_w_in.astype(BF16), gmlp_b_in=gmlp_b_in.reshape(N_A_LAYERS, 1, 2 * D_GATE),
        gmlp_ln_g=gmlp_ln_g.reshape(N_A_LAYERS, 1, D_GATE), gmlp_ln_b=gmlp_ln_b.reshape(N_A_LAYERS, 1, D_GATE),
        gmlp_w_s=gmlp_w_s, gmlp_b_s=gmlp_b_s, gmlp_w_out=gmlp_w_out.astype(BF16),
        w_q=nsa_w_qg[:, :, :nq].astype(BF16), w_g=wg.astype(BF16), w_o=nsa_w_o.astype(BF16),
        w_kv=w_kv.astype(BF16),
        pos_t=jnp.concatenate([pos_t] * (PAGE_SIZE // CMP_BLOCK), axis=0),
        w_cmp=w_cmp.astype(BF16),
    )


def _trunk(x, w, tables, pos_rows, spatial, attend):
    xb = x.astype(BF16)
    v_rows, ctx = [], None
    for l in range(DEPTH):
        if l == N_A_LAYERS:
            ctx = _kv_proj(xb, w["w_kv"], tables, pos_rows)
        x, xb = _ffn(x, xb, w["ffn_w_in"], w["ffn_w_out"], l, 0, w["ln_g"][l, 0], w["ln_b"][l, 0])
        if l < N_A_LAYERS:
            rows, causal, w_sp, b_sp = spatial(l)
            z = _gmlp_in(xb, w["gmlp_w_in"][l], w["gmlp_b_in"][l], w["gmlp_ln_g"][l], w["gmlp_ln_b"][l])
            v_rows.append(z[:, D_GATE:])
            x, xb = _gmlp_out(z, x, w_sp, b_sp, w["gmlp_w_out"][l], w["ln_g"][l, 1], w["ln_b"][l, 1], rows, causal)
        else:
            o = attend(l - N_A_LAYERS, xb, ctx)
            x, xb = _proj_norm(o, x, w["w_o"][l - N_A_LAYERS], w["ln_g"][l, 1], w["ln_b"][l, 1])
        x, xb = _ffn(x, xb, w["ffn_w_in"], w["ffn_w_out"], l, 1, w["ln_g"][l, 2], w["ln_b"][l, 2])
    return x, ctx, v_rows


def kernel(x_prompt, x_sample, cache_cmp_kv, cache_slc_kv, cache_win_kv, page_table, ln_g, ln_b, ffn_w_in,
           ffn_w_out, gmlp_w_in, gmlp_b_in, gmlp_ln_g, gmlp_ln_b, gmlp_w_s, gmlp_b_s, gmlp_w_out, nsa_w_qg,
           nsa_w_o, w_kv, cmp_pos, w_cmp):
    w = _prep_weights(ln_g, ln_b, ffn_w_in, ffn_w_out, gmlp_w_in, gmlp_b_in, gmlp_ln_g, gmlp_ln_b, gmlp_w_s,
                      gmlp_b_s, gmlp_w_out, nsa_w_qg, nsa_w_o, w_kv, cmp_pos, w_cmp)
    kv_shape = (2, N_KV, HEAD_DIM)

    mp = BATCH * SEQ
    tables_p = _rotary_tables(np.arange(SEQ))
    n_slab_p = mp // PAGE_SIZE
    n_blk_p = SEQ // CMP_BLOCK
    expand = (np.arange(SEQ)[None, :] // SEL_BLOCK == np.arange(n_blk_p)[:, None])
    expand = jnp.asarray(expand, BF16)
    cmp_p = {}

    def spatial_p(l):
        return CHUNK, CHUNK, w["gmlp_w_s"][l], w["gmlp_b_s"][l].T

    def attend_p(bl, xb, ctx):
        kv, kvb = ctx
        if "c" not in cmp_p:
            cmp_p["c"] = _compress(kv, jnp.arange(n_slab_p, dtype=jnp.int32), w["pos_t"], w["w_cmp"])
        q, gates = _qg_proj(xb, w["w_q"][bl], w["w_g"][bl], tables_p, SEQ)
        return _nsa_prompt(q, gates, cmp_p["c"], kvb, expand, BATCH, SEQ)

    y_p, (kv_p, _), _ = _trunk(x_prompt.reshape(mp, D_MODEL), w, tables_p, SEQ, spatial_p, attend_p)
    kv_p = kv_p.reshape(BATCH, SEQ, N_BRANCH, *kv_shape)
    p_cmp, p_slc, p_win = kv_p[:, :, 0], kv_p[:, :, 1], kv_p[:, -min(WINDOW, SEQ):, 2]

    ms = DEC_BATCH * DEC_SEQ
    pos_s = PAST_LEN + np.arange(DEC_SEQ)
    tables_s = _rotary_tables(np.tile(pos_s, DEC_BATCH))
    n_pages = PAST_LEN // PAGE_SIZE
    n_cache_blocks = PAST_LEN // SEL_BLOCK
    n_sel_blocks = -(-(PAST_LEN + DEC_SEQ) // SEL_BLOCK)
    n_cmp_s = (PAST_LEN + DEC_SEQ) // CMP_BLOCK
    pt_flat = page_table.reshape(-1)
    cache_cmp_rows = cache_cmp_kv.reshape(-1, KV_COLS)
    cache_slc_blocks = cache_slc_kv.reshape(-1, SEL_BLOCK, KV_COLS)
    cache_win = cache_win_kv.reshape(DEC_BATCH, -1, KV_COLS)
    cmp_s = {}

    def spatial_s(l):
        c = min(DEC_SEQ, CHUNK)
        w_t = jnp.tile(w["gmlp_w_s"][l][:, :c, :c], (1, DEC_BATCH, DEC_BATCH))
        b_t = jnp.tile(w["gmlp_b_s"][l][:, :c].T, (DEC_BATCH, 1))
        return ms, c, w_t, b_t

    def attend_s(bl, xb, ctx):
        kv, _ = ctx
        if "c" not in cmp_s:
            cmp_s["c"] = _compress(cache_cmp_rows, pt_flat, w["pos_t"], w["w_cmp"])
            new = kv.reshape(DEC_BATCH, DEC_SEQ, N_BRANCH * KV_COLS)
            cmp_s["new"] = jnp.pad(new, ((0, 0), (0, 8 - DEC_SEQ), (0, 0)))
        q, gates = _qg_proj(xb, w["w_q"][bl], w["w_g"][bl], tables_s, ms)
        q5 = q.reshape(DEC_BATCH, DEC_SEQ, N_KV, HPG, HEAD_DIM)
        qs = q5.transpose(0, 2, 3, 1, 4).reshape(DEC_BATCH, N_KV, HPG * DEC_SEQ, HEAD_DIM)
        o_cmp, imp = _smp_cmp(qs, cmp_s["c"], n_cmp_s)
        idx = _smp_topk(imp.reshape(DEC_BATCH * N_KV * DEC_SEQ, n_cmp_s), n_sel_blocks)[:, :TOP_N]
        q8 = jnp.pad(q5.transpose(0, 2, 1, 3, 4), ((0, 0), (0, 0), (0, 0), (0, 8 - HPG), (0, 0)))
        o_slc, o_win = _smp_attn(idx.reshape(-1), pt_flat, q8, cache_slc_blocks, cmp_s["new"], cache_win,
                                 n_pages, n_cache_blocks)
        o_cmp = o_cmp.reshape(DEC_BATCH, N_KV, HPG, DEC_SEQ, HEAD_DIM).transpose(0, 3, 1, 2, 4).reshape(ms, D_MODEL)
        o_slc = o_slc[:, :, :, :HPG].transpose(0, 2, 1, 3, 4).reshape(ms, D_MODEL)
        o_win = o_win[:, :, :, :HPG].transpose(0, 2, 1, 3, 4).reshape(ms, D_MODEL)
        return _gate_combine(gates, o_cmp, o_slc, o_win)

    y_s, (kv_s, _), v_s = _trunk(x_sample.reshape(ms, D_MODEL), w, tables_s, ms, spatial_s, attend_s)
    kv_s = kv_s.reshape(DEC_BATCH, DEC_SEQ, N_BRANCH, *kv_shape)
    s_cmp, s_slc = kv_s[:, :, 0], kv_s[:, :, 1]
    s_win = jnp.concatenate([cache_win_kv, kv_s[:, :, 2]], axis=1)[:, -min(WINDOW, PAST_LEN + DEC_SEQ):]
    s_v = jnp.stack([v.reshape(DEC_BATCH, DEC_SEQ, D_GATE) for v in v_s])

    return (y_p.reshape(BATCH, SEQ, D_MODEL), y_s.reshape(DEC_BATCH, DEC_SEQ, D_MODEL), p_cmp, p_slc, p_win,
            s_cmp, s_slc, s_win, s_v)
```

```python
import functools
import math

import numpy as np
import jax
import jax.numpy as jnp
from jax import lax
from jax.experimental import pallas as pl
from jax.experimental.pallas import tpu as pltpu

D_MODEL = 2048
BATCH = 4
SEQ = 2048
DEPTH = 4
DEC_BATCH = 8
DEC_SEQ = 4
PAST_LEN = 16384
PAGE_SIZE = 128

N_A_LAYERS = DEPTH // 2
N_B_LAYERS = DEPTH - N_A_LAYERS
D_FF = 5632
D_GATE = D_MODEL
CHUNK = 128
N_GROUPS_A = 8
GROUP_DIM_A = D_GATE // N_GROUPS_A
HEAD_DIM = 128
N_HEADS = D_MODEL // HEAD_DIM
N_KV = 4
HPG = N_HEADS // N_KV
ROT_DIM = HEAD_DIM // 4
ROPE_THETA = 500000.0
CMP_BLOCK = 64
SEL_BLOCK = 64
TOP_N = 16
WINDOW = 512
N_BRANCH = 3
ALPHA = (2 * DEPTH) ** 0.25
LN_EPS = 1e-5
NEG = -1e30
FORCED = 1e4

F32 = jnp.float32
BF16 = jnp.bfloat16

V7X_VMEM_LIMIT_BYTES = 56 * 1024 * 1024
LOG2E = math.log2(math.e)
LANES = 128
KV_COLS = 2 * N_KV * HEAD_DIM
GATE_COLS = HPG * N_BRANCH
KEY_CHUNK = 512
PAGES_PER_STEP = 16
Q_TILE = 128


def _params(*sem):
    return pltpu.CompilerParams(dimension_semantics=sem, vmem_limit_bytes=V7X_VMEM_LIMIT_BYTES)


def _row_tile(m, cap=512):
    return m if m <= cap else cap


def _dot(a, b):
    return jnp.dot(a, b, preferred_element_type=F32)


def _dot_nt(a, b):
    return lax.dot_general(a, b, (((1,), (1,)), ((), ())), preferred_element_type=F32)


def _dot_tn(a, b):
    return lax.dot_general(a, b, (((0,), (0,)), ((), ())), preferred_element_type=F32)


def _layer_norm(x, g, b):
    mu = jnp.mean(x, axis=-1, keepdims=True)
    d = x - mu
    var = jnp.mean(d * d, axis=-1, keepdims=True)
    return d * lax.rsqrt(var + LN_EPS) * g + b


def _sigmoid(x):
    return 1.0 / (1.0 + jnp.exp(-x))


def _softmax_rows(s):
    m = jnp.max(s, axis=-1, keepdims=True)
    p = jnp.exp(s - m)
    return p / jnp.sum(p, axis=-1, keepdims=True)


def _rotate(x, cos_t, sin_lo, sin_hi):
    half = ROT_DIM // 2
    return (x * cos_t + pltpu.roll(x, LANES - half, 1) * sin_lo + pltpu.roll(x, half, 1) * sin_hi)


def _rotary_tables(pos):
    half = ROT_DIM // 2
    inv = ROPE_THETA ** (-jnp.arange(half, dtype=F32) / half)
    ang = jnp.asarray(pos).astype(F32)[:, None] * inv[None, :]
    cos, sin = jnp.cos(ang), jnp.sin(ang)
    rows = ang.shape[0]
    pad = jnp.zeros((rows, LANES - ROT_DIM), F32)
    zero = jnp.zeros((rows, half), F32)
    cos_t = jnp.concatenate([cos, cos, pad + 1.0], axis=1)
    sin_lo = jnp.concatenate([-sin, zero, pad], axis=1)
    sin_hi = jnp.concatenate([zero, sin, pad], axis=1)
    return cos_t, sin_lo, sin_hi


def _ffn_kernel(xb_ref, x_ref, wg_ref, wu_ref, wo_ref, g_ref, b_ref, y_ref, yb_ref):
    f = pl.program_id(1)

    @pl.when(f == 0)
    def _():
        y_ref[...] = jnp.zeros_like(y_ref)

    xb = xb_ref[...]
    gate = _dot(xb, wg_ref[...])
    up = _dot(xb, wu_ref[...])
    h = gate * _sigmoid(gate) * up
    y_ref[...] += _dot(h.astype(BF16), wo_ref[...])

    @pl.when(f == pl.num_programs(1) - 1)
    def _():
        y = _layer_norm(ALPHA * x_ref[...] + 0.5 * y_ref[...], g_ref[...], b_ref[...])
        y_ref[...] = y
        yb_ref[...] = y.astype(BF16)


def _ffn(x, xb, w_in, w_out, l, s, g, b):
    m = x.shape[0]
    tm = _row_tile(m, 1024)
    tf = 512
    nf = D_FF // tf
    row = lambda i, f: (i, 0)
    once = pl.Buffered(1)
    return pl.pallas_call(
        _ffn_kernel,
        grid=(m // tm, nf),
        in_specs=[
            pl.BlockSpec((tm, D_MODEL), row),
            pl.BlockSpec((tm, D_MODEL), row, pipeline_mode=once),
            pl.BlockSpec((None, None, D_MODEL, tf), lambda i, f: (l, s, 0, f)),
            pl.BlockSpec((None, None, D_MODEL, tf), lambda i, f: (l, s, 0, nf + f)),
            pl.BlockSpec((None, None, tf, D_MODEL), lambda i, f: (l, s, f, 0)),
            pl.BlockSpec((1, D_MODEL), lambda i, f: (0, 0)),
            pl.BlockSpec((1, D_MODEL), lambda i, f: (0, 0)),
        ],
        out_specs=[pl.BlockSpec((tm, D_MODEL), row, pipeline_mode=once),
                   pl.BlockSpec((tm, D_MODEL), row, pipeline_mode=once)],
        out_shape=[jax.ShapeDtypeStruct((m, D_MODEL), F32), jax.ShapeDtypeStruct((m, D_MODEL), BF16)],
        compiler_params=_params("parallel", "arbitrary"),
        name="ffn",
    )(xb, x, w_in, w_in, w_out, g, b)


def _gmlp_in_kernel(xb_ref, w_ref, bias_ref, lg_ref, lb_ref, z_ref):
    n = pl.program_id(0)
    z = _dot(xb_ref[...], w_ref[...]) + bias_ref[...]
    z = 0.5 * z * (1.0 + lax.erf(z * (1.0 / math.sqrt(2.0))))

    @pl.when(n == 0)
    def _():
        z_ref[...] = z

    @pl.when(n == 1)
    def _():
        z_ref[...] = _layer_norm(z, lg_ref[...], lb_ref[...])


def _gmlp_in(xb, w, bias, lg, lb):
    m = xb.shape[0]
    tm = _row_tile(m, 256)
    return pl.pallas_call(
        _gmlp_in_kernel,
        grid=(2, m // tm),
        in_specs=[
            pl.BlockSpec((tm, D_MODEL), lambda n, i: (i, 0)),
            pl.BlockSpec((D_MODEL, D_GATE), lambda n, i: (0, n)),
            pl.BlockSpec((1, D_GATE), lambda n, i: (0, n)),
            pl.BlockSpec((1, D_GATE), lambda n, i: (0, 0)),
            pl.BlockSpec((1, D_GATE), lambda n, i: (0, 0)),
        ],
        out_specs=pl.BlockSpec((tm, D_GATE), lambda n, i: (i, n)),
        out_shape=jax.ShapeDtypeStruct((m, 2 * D_GATE), F32),
        compiler_params=_params("arbitrary", "arbitrary"),
        name="gmlp_in",
    )(xb, w, bias, lg, lb)


def _gmlp_out_kernel(u_ref, v_ref, x_ref, ws_ref, bs_ref, wo_ref, g_ref, b_ref, y_ref, yb_ref, *, rows, causal):
    tm = u_ref.shape[0]
    r_i = lax.broadcasted_iota(jnp.int32, (rows, rows), 0)
    c_i = lax.broadcasted_iota(jnp.int32, (rows, rows), 1)
    keep = ((r_i // causal) == (c_i // causal)) & (c_i <= r_i)
    w_sp = [jnp.where(keep, ws_ref[g], 0.0).astype(BF16) for g in range(N_GROUPS_A)]
    bs = bs_ref[...]
    chunks = []
    for ch in range(tm // rows):
        u = u_ref[ch * rows:(ch + 1) * rows, :]
        v = v_ref[ch * rows:(ch + 1) * rows, :]
        parts = []
        for g in range(N_GROUPS_A):
            lo, hi = g * GROUP_DIM_A, (g + 1) * GROUP_DIM_A
            mixed = _dot(w_sp[g], v[:, lo:hi].astype(BF16)) + bs[:, g:g + 1]
            parts.append((u[:, lo:hi] * mixed).astype(BF16))
        chunks.append(jnp.concatenate(parts, axis=1))
    y = chunks[0] if len(chunks) == 1 else jnp.concatenate(chunks, axis=0)
    f = _dot(y, wo_ref[...])
    out = _layer_norm(ALPHA * x_ref[...] + f, g_ref[...], b_ref[...])
    y_ref[...] = out
    yb_ref[...] = out.astype(BF16)


def _gmlp_out(z, x, w_sp, b_sp, w_out, g, b, rows, causal):
    m = x.shape[0]
    tm = _row_tile(m)
    row = lambda i: (i, 0)
    kern = functools.partial(_gmlp_out_kernel, rows=rows, causal=causal)
    return pl.pallas_call(
        kern,
        grid=(m // tm,),
        in_specs=[
            pl.BlockSpec((tm, D_GATE), lambda i: (i, 0)),
            pl.BlockSpec((tm, D_GATE), lambda i: (i, 1)),
            pl.BlockSpec((tm, D_MODEL), row),
            pl.BlockSpec((N_GROUPS_A, rows, rows), lambda i: (0, 0, 0)),
            pl.BlockSpec((rows, N_GROUPS_A), lambda i: (0, 0)),
            pl.BlockSpec((D_GATE, D_MODEL), lambda i: (0, 0)),
            pl.BlockSpec((1, D_MODEL), lambda i: (0, 0)),
            pl.BlockSpec((1, D_MODEL), lambda i: (0, 0)),
        ],
        out_specs=[pl.BlockSpec((tm, D_MODEL), row), pl.BlockSpec((tm, D_MODEL), row)],
        out_shape=[jax.ShapeDtypeStruct((m, D_MODEL), F32), jax.ShapeDtypeStruct((m, D_MODEL), BF16)],
        compiler_params=_params("parallel"),
        name="gmlp_out",
    )(z, z, x, w_sp, b_sp, w_out, g, b)


def _proj_norm_kernel(a_ref, x_ref, w_ref, g_ref, b_ref, y_ref, yb_ref):
    out = _layer_norm(ALPHA * x_ref[...] + _dot(a_ref[...], w_ref[...]), g_ref[...], b_ref[...])
    y_ref[...] = out
    yb_ref[...] = out.astype(BF16)


def _proj_norm(a, x, w, g, b):
    m = x.shape[0]
    tm = _row_tile(m)
    row = lambda i: (i, 0)
    return pl.pallas_call(
        _proj_norm_kernel,
        grid=(m // tm,),
        in_specs=[
            pl.BlockSpec((tm, D_MODEL), row),
            pl.BlockSpec((tm, D_MODEL), row),
            pl.BlockSpec((D_MODEL, D_MODEL), lambda i: (0, 0)),
            pl.BlockSpec((1, D_MODEL), lambda i: (0, 0)),
            pl.BlockSpec((1, D_MODEL), lambda i: (0, 0)),
        ],
        out_specs=[pl.BlockSpec((tm, D_MODEL), row), pl.BlockSpec((tm, D_MODEL), row)],
        out_shape=[jax.ShapeDtypeStruct((m, D_MODEL), F32), jax.ShapeDtypeStruct((m, D_MODEL), BF16)],
        compiler_params=_params("parallel"),
        name="proj_norm",
    )(a, x, w, g, b)


def _kv_proj_kernel(xb_ref, w_ref, cos_ref, slo_ref, shi_ref, kv_ref, kvb_ref):
    n = pl.program_id(1)
    acc = _dot(xb_ref[...], w_ref[...])

    @pl.when(n % 2 == 0)
    def _():
        cos_t, s_lo, s_hi = cos_ref[...], slo_ref[...], shi_ref[...]
        heads = [_rotate(acc[:, h * HEAD_DIM:(h + 1) * HEAD_DIM], cos_t, s_lo, s_hi) for h in range(N_KV)]
        rot = jnp.concatenate(heads, axis=1)
        kv_ref[...] = rot
        kvb_ref[...] = rot.astype(BF16)

    @pl.when(n % 2 == 1)
    def _():
        kv_ref[...] = acc
        kvb_ref[...] = acc.astype(BF16)


def _kv_proj(xb, w, tables, pos_rows):
    m = xb.shape[0]
    tm = _row_tile(m)
    half = N_KV * HEAD_DIM
    ncol = N_BRANCH * 2
    nt = pos_rows // tm
    tab = lambda i, n: (i % nt, 0)
    return pl.pallas_call(
        _kv_proj_kernel,
        grid=(m // tm, ncol),
        in_specs=[
            pl.BlockSpec((tm, D_MODEL), lambda i, n: (i, 0)),
            pl.BlockSpec((D_MODEL, half), lambda i, n: (0, n)),
            pl.BlockSpec((tm, LANES), tab),
            pl.BlockSpec((tm, LANES), tab),
            pl.BlockSpec((tm, LANES), tab),
        ],
        out_specs=[pl.BlockSpec((tm, half), lambda i, n: (i, n)), pl.BlockSpec((tm, half), lambda i, n: (i, n))],
        out_shape=[jax.ShapeDtypeStruct((m, ncol * half), F32), jax.ShapeDtypeStruct((m, ncol * half), BF16)],
        compiler_params=_params("parallel", "arbitrary"),
        name="kv_proj",
    )(xb, w, *tables)


def _qg_proj_kernel(xb_ref, wq_ref, wg_ref, cos_ref, slo_ref, shi_ref, q_ref, gt_ref):
    xb = xb_ref[...]
    acc = _dot(xb, wq_ref[...])
    cos_t, s_lo, s_hi = cos_ref[...], slo_ref[...], shi_ref[...]
    for h in range(N_HEADS):
        lo, hi = h * HEAD_DIM, (h + 1) * HEAD_DIM
        q_ref[:, lo:hi] = _rotate(acc[:, lo:hi], cos_t, s_lo, s_hi).astype(BF16)
    gt_ref[...] = _sigmoid(_dot(xb, wg_ref[...]))


def _qg_proj(xb, wq, wg, tables, pos_rows):
    m = xb.shape[0]
    tm = _row_tile(m)
    nt = pos_rows // tm
    tab = lambda i: (i % nt, 0)
    gcols = N_KV * LANES
    return pl.pallas_call(
        _qg_proj_kernel,
        grid=(m // tm,),
        in_specs=[
            pl.BlockSpec((tm, D_MODEL), lambda i: (i, 0)),
            pl.BlockSpec((D_MODEL, D_MODEL), lambda i: (0, 0)),
            pl.BlockSpec((D_MODEL, gcols), lambda i: (0, 0)),
            pl.BlockSpec((tm, LANES), tab),
            pl.BlockSpec((tm, LANES), tab),
            pl.BlockSpec((tm, LANES), tab),
        ],
        out_specs=[pl.BlockSpec((tm, D_MODEL), lambda i: (i, 0)), pl.BlockSpec((tm, gcols), lambda i: (i, 0))],
        out_shape=[jax.ShapeDtypeStruct((m, D_MODEL), BF16), jax.ShapeDtypeStruct((m, gcols), F32)],
        compiler_params=_params("parallel"),
        name="qg_proj",
    )(xb, wq, wg, *tables)


def _compress_rows_kernel(x_ref, pos_ref, w_ref, out_ref, buf_ref):
    ntok = x_ref.shape[0]
    nblk = ntok // CMP_BLOCK
    for p in range(ntok // PAGE_SIZE):
        rows = slice(p * PAGE_SIZE, (p + 1) * PAGE_SIZE)
        for cb in range(2 * N_KV):
            lanes = slice(cb * HEAD_DIM, (cb + 1) * HEAD_DIM)
            buf_ref[cb, rows, :] = x_ref[rows, lanes] + pos_ref[:, lanes]

    def body(l, accs):
        new = []
        for c in range(2):
            lhs = jnp.concatenate(
                [buf_ref[c * N_KV + g, pl.ds(l, nblk, stride=CMP_BLOCK), :] for g in range(N_KV)],
                axis=0)
            new.append(accs[c] + _dot(lhs.astype(BF16), w_ref[l, c]))
        return tuple(new)

    zero = jnp.zeros((N_KV * nblk, HEAD_DIM), F32)
    accs = lax.fori_loop(0, CMP_BLOCK, body, (zero, zero), unroll=8)
    for c in range(2):
        for g in range(N_KV):
            col = (c * N_KV + g) * HEAD_DIM
            out_ref[:, col:col + HEAD_DIM] = accs[c][g * nblk:(g + 1) * nblk, :]


def _compress_rows(rows, pos_t, w_cmp_b):
    ntok = PAGES_PER_STEP * PAGE_SIZE
    nblk = ntok // CMP_BLOCK
    return pl.pallas_call(
        _compress_rows_kernel,
        grid=(rows.shape[0] // ntok,),
        in_specs=[
            pl.BlockSpec((ntok, KV_COLS), lambda i: (i, 0)),
            pl.BlockSpec((PAGE_SIZE, KV_COLS), lambda i: (0, 0)),
            pl.BlockSpec((CMP_BLOCK, 2, HEAD_DIM, HEAD_DIM), lambda i: (0, 0, 0, 0)),
        ],
        out_specs=pl.BlockSpec((nblk, KV_COLS), lambda i: (i, 0)),
        out_shape=jax.ShapeDtypeStruct((rows.shape[0] // CMP_BLOCK, KV_COLS), F32),
        scratch_shapes=[pltpu.VMEM((2 * N_KV, ntok, HEAD_DIM), F32)],
        compiler_params=_params("arbitrary"),
        name="compress_rows",
    )(rows, pos_t, w_cmp_b)


def _compress_pages_kernel(tbl_ref, *refs):
    del tbl_ref
    npg = PAGES_PER_STEP
    pages = refs[:npg]
    pos_ref, w_ref, out_ref = refs[npg:]
    per_page = PAGE_SIZE // CMP_BLOCK
    rows = npg * per_page * 2 * N_KV
    acc = jnp.zeros((rows, 2 * HEAD_DIM), F32)
    for j in range(CMP_BLOCK // 2):
        halves = []
        for l in (2 * j, 2 * j + 1):
            pos_l = pos_ref[l]
            tiles = [pages[p][l + CMP_BLOCK * h] + pos_l for p in range(npg) for h in range(per_page)]
            halves.append(jnp.concatenate(tiles, axis=0).astype(BF16))
        acc = acc + _dot(jnp.concatenate(halves, axis=1), w_ref[j])
    is_k = (lax.broadcasted_iota(jnp.int32, (rows, HEAD_DIM), 0) % (2 * N_KV)) < N_KV
    out_ref[...] = jnp.where(is_k, acc[:, :HEAD_DIM], acc[:, HEAD_DIM:])


def _compress_pages(cache, table, pos8, w_pair):
    npg = PAGES_PER_STEP
    rows = npg * (PAGE_SIZE // CMP_BLOCK) * 2 * N_KV

    def page_spec(p):
        return pl.BlockSpec((PAGE_SIZE, 2 * N_KV, HEAD_DIM), lambda i, tbl: (tbl[i * npg + p], 0, 0))

    grid_spec = pltpu.PrefetchScalarGridSpec(
        num_scalar_prefetch=1,
        grid=(table.shape[0] // npg,),
        in_specs=[page_spec(p) for p in range(npg)] + [
            pl.BlockSpec((CMP_BLOCK, 2 * N_KV, HEAD_DIM), lambda i, tbl: (0, 0, 0)),
            pl.BlockSpec((CMP_BLOCK // 2, 2 * HEAD_DIM, 2 * HEAD_DIM), lambda i, tbl: (0, 0, 0)),
        ],
        out_specs=pl.BlockSpec((rows, HEAD_DIM), lambda i, tbl: (i, 0)),
    )
    return pl.pallas_call(
        _compress_pages_kernel,
        grid_spec=grid_spec,
        out_shape=jax.ShapeDtypeStruct((table.shape[0] // npg * rows, HEAD_DIM), F32),
        compiler_params=_params("arbitrary"),
        name="compress_pages",
    )(table, *([cache] * npg), pos8, w_pair)


def _rank_select(score_t, n_blocks):
    idx = lax.broadcasted_iota(jnp.int32, score_t.shape, 0)
    rank = jnp.zeros(score_t.shape, F32)
    for i in range(n_blocks):
        row = score_t[i:i + 1, :]
        rank = rank + jnp.where(idx > i, jnp.where(row >= score_t, 1.0, 0.0), jnp.where(row > score_t, 1.0, 0.0))
    return jnp.where(rank < float(min(TOP_N, n_blocks)), 1.0, 0.0)


def _nsa_prompt_kernel(q_ref, gt_ref, ck_ref, cv_ref, sk_ref, sv_ref, wk_ref, wv_ref, e_ref, o_ref, mask_ref):
    tq = q_ref.shape[0]
    t_len = sk_ref.shape[0]
    nblk = ck_ref.shape[0]
    i = pl.program_id(2)
    scale = HEAD_DIM ** -0.5
    q = q_ref[...]
    q4 = jnp.concatenate([q[:, h * HEAD_DIM:(h + 1) * HEAD_DIM] for h in range(HPG)], axis=0)
    pos_c = i * tq + lax.broadcasted_iota(jnp.int32, (tq, 1), 0)
    pos4_c = jnp.concatenate([pos_c] * HPG, axis=0)
    pos_r = i * tq + lax.broadcasted_iota(jnp.int32, (1, tq), 1)
    pos4_r = jnp.concatenate([pos_r] * HPG, axis=1)

    ck = ck_ref[...].astype(BF16)
    cv = cv_ref[...].astype(BF16)
    blk_r = lax.broadcasted_iota(jnp.int32, (1, nblk), 1)
    vis = (blk_r + 1) * CMP_BLOCK - 1 <= pos4_c
    p = _softmax_rows(jnp.where(vis, _dot_nt(q4, ck) * scale, NEG))
    p = p * jnp.where(pos4_c >= CMP_BLOCK - 1, 1.0, 0.0)
    o_cmp = _dot(p.astype(BF16), cv)

    blk_c = lax.broadcasted_iota(jnp.int32, (nblk, 1), 0)
    vis_t = (blk_c + 1) * CMP_BLOCK - 1 <= pos4_r
    s_t = jnp.where(vis_t, _dot_nt(ck, q4) * scale, NEG)
    p_t = jnp.exp(s_t - jnp.max(s_t, axis=0, keepdims=True))
    p_t = p_t / jnp.sum(p_t, axis=0, keepdims=True)
    p_t = p_t * jnp.where(pos4_r >= CMP_BLOCK - 1, 1.0, 0.0)
    imp_t = p_t[:, 0:tq]
    for h in range(1, HPG):
        imp_t = imp_t + p_t[:, h * tq:(h + 1) * tq]
    cur = pos_r // SEL_BLOCK
    forced = (blk_c == 0) | (blk_c == cur) | (blk_c == cur - 1)
    score_t = jnp.where(forced, FORCED, jnp.where(blk_c <= cur, imp_t, -1.0))
    sel_t = _rank_select(score_t, nblk)
    tok_mask = _dot_tn(sel_t.astype(BF16), e_ref[...])
    key_r = lax.broadcasted_iota(jnp.int32, (1, t_len), 1)
    tok_bias = jnp.where(key_r <= pos_c, (tok_mask - 1.0) * (-NEG), NEG)
    n_chunks_all = t_len // KEY_CHUNK
    for c in range(n_chunks_all):
        mask_ref[c] = tok_bias[:, c * KEY_CHUNK:(c + 1) * KEY_CHUNK]

    c2 = scale * LOG2E

    def body(kc, carry):
        m, l, acc = carry
        start = pl.multiple_of(kc * KEY_CHUNK, KEY_CHUNK)
        k = sk_ref[pl.ds(start, KEY_CHUNK), :]
        v = sv_ref[pl.ds(start, KEY_CHUNK), :]
        bias = mask_ref[kc]
        s = _dot_nt(q4, k) + jnp.concatenate([bias] * HPG, axis=0)
        m_new = jnp.maximum(m, jnp.max(s, axis=-1, keepdims=True))
        a = jnp.exp2((m - m_new) * c2)
        pr = jnp.exp2((s - m_new) * c2)
        l = a * l + jnp.sum(pr, axis=-1, keepdims=True)
        acc = a * acc + _dot(pr.astype(BF16), v)
        return m_new, l, acc

    n_chunks = (i * tq + tq + KEY_CHUNK - 1) // KEY_CHUNK
    init = (jnp.full((HPG * tq, 1), NEG, F32), jnp.zeros((HPG * tq, 1), F32), jnp.zeros((HPG * tq, HEAD_DIM), F32))
    _, l_s, acc_s = lax.fori_loop(0, n_chunks, body, init)
    o_slc = acc_s * (1.0 / l_s)

    n_win = WINDOW + tq
    start = pl.multiple_of(jnp.maximum(i * tq - WINDOW, 0), tq)
    kw = wk_ref[pl.ds(start, n_win), :]
    vw = wv_ref[pl.ds(start, n_win), :]
    kpos = start + lax.broadcasted_iota(jnp.int32, (1, n_win), 1)
    d = pos4_c - kpos
    ok = (d >= 0) & (d < WINDOW)
    sw = jnp.where(ok, _dot_nt(q4, kw), NEG)
    pw = jnp.exp2((sw - jnp.max(sw, axis=-1, keepdims=True)) * c2)
    o_win = _dot(pw.astype(BF16), vw) * (1.0 / jnp.sum(pw, axis=-1, keepdims=True))

    gt = gt_ref[...]
    for h in range(HPG):
        r0, r1 = h * tq, (h + 1) * tq
        c0 = h * N_BRANCH
        og = (gt[:, c0:c0 + 1] * o_cmp[r0:r1] + gt[:, c0 + 1:c0 + 2] * o_slc[r0:r1]
              + gt[:, c0 + 2:c0 + 3] * o_win[r0:r1])
        o_ref[:, h * HEAD_DIM:(h + 1) * HEAD_DIM] = og.astype(BF16)


def _nsa_prompt(q, gates, cmp_c, kvb, expand, batch, t_len):
    tq = Q_TILE
    nq = t_len // tq
    nblk = t_len // CMP_BLOCK
    gw = HPG * HEAD_DIM
    qrow = lambda b, g, i: (b * nq + i, g)

    def kv_spec(col0):
        return pl.BlockSpec((t_len, HEAD_DIM), lambda b, g, i: (b, col0 + g))

    return pl.pallas_call(
        _nsa_prompt_kernel,
        grid=(batch, N_KV, nq),
        in_specs=[
            pl.BlockSpec((tq, gw), qrow),
            pl.BlockSpec((tq, LANES), qrow),
            pl.BlockSpec((nblk, HEAD_DIM), lambda b, g, i: (b, g)),
            pl.BlockSpec((nblk, HEAD_DIM), lambda b, g, i: (b, N_KV + g)),
            kv_spec(2 * N_KV), kv_spec(3 * N_KV), kv_spec(4 * N_KV), kv_spec(5 * N_KV),
            pl.BlockSpec((nblk, t_len), lambda b, g, i: (0, 0)),
        ],
        out_specs=pl.BlockSpec((tq, gw), qrow),
        out_shape=jax.ShapeDtypeStruct((batch * t_len, D_MODEL), BF16),
        scratch_shapes=[pltpu.VMEM((t_len // KEY_CHUNK, tq, KEY_CHUNK), F32)],
        compiler_params=_params("parallel", "parallel", "arbitrary"),
        name="nsa_prompt",
    )(q, gates, cmp_c, cmp_c, kvb, kvb, kvb, kvb, expand)


def _smp_cmp_kernel(q_ref, ck_ref, cv_ref, o_ref, imp_ref):
    nblk = ck_ref.shape[0]
    scale = HEAD_DIM ** -0.5
    q = q_ref[0, 0]
    rows = q.shape[0]
    tok = lax.broadcasted_iota(jnp.int32, (rows, 1), 0) % DEC_SEQ
    pos = PAST_LEN + tok
    blk = lax.broadcasted_iota(jnp.int32, (1, nblk), 1)
    vis = (blk + 1) * CMP_BLOCK - 1 <= pos
    p = _softmax_rows(jnp.where(vis, _dot_nt(q, ck_ref[...].astype(BF16)) * scale, NEG))
    p = p * jnp.where(pos >= CMP_BLOCK - 1, 1.0, 0.0)
    o_ref[0, 0] = _dot(p.astype(BF16), cv_ref[...].astype(BF16))
    imp = p[0:DEC_SEQ]
    for h in range(1, HPG):
        imp = imp + p[h * DEC_SEQ:(h + 1) * DEC_SEQ]
    imp_ref[0, 0] = imp


def _smp_cmp(qs, cmp_c, nblk):
    rows = HPG * DEC_SEQ
    return pl.pallas_call(
        _smp_cmp_kernel,
        grid=(DEC_BATCH, N_KV),
        in_specs=[
            pl.BlockSpec((1, 1, rows, HEAD_DIM), lambda b, g: (b, g, 0, 0)),
            pl.BlockSpec((nblk, HEAD_DIM), lambda b, g: (b, g)),
            pl.BlockSpec((nblk, HEAD_DIM), lambda b, g: (b, N_KV + g)),
        ],
        out_specs=[pl.BlockSpec((1, 1, rows, HEAD_DIM), lambda b, g: (b, g, 0, 0)),
                   pl.BlockSpec((1, 1, DEC_SEQ, nblk), lambda b, g: (b, g, 0, 0))],
        out_shape=[jax.ShapeDtypeStruct((DEC_BATCH, N_KV, rows, HEAD_DIM), F32),
                   jax.ShapeDtypeStruct((DEC_BATCH, N_KV, DEC_SEQ, nblk), F32)],
        compiler_params=_params("parallel", "parallel"),
        name="smp_cmp",
    )(qs, cmp_c, cmp_c)


def _smp_topk_kernel(imp_ref, idx_ref, *, n_sel_blocks):
    imp = imp_ref[...]
    rows, ncmp = imp.shape
    width = ncmp + LANES
    imp = jnp.concatenate([imp, jnp.zeros((rows, LANES), F32)], axis=1)
    j = lax.broadcasted_iota(jnp.int32, (rows, width), 1)
    tok = lax.broadcasted_iota(jnp.int32, (rows, 1), 0) % DEC_SEQ
    cur = (PAST_LEN + tok) // SEL_BLOCK
    forced = (j == 0) | (j == cur) | (j == cur - 1)
    score = jnp.where(forced, FORCED, jnp.where(j <= cur, imp, -1.0))
    score = jnp.where(j < n_sel_blocks, score, -jnp.inf)
    lane = lax.broadcasted_iota(jnp.int32, (rows, LANES), 1)
    out = jnp.zeros((rows, LANES), jnp.int32)
    jf = j.astype(F32)
    for n in range(min(TOP_N, n_sel_blocks)):
        m = jnp.max(score, axis=-1, keepdims=True)
        pick = jnp.min(jnp.where(score == m, jf, float(width)), axis=-1, keepdims=True)
        out = jnp.where(lane == n, pick.astype(jnp.int32), out)
        score = jnp.where(jf == pick, -jnp.inf, score)
    idx_ref[...] = out


def _smp_topk(imp, n_sel_blocks):
    rows = imp.shape[0]
    return pl.pallas_call(
        functools.partial(_smp_topk_kernel, n_sel_blocks=n_sel_blocks),
        out_shape=jax.ShapeDtypeStruct((rows, LANES), jnp.int32),
        name="smp_topk",
    )(imp)


def _smp_attn_kernel(idx_ref, pt_ref, q_ref, nsk_ref, nsv_ref, nwk_ref, nwv_ref, slc_hbm, win_hbm,
                     oslc_ref, owin_ref, kbuf, vbuf, wbuf, sem, *, n_pages, n_cache_blocks):
    n_sel = TOP_N
    per_page = PAGE_SIZE // SEL_BLOCK
    n_sel_keys = n_sel * SEL_BLOCK
    n_buf = wbuf.shape[2] - LANES
    s = pl.program_id(0)
    slot = s % 2

    def copies(step, slot_, for_wait):
        b, g = step // N_KV, step % N_KV
        out = []
        for t in range(DEC_SEQ):
            for n in range(n_sel):
                if for_wait:
                    row0 = 0
                else:
                    bid = jnp.minimum(idx_ref[(step * DEC_SEQ + t) * n_sel + n], n_cache_blocks - 1)
                    row0 = (pt_ref[b * n_pages + bid // per_page] * per_page + bid % per_page) * SEL_BLOCK
                dst = pl.ds(n * SEL_BLOCK, SEL_BLOCK)
                out.append(pltpu.make_async_copy(slc_hbm.at[pl.ds(row0, SEL_BLOCK), g, :],
                                                 kbuf.at[slot_, t, dst, :], sem.at[slot_]))
                out.append(pltpu.make_async_copy(slc_hbm.at[pl.ds(row0, SEL_BLOCK), N_KV + g, :],
                                                 vbuf.at[slot_, t, dst, :], sem.at[slot_]))
        for c in range(2):
            out.append(pltpu.make_async_copy(win_hbm.at[pl.ds(b * n_buf, n_buf), c * N_KV + g, :],
                                             wbuf.at[slot_, c, pl.ds(0, n_buf), :], sem.at[slot_]))
        return out

    @pl.when(s == 0)
    def _():
        for cp in copies(s, slot, False):
            cp.start()

    @pl.when(s + 1 < pl.num_programs(0))
    def _():
        for cp in copies(s + 1, 1 - slot, False):
            cp.start()

    for cp in copies(s, slot, True):
        cp.wait()

    scale = HEAD_DIM ** -0.5
    n_new = nsk_ref.shape[1]
    zeros = jnp.zeros((LANES, HEAD_DIM), F32)
    blk_lane = lax.broadcasted_iota(jnp.int32, (1, n_sel_keys + LANES), 1)
    in_cache = blk_lane < n_sel_keys
    off_new = blk_lane - n_sel_keys
    lane = lax.broadcasted_iota(jnp.int32, (1, n_buf + LANES), 1)
    kpos = jnp.where(lane < n_buf, PAST_LEN - n_buf + lane, PAST_LEN + lane - n_buf)
    real = (lane < n_buf) | (lane - n_buf < DEC_SEQ)

    wbuf[slot, 0, pl.ds(n_buf, LANES), :] = zeros
    wbuf[slot, 1, pl.ds(n_buf, LANES), :] = zeros
    wbuf[slot, 0, pl.ds(n_buf, n_new), :] = nwk_ref[0]
    wbuf[slot, 1, pl.ds(n_buf, n_new), :] = nwv_ref[0]
    kw = wbuf[slot, 0].astype(BF16)
    vw = wbuf[slot, 1].astype(BF16)

    for t in range(DEC_SEQ):
        pos = PAST_LEN + t
        q = q_ref[0, 0, t]
        ids = jnp.zeros((1, n_sel_keys + LANES), jnp.int32)
        has_new = jnp.zeros((1, 1), jnp.int32)
        for n in range(n_sel):
            bid = idx_ref[(s * DEC_SEQ + t) * n_sel + n]
            ids = jnp.where(blk_lane // SEL_BLOCK == n, bid, ids)
            has_new = jnp.maximum(has_new, jnp.where(bid == n_cache_blocks, 1, 0))
        kbuf[slot, t, pl.ds(n_sel_keys, LANES), :] = zeros
        vbuf[slot, t, pl.ds(n_sel_keys, LANES), :] = zeros
        kbuf[slot, t, pl.ds(n_sel_keys, n_new), :] = nsk_ref[0]
        vbuf[slot, t, pl.ds(n_sel_keys, n_new), :] = nsv_ref[0]
        tok = jnp.where(in_cache, ids * SEL_BLOCK + blk_lane % SEL_BLOCK, n_cache_blocks * SEL_BLOCK + off_new)
        src_ok = jnp.where(in_cache, jnp.where(ids != n_cache_blocks, 1, 0),
                           jnp.where(off_new < DEC_SEQ, 1, 0) * has_new)
        ok = (src_ok > 0) & (tok <= pos)
        ps = _softmax_rows(jnp.where(ok, _dot_nt(q, kbuf[slot, t].astype(BF16)) * scale, NEG))
        oslc_ref[0, 0, t] = _dot(ps.astype(BF16), vbuf[slot, t].astype(BF16))

        d = pos - kpos
        okw = real & (d >= 0) & (d < WINDOW)
        pw = _softmax_rows(jnp.where(okw, _dot_nt(q, kw) * scale, NEG))
        owin_ref[0, 0, t] = _dot(pw.astype(BF16), vw)


def _smp_attn(idx_flat, pt_flat, q8, cache_slc_rows, new_pad, cache_win_rows, n_pages, n_cache_blocks, n_buf):
    n_sel = TOP_N
    n_new = new_pad.shape[1]
    bg = lambda s, idx, pt: (s // N_KV, s % N_KV, 0, 0, 0)

    def new_spec(col0):
        return pl.BlockSpec((1, n_new, HEAD_DIM), lambda s, idx, pt: (s // N_KV, 0, col0 + s % N_KV))

    qo_spec = pl.BlockSpec((1, 1, DEC_SEQ, 8, HEAD_DIM), bg)
    grid_spec = pltpu.PrefetchScalarGridSpec(
        num_scalar_prefetch=2,
        grid=(DEC_BATCH * N_KV,),
        in_specs=[qo_spec, new_spec(2 * N_KV), new_spec(3 * N_KV), new_spec(4 * N_KV), new_spec(5 * N_KV),
                  pl.BlockSpec(memory_space=pl.ANY), pl.BlockSpec(memory_space=pl.ANY)],
        out_specs=[qo_spec, qo_spec],
        scratch_shapes=[pltpu.VMEM((2, DEC_SEQ, n_sel * SEL_BLOCK + LANES, HEAD_DIM), F32),
                        pltpu.VMEM((2, DEC_SEQ, n_sel * SEL_BLOCK + LANES, HEAD_DIM), F32),
                        pltpu.VMEM((2, 2, n_buf + LANES, HEAD_DIM), F32),
                        pltpu.SemaphoreType.DMA((2,))],
    )
    shape = jax.ShapeDtypeStruct((DEC_BATCH, N_KV, DEC_SEQ, 8, HEAD_DIM), F32)
    return pl.pallas_call(
        functools.partial(_smp_attn_kernel, n_pages=n_pages, n_cache_blocks=n_cache_blocks),
        grid_spec=grid_spec,
        out_shape=[shape, shape],
        compiler_params=_params("arbitrary"),
        name="smp_attn",
    )(idx_flat, pt_flat, q8, new_pad, new_pad, new_pad, new_pad, cache_slc_rows, cache_win_rows)


def _gate_combine_kernel(gt_ref, oc_ref, os_ref, ow_ref, o_ref):
    gt = gt_ref[...]
    for g in range(N_KV):
        for h in range(HPG):
            c0 = g * LANES + h * N_BRANCH
            lo = (g * HPG + h) * HEAD_DIM
            hi = lo + HEAD_DIM
            og = (gt[:, c0:c0 + 1] * oc_ref[:, lo:hi] + gt[:, c0 + 1:c0 + 2] * os_ref[:, lo:hi]
                  + gt[:, c0 + 2:c0 + 3] * ow_ref[:, lo:hi])
            o_ref[:, lo:hi] = og.astype(BF16)


def _gate_combine(gates, o_cmp, o_slc, o_win):
    return pl.pallas_call(
        _gate_combine_kernel,
        out_shape=jax.ShapeDtypeStruct(o_cmp.shape, BF16),
        name="gate_combine",
    )(gates, o_cmp, o_slc, o_win)


def _prep_weights(ln_g, ln_b, ffn_w_in, ffn_w_out, gmlp_w_in, gmlp_b_in, gmlp_ln_g, gmlp_ln_b, gmlp_w_s,
                  gmlp_b_s, gmlp_w_out, nsa_w_qg, nsa_w_o, w_kv, cmp_pos, w_cmp):
    nq = N_HEADS * HEAD_DIM
    wg = nsa_w_qg[:, :, nq:].reshape(N_B_LAYERS, D_MODEL, N_KV, GATE_COLS)
    wg = jnp.pad(wg, ((0, 0), (0, 0), (0, 0), (0, LANES - GATE_COLS))).reshape(N_B_LAYERS, D_MODEL, N_KV * LANES)
    pos_t = jnp.broadcast_to(cmp_pos[:, :, None, :], (CMP_BLOCK, 2, N_KV, HEAD_DIM)).reshape(CMP_BLOCK, KV_COLS)
    return dict(
        ln_g=ln_g.reshape(DEPTH, 3, 1, D_MODEL), ln_b=ln_b.reshape(DEPTH, 3, 1, D_MODEL),
        ffn_w_in=ffn_w_in.astype(BF16), ffn_w_out=ffn_w_out.astype(BF16),
        gmlp_w_in=gmlp_w_in.astype(BF16), gmlp_b_in=gmlp_b_in.reshape(N_A_LAYERS, 1, 2 * D_GATE),
        gmlp_ln_g=gmlp_ln_g.reshape(N_A_LAYERS, 1, D_GATE), gmlp_ln_b=gmlp_ln_b.reshape(N_A_LAYERS, 1, D_GATE),
        gmlp_w_s=gmlp_w_s, gmlp_b_s=gmlp_b_s, gmlp_w_out=gmlp_w_out.astype(BF16),
        w_q=nsa_w_qg[:, :, :nq].astype(BF16), w_g=wg.astype(BF16), w_o=nsa_w_o.astype(BF16),
        w_kv=w_kv.astype(BF16),
        pos_t=jnp.concatenate([pos_t] * (PAGE_SIZE // CMP_BLOCK), axis=0),
        pos8=pos_t.reshape(CMP_BLOCK, 2 * N_KV, HEAD_DIM),
        w_cmp=w_cmp.astype(BF16),
        w_pair=jnp.concatenate([w_cmp[:, 0], w_cmp[:, 1]], axis=-1).astype(BF16).reshape(
            CMP_BLOCK // 2, 2 * HEAD_DIM, 2 * HEAD_DIM),
    )


def _trunk(x, w, tables, pos_rows, spatial, attend):
    xb = x.astype(BF16)
    v_rows, ctx = [], None
    for l in range(DEPTH):
        if l == N_A_LAYERS:
            ctx = _kv_proj(xb, w["w_kv"], tables, pos_rows)
        x, xb = _ffn(x, xb, w["ffn_w_in"], w["ffn_w_out"], l, 0, w["ln_g"][l, 0], w["ln_b"][l, 0])
        if l < N_A_LAYERS:
            rows, causal, w_sp, b_sp = spatial(l)
            z = _gmlp_in(xb, w["gmlp_w_in"][l], w["gmlp_b_in"][l], w["gmlp_ln_g"][l], w["gmlp_ln_b"][l])
            v_rows.append(z[:, D_GATE:])
            x, xb = _gmlp_out(z, x, w_sp, b_sp, w["gmlp_w_out"][l], w["ln_g"][l, 1], w["ln_b"][l, 1], rows, causal)
        else:
            o = attend(l - N_A_LAYERS, xb, ctx)
            x, xb = _proj_norm(o, x, w["w_o"][l - N_A_LAYERS], w["ln_g"][l, 1], w["ln_b"][l, 1])
        x, xb = _ffn(x, xb, w["ffn_w_in"], w["ffn_w_out"], l, 1, w["ln_g"][l, 2], w["ln_b"][l, 2])
    return x, ctx, v_rows


def kernel(x_prompt, x_sample, cache_cmp_kv, cache_slc_kv, cache_win_kv, page_table, ln_g, ln_b, ffn_w_in,
           ffn_w_out, gmlp_w_in, gmlp_b_in, gmlp_ln_g, gmlp_ln_b, gmlp_w_s, gmlp_b_s, gmlp_w_out, nsa_w_qg,
           nsa_w_o, w_kv, cmp_pos, w_cmp):
    w = _prep_weights(ln_g, ln_b, ffn_w_in, ffn_w_out, gmlp_w_in, gmlp_b_in, gmlp_ln_g, gmlp_ln_b, gmlp_w_s,
                      gmlp_b_s, gmlp_w_out, nsa_w_qg, nsa_w_o, w_kv, cmp_pos, w_cmp)
    kv_shape = (2, N_KV, HEAD_DIM)

    mp = BATCH * SEQ
    tables_p = _rotary_tables(np.arange(SEQ))
    n_blk_p = SEQ // CMP_BLOCK
    expand = (np.arange(SEQ)[None, :] // SEL_BLOCK == np.arange(n_blk_p)[:, None])
    expand = jnp.asarray(expand, BF16)
    cmp_p = {}

    def spatial_p(l):
        return CHUNK, CHUNK, w["gmlp_w_s"][l], w["gmlp_b_s"][l].T

    def attend_p(bl, xb, ctx):
        kv, kvb = ctx
        if "c" not in cmp_p:
            cmp_p["c"] = _compress_rows(kv, w["pos_t"], w["w_cmp"])
        q, gates = _qg_proj(xb, w["w_q"][bl], w["w_g"][bl], tables_p, SEQ)
        return _nsa_prompt(q, gates, cmp_p["c"], kvb, expand, BATCH, SEQ)

    y_p, (kv_p, _), _ = _trunk(x_prompt.reshape(mp, D_MODEL), w, tables_p, SEQ, spatial_p, attend_p)
    kv_p = kv_p.reshape(BATCH, SEQ, N_BRANCH, *kv_shape)
    p_cmp, p_slc, p_win = kv_p[:, :, 0], kv_p[:, :, 1], kv_p[:, -min(WINDOW, SEQ):, 2]

    ms = DEC_BATCH * DEC_SEQ
    pos_s = PAST_LEN + np.arange(DEC_SEQ)
    tables_s = _rotary_tables(np.tile(pos_s, DEC_BATCH))
    n_pages = PAST_LEN // PAGE_SIZE
    n_cache_blocks = PAST_LEN // SEL_BLOCK
    n_sel_blocks = -(-(PAST_LEN + DEC_SEQ) // SEL_BLOCK)
    n_cmp_s = (PAST_LEN + DEC_SEQ) // CMP_BLOCK
    pt_flat = page_table.reshape(-1)
    cache_cmp_rows = cache_cmp_kv.reshape(-1, 2 * N_KV, HEAD_DIM)
    cache_slc_rows = cache_slc_kv.reshape(-1, 2 * N_KV, HEAD_DIM)
    cache_win_rows = cache_win_kv.reshape(-1, 2 * N_KV, HEAD_DIM)
    n_buf = cache_win_kv.shape[1]
    cmp_s = {}

    def spatial_s(l):
        c = min(DEC_SEQ, CHUNK)
        w_t = jnp.tile(w["gmlp_w_s"][l][:, :c, :c], (1, DEC_BATCH, DEC_BATCH))
        b_t = jnp.tile(w["gmlp_b_s"][l][:, :c].T, (DEC_BATCH, 1))
        return ms, c, w_t, b_t

    def attend_s(bl, xb, ctx):
        kv, _ = ctx
        if "c" not in cmp_s:
            cmp_s["c"] = _compress_pages(cache_cmp_rows, pt_flat, w["pos8"], w["w_pair"]).reshape(-1, KV_COLS)
            new = kv.reshape(DEC_BATCH, DEC_SEQ, N_BRANCH * KV_COLS)
            cmp_s["new"] = jnp.pad(new, ((0, 0), (0, 8 - DEC_SEQ), (0, 0)))
        q, gates = _qg_proj(xb, w["w_q"][bl], w["w_g"][bl], tables_s, ms)
        q5 = q.reshape(DEC_BATCH, DEC_SEQ, N_KV, HPG, HEAD_DIM)
        qs = q5.transpose(0, 2, 3, 1, 4).reshape(DEC_BATCH, N_KV, HPG * DEC_SEQ, HEAD_DIM)
        o_cmp, imp = _smp_cmp(qs, cmp_s["c"], n_cmp_s)
        idx = _smp_topk(imp.reshape(DEC_BATCH * N_KV * DEC_SEQ, n_cmp_s), n_sel_blocks)[:, :TOP_N]
        q8 = jnp.pad(q5.transpose(0, 2, 1, 3, 4), ((0, 0), (0, 0), (0, 0), (0, 8 - HPG), (0, 0)))
        o_slc, o_win = _smp_attn(idx.reshape(-1), pt_flat, q8, cache_slc_rows, cmp_s["new"], cache_win_rows,
                                 n_pages, n_cache_blocks, n_buf)
        o_cmp = o_cmp.reshape(DEC_BATCH, N_KV, HPG, DEC_SEQ, HEAD_DIM).transpose(0, 3, 1, 2, 4).reshape(ms, D_MODEL)
        o_slc = o_slc[:, :, :, :HPG].transpose(0, 2, 1, 3, 4).reshape(ms, D_MODEL)
        o_win = o_win[:, :, :, :HPG].transpose(0, 2, 1, 3, 4).reshape(ms, D_MODEL)
        return _gate_combine(gates, o_cmp, o_slc, o_win)

    y_s, (kv_s, _), v_s = _trunk(x_sample.reshape(ms, D_MODEL), w, tables_s, ms, spatial_s, attend_s)
    kv_s = kv_s.reshape(DEC_BATCH, DEC_SEQ, N_BRANCH, *kv_shape)
    s_cmp, s_slc = kv_s[:, :, 0], kv_s[:, :, 1]
    s_win = jnp.concatenate([cache_win_kv, kv_s[:, :, 2]], axis=1)[:, -min(WINDOW, PAST_LEN + DEC_SEQ):]
    s_v = jnp.stack([v.reshape(DEC_BATCH, DEC_SEQ, D_GATE) for v in v_s])

    return (y_p.reshape(BATCH, SEQ, D_MODEL), y_s.reshape(DEC_BATCH, DEC_SEQ, D_MODEL), p_cmp, p_slc, p_win,
            s_cmp, s_slc, s_win, s_v)
```

```python
import functools
import math

import numpy as np
import jax
import jax.numpy as jnp
from jax import lax
from jax.experimental import pallas as pl
from jax.experimental.pallas import tpu as pltpu

D_MODEL = 2048
BATCH = 4
SEQ = 2048
DEPTH = 4
DEC_BATCH = 8
DEC_SEQ = 4
PAST_LEN = 16384
PAGE_SIZE = 128

N_A_LAYERS = DEPTH // 2
N_B_LAYERS = DEPTH - N_A_LAYERS
D_FF = 5632
D_GATE = D_MODEL
CHUNK = 128
N_GROUPS_A = 8
GROUP_DIM_A = D_GATE // N_GROUPS_A
HEAD_DIM = 128
N_HEADS = D_MODEL // HEAD_DIM
N_KV = 4
HPG = N_HEADS // N_KV
ROT_DIM = HEAD_DIM // 4
ROPE_THETA = 500000.0
CMP_BLOCK = 64
SEL_BLOCK = 64
TOP_N = 16
WINDOW = 512
N_BRANCH = 3
ALPHA = (2 * DEPTH) ** 0.25
LN_EPS = 1e-5
NEG = -1e30
FORCED = 1e4

F32 = jnp.float32
BF16 = jnp.bfloat16

V7X_VMEM_LIMIT_BYTES = 56 * 1024 * 1024
LOG2E = math.log2(math.e)
LANES = 128
KV_COLS = 2 * N_KV * HEAD_DIM
GATE_COLS = HPG * N_BRANCH
KEY_CHUNK = 512
PAGES_PER_STEP = 16
Q_TILE = 256


def _params(*sem):
    return pltpu.CompilerParams(dimension_semantics=sem, vmem_limit_bytes=V7X_VMEM_LIMIT_BYTES)


def _row_tile(m, cap=512):
    return m if m <= cap else cap


def _dot(a, b):
    return jnp.dot(a, b, preferred_element_type=F32)


def _dot_nt(a, b):
    return lax.dot_general(a, b, (((1,), (1,)), ((), ())), preferred_element_type=F32)


def _dot_tn(a, b):
    return lax.dot_general(a, b, (((0,), (0,)), ((), ())), preferred_element_type=F32)


def _layer_norm(x, g, b):
    mu = jnp.mean(x, axis=-1, keepdims=True)
    d = x - mu
    var = jnp.mean(d * d, axis=-1, keepdims=True)
    return d * lax.rsqrt(var + LN_EPS) * g + b


def _sigmoid(x):
    return 1.0 / (1.0 + jnp.exp(-x))


def _softmax_rows(s):
    m = jnp.max(s, axis=-1, keepdims=True)
    p = jnp.exp(s - m)
    return p / jnp.sum(p, axis=-1, keepdims=True)


def _rotate(x, cos_t, sin_lo, sin_hi):
    half = ROT_DIM // 2
    return (x * cos_t + pltpu.roll(x, LANES - half, 1) * sin_lo + pltpu.roll(x, half, 1) * sin_hi)


def _rotary_tables(pos):
    half = ROT_DIM // 2
    inv = ROPE_THETA ** (-jnp.arange(half, dtype=F32) / half)
    ang = jnp.asarray(pos).astype(F32)[:, None] * inv[None, :]
    cos, sin = jnp.cos(ang), jnp.sin(ang)
    rows = ang.shape[0]
    pad = jnp.zeros((rows, LANES - ROT_DIM), F32)
    zero = jnp.zeros((rows, half), F32)
    cos_t = jnp.concatenate([cos, cos, pad + 1.0], axis=1)
    sin_lo = jnp.concatenate([-sin, zero, pad], axis=1)
    sin_hi = jnp.concatenate([zero, sin, pad], axis=1)
    return cos_t, sin_lo, sin_hi


def _ffn_kernel(xb_ref, x_ref, wg_ref, wu_ref, wo_ref, g_ref, b_ref, y_ref, yb_ref):
    f = pl.program_id(1)

    @pl.when(f == 0)
    def _():
        y_ref[...] = jnp.zeros_like(y_ref)

    xb = xb_ref[...]
    gate = _dot(xb, wg_ref[...])
    up = _dot(xb, wu_ref[...])
    h = gate * _sigmoid(gate) * up
    y_ref[...] += _dot(h.astype(BF16), wo_ref[...])

    @pl.when(f == pl.num_programs(1) - 1)
    def _():
        y = _layer_norm(ALPHA * x_ref[...] + 0.5 * y_ref[...], g_ref[...], b_ref[...])
        y_ref[...] = y
        yb_ref[...] = y.astype(BF16)


def _ffn(x, xb, w_in, w_out, l, s, g, b):
    m = x.shape[0]
    tm = _row_tile(m)
    tf = 512
    nf = D_FF // tf
    row = lambda i, f: (i, 0)
    return pl.pallas_call(
        _ffn_kernel,
        grid=(m // tm, nf),
        in_specs=[
            pl.BlockSpec((tm, D_MODEL), row),
            pl.BlockSpec((tm, D_MODEL), row),
            pl.BlockSpec((None, None, D_MODEL, tf), lambda i, f: (l, s, 0, f)),
            pl.BlockSpec((None, None, D_MODEL, tf), lambda i, f: (l, s, 0, nf + f)),
            pl.BlockSpec((None, None, tf, D_MODEL), lambda i, f: (l, s, f, 0)),
            pl.BlockSpec((1, D_MODEL), lambda i, f: (0, 0)),
            pl.BlockSpec((1, D_MODEL), lambda i, f: (0, 0)),
        ],
        out_specs=[pl.BlockSpec((tm, D_MODEL), row), pl.BlockSpec((tm, D_MODEL), row)],
        out_shape=[jax.ShapeDtypeStruct((m, D_MODEL), F32), jax.ShapeDtypeStruct((m, D_MODEL), BF16)],
        compiler_params=_params("parallel", "arbitrary"),
        name="ffn",
    )(xb, x, w_in, w_in, w_out, g, b)


def _gmlp_in_kernel(xb_ref, w_ref, bias_ref, lg_ref, lb_ref, z_ref):
    n = pl.program_id(0)
    z = _dot(xb_ref[...], w_ref[...]) + bias_ref[...]
    z = 0.5 * z * (1.0 + lax.erf(z * (1.0 / math.sqrt(2.0))))

    @pl.when(n == 0)
    def _():
        z_ref[...] = z

    @pl.when(n == 1)
    def _():
        z_ref[...] = _layer_norm(z, lg_ref[...], lb_ref[...])


def _gmlp_in(xb, w, bias, lg, lb):
    m = xb.shape[0]
    tm = _row_tile(m)
    return pl.pallas_call(
        _gmlp_in_kernel,
        grid=(2, m // tm),
        in_specs=[
            pl.BlockSpec((tm, D_MODEL), lambda n, i: (i, 0)),
            pl.BlockSpec((D_MODEL, D_GATE), lambda n, i: (0, n)),
            pl.BlockSpec((1, D_GATE), lambda n, i: (0, n)),
            pl.BlockSpec((1, D_GATE), lambda n, i: (0, 0)),
            pl.BlockSpec((1, D_GATE), lambda n, i: (0, 0)),
        ],
        out_specs=pl.BlockSpec((tm, D_GATE), lambda n, i: (i, n)),
        out_shape=jax.ShapeDtypeStruct((m, 2 * D_GATE), F32),
        compiler_params=_params("arbitrary", "arbitrary"),
        name="gmlp_in",
    )(xb, w, bias, lg, lb)


def _gmlp_out_kernel(u_ref, v_ref, x_ref, ws_ref, bs_ref, wo_ref, g_ref, b_ref, y_ref, yb_ref, *, rows, causal):
    tm = u_ref.shape[0]
    r_i = lax.broadcasted_iota(jnp.int32, (rows, rows), 0)
    c_i = lax.broadcasted_iota(jnp.int32, (rows, rows), 1)
    keep = ((r_i // causal) == (c_i // causal)) & (c_i <= r_i)
    w_sp = [jnp.where(keep, ws_ref[g], 0.0).astype(BF16) for g in range(N_GROUPS_A)]
    bs = bs_ref[...]
    chunks = []
    for ch in range(tm // rows):
        u = u_ref[ch * rows:(ch + 1) * rows, :]
        v = v_ref[ch * rows:(ch + 1) * rows, :]
        parts = []
        for g in range(N_GROUPS_A):
            lo, hi = g * GROUP_DIM_A, (g + 1) * GROUP_DIM_A
            mixed = _dot(w_sp[g], v[:, lo:hi].astype(BF16)) + bs[:, g:g + 1]
            parts.append((u[:, lo:hi] * mixed).astype(BF16))
        chunks.append(jnp.concatenate(parts, axis=1))
    y = chunks[0] if len(chunks) == 1 else jnp.concatenate(chunks, axis=0)
    f = _dot(y, wo_ref[...])
    out = _layer_norm(ALPHA * x_ref[...] + f, g_ref[...], b_ref[...])
    y_ref[...] = out
    yb_ref[...] = out.astype(BF16)


def _gmlp_out(z, x, w_sp, b_sp, w_out, g, b, rows, causal):
    m = x.shape[0]
    tm = _row_tile(m)
    row = lambda i: (i, 0)
    kern = functools.partial(_gmlp_out_kernel, rows=rows, causal=causal)
    return pl.pallas_call(
        kern,
        grid=(m // tm,),
        in_specs=[
            pl.BlockSpec((tm, D_GATE), lambda i: (i, 0)),
            pl.BlockSpec((tm, D_GATE), lambda i: (i, 1)),
            pl.BlockSpec((tm, D_MODEL), row),
            pl.BlockSpec((N_GROUPS_A, rows, rows), lambda i: (0, 0, 0)),
            pl.BlockSpec((rows, N_GROUPS_A), lambda i: (0, 0)),
            pl.BlockSpec((D_GATE, D_MODEL), lambda i: (0, 0)),
            pl.BlockSpec((1, D_MODEL), lambda i: (0, 0)),
            pl.BlockSpec((1, D_MODEL), lambda i: (0, 0)),
        ],
        out_specs=[pl.BlockSpec((tm, D_MODEL), row), pl.BlockSpec((tm, D_MODEL), row)],
        out_shape=[jax.ShapeDtypeStruct((m, D_MODEL), F32), jax.ShapeDtypeStruct((m, D_MODEL), BF16)],
        compiler_params=_params("parallel"),
        name="gmlp_out",
    )(z, z, x, w_sp, b_sp, w_out, g, b)


def _proj_norm_kernel(a_ref, x_ref, w_ref, g_ref, b_ref, y_ref, yb_ref):
    out = _layer_norm(ALPHA * x_ref[...] + _dot(a_ref[...], w_ref[...]), g_ref[...], b_ref[...])
    y_ref[...] = out
    yb_ref[...] = out.astype(BF16)


def _proj_norm(a, x, w, g, b):
    m = x.shape[0]
    tm = _row_tile(m)
    row = lambda i: (i, 0)
    return pl.pallas_call(
        _proj_norm_kernel,
        grid=(m // tm,),
        in_specs=[
            pl.BlockSpec((tm, D_MODEL), row),
            pl.BlockSpec((tm, D_MODEL), row),
            pl.BlockSpec((D_MODEL, D_MODEL), lambda i: (0, 0)),
            pl.BlockSpec((1, D_MODEL), lambda i: (0, 0)),
            pl.BlockSpec((1, D_MODEL), lambda i: (0, 0)),
        ],
        out_specs=[pl.BlockSpec((tm, D_MODEL), row), pl.BlockSpec((tm, D_MODEL), row)],
        out_shape=[jax.ShapeDtypeStruct((m, D_MODEL), F32), jax.ShapeDtypeStruct((m, D_MODEL), BF16)],
        compiler_params=_params("parallel"),
        name="proj_norm",
    )(a, x, w, g, b)


def _kv_proj_kernel(xb_ref, w_ref, cos_ref, slo_ref, shi_ref, kvb_ref, cmp_hbm, slc_hbm, win_hbm, stage, sem):
    i, n = pl.program_id(0), pl.program_id(1)
    tm = xb_ref.shape[0]
    step = i * pl.num_programs(1) + n
    last = pl.num_programs(0) * pl.num_programs(1) - 1
    slot = step % 2
    outs = (cmp_hbm, slc_hbm, win_hbm)

    def copies(out_hbm, c, slot_):
        return [pltpu.make_async_copy(stage.at[slot_, :, pl.ds(g * HEAD_DIM, HEAD_DIM)],
                                      out_hbm.at[pl.ds(i * tm, tm), c * N_KV + g, :], sem.at[slot_])
                for g in range(N_KV)]

    @pl.when(step >= 2)
    def _():
        for cp in copies(cmp_hbm, 0, slot):
            cp.wait()

    acc = _dot(xb_ref[...], w_ref[...])
    for col in range(2 * N_BRANCH):
        @pl.when(n == col)
        def _(col=col):
            br, c = divmod(col, 2)
            if c == 0:
                cos_t, s_lo, s_hi = cos_ref[...], slo_ref[...], shi_ref[...]
                val = jnp.concatenate([_rotate(acc[:, h * HEAD_DIM:(h + 1) * HEAD_DIM], cos_t, s_lo, s_hi)
                                       for h in range(N_KV)], axis=1)
            else:
                val = acc
            stage[slot] = val
            kvb_ref[...] = val.astype(BF16)
            for cp in copies(outs[br], c, slot):
                cp.start()

    @pl.when(step == last)
    def _():
        for cp in copies(cmp_hbm, 0, 1 - slot) + copies(cmp_hbm, 0, slot):
            cp.wait()


def _kv_proj(xb, w, tables, pos_rows):
    m = xb.shape[0]
    tm = _row_tile(m, 1024)
    half = N_KV * HEAD_DIM
    ncol = N_BRANCH * 2
    nt = pos_rows // tm
    tab = lambda i, n: (i % nt, 0)
    rows = jax.ShapeDtypeStruct((m, 2 * N_KV, HEAD_DIM), F32)
    hbm = pl.BlockSpec(memory_space=pl.ANY)
    return pl.pallas_call(
        _kv_proj_kernel,
        grid=(m // tm, ncol),
        in_specs=[
            pl.BlockSpec((tm, D_MODEL), lambda i, n: (i, 0)),
            pl.BlockSpec((D_MODEL, half), lambda i, n: (0, n)),
            pl.BlockSpec((tm, LANES), tab),
            pl.BlockSpec((tm, LANES), tab),
            pl.BlockSpec((tm, LANES), tab),
        ],
        out_specs=[pl.BlockSpec((tm, half), lambda i, n: (i, n)), hbm, hbm, hbm],
        out_shape=[jax.ShapeDtypeStruct((m, ncol * half), BF16), rows, rows, rows],
        scratch_shapes=[pltpu.VMEM((2, tm, half), F32), pltpu.SemaphoreType.DMA((2,))],
        compiler_params=_params("arbitrary", "arbitrary"),
        name="kv_proj",
    )(xb, w, *tables)


def _qg_proj_kernel(xb_ref, wq_ref, wg_ref, cos_ref, slo_ref, shi_ref, q_ref, gt_ref):
    xb = xb_ref[...]
    acc = _dot(xb, wq_ref[...])
    cos_t, s_lo, s_hi = cos_ref[...], slo_ref[...], shi_ref[...]
    for h in range(N_HEADS):
        lo, hi = h * HEAD_DIM, (h + 1) * HEAD_DIM
        q_ref[:, lo:hi] = _rotate(acc[:, lo:hi], cos_t, s_lo, s_hi).astype(BF16)
    gt_ref[...] = _sigmoid(_dot(xb, wg_ref[...]))


def _qg_proj(xb, wq, wg, tables, pos_rows):
    m = xb.shape[0]
    tm = _row_tile(m)
    nt = pos_rows // tm
    tab = lambda i: (i % nt, 0)
    gcols = N_KV * LANES
    return pl.pallas_call(
        _qg_proj_kernel,
        grid=(m // tm,),
        in_specs=[
            pl.BlockSpec((tm, D_MODEL), lambda i: (i, 0)),
            pl.BlockSpec((D_MODEL, D_MODEL), lambda i: (0, 0)),
            pl.BlockSpec((D_MODEL, gcols), lambda i: (0, 0)),
            pl.BlockSpec((tm, LANES), tab),
            pl.BlockSpec((tm, LANES), tab),
            pl.BlockSpec((tm, LANES), tab),
        ],
        out_specs=[pl.BlockSpec((tm, D_MODEL), lambda i: (i, 0)), pl.BlockSpec((tm, gcols), lambda i: (i, 0))],
        out_shape=[jax.ShapeDtypeStruct((m, D_MODEL), BF16), jax.ShapeDtypeStruct((m, gcols), F32)],
        compiler_params=_params("parallel"),
        name="qg_proj",
    )(xb, wq, wg, *tables)


def _compress_pages_kernel(tbl_ref, *refs):
    del tbl_ref
    npg = PAGES_PER_STEP
    pages = refs[:npg]
    pos_ref, w_ref, out_ref = refs[npg:]
    per_page = PAGE_SIZE // CMP_BLOCK
    rows = npg * per_page * 2 * N_KV
    acc = jnp.zeros((rows, 2 * HEAD_DIM), F32)
    for j in range(CMP_BLOCK // 2):
        halves = []
        for l in (2 * j, 2 * j + 1):
            pos_l = pos_ref[l]
            tiles = [pages[p][l + CMP_BLOCK * h] + pos_l for p in range(npg) for h in range(per_page)]
            halves.append(jnp.concatenate(tiles, axis=0).astype(BF16))
        acc = acc + _dot(jnp.concatenate(halves, axis=1), w_ref[j])
    is_k = (lax.broadcasted_iota(jnp.int32, (rows, HEAD_DIM), 0) % (2 * N_KV)) < N_KV
    out_ref[...] = jnp.where(is_k, acc[:, :HEAD_DIM], acc[:, HEAD_DIM:])


def _compress_pages(cache, table, pos8, w_pair):
    npg = PAGES_PER_STEP
    rows = npg * (PAGE_SIZE // CMP_BLOCK) * 2 * N_KV

    def page_spec(p):
        return pl.BlockSpec((PAGE_SIZE, 2 * N_KV, HEAD_DIM), lambda i, tbl: (tbl[i * npg + p], 0, 0))

    grid_spec = pltpu.PrefetchScalarGridSpec(
        num_scalar_prefetch=1,
        grid=(table.shape[0] // npg,),
        in_specs=[page_spec(p) for p in range(npg)] + [
            pl.BlockSpec((CMP_BLOCK, 2 * N_KV, HEAD_DIM), lambda i, tbl: (0, 0, 0)),
            pl.BlockSpec((CMP_BLOCK // 2, 2 * HEAD_DIM, 2 * HEAD_DIM), lambda i, tbl: (0, 0, 0)),
        ],
        out_specs=pl.BlockSpec((rows, HEAD_DIM), lambda i, tbl: (i, 0)),
    )
    return pl.pallas_call(
        _compress_pages_kernel,
        grid_spec=grid_spec,
        out_shape=jax.ShapeDtypeStruct((table.shape[0] // npg * rows, HEAD_DIM), F32),
        compiler_params=_params("arbitrary"),
        name="compress_pages",
    )(table, *([cache] * npg), pos8, w_pair)


def _rank_select(score_t, n_blocks):
    idx = lax.broadcasted_iota(jnp.int32, score_t.shape, 0)
    rank = jnp.zeros(score_t.shape, F32)
    for i in range(n_blocks):
        row = score_t[i:i + 1, :]
        rank = rank + jnp.where(idx > i, jnp.where(row >= score_t, 1.0, 0.0), jnp.where(row > score_t, 1.0, 0.0))
    return jnp.where(rank < float(min(TOP_N, n_blocks)), 1.0, 0.0)


def _nsa_prompt_kernel(q_ref, gt_ref, ck_ref, cv_ref, sk_ref, sv_ref, wk_ref, wv_ref, e_ref, o_ref):
    tq = q_ref.shape[0]
    t_len = sk_ref.shape[0]
    nblk = ck_ref.shape[0]
    i = pl.program_id(2)
    scale = HEAD_DIM ** -0.5
    q = q_ref[...]
    q4 = jnp.concatenate([q[:, h * HEAD_DIM:(h + 1) * HEAD_DIM] for h in range(HPG)], axis=0)
    pos_c = i * tq + lax.broadcasted_iota(jnp.int32, (tq, 1), 0)
    pos_r = i * tq + lax.broadcasted_iota(jnp.int32, (1, tq), 1)
    pos4_r = jnp.concatenate([pos_r] * HPG, axis=1)

    c2 = scale * LOG2E

    ck = ck_ref[...].astype(BF16)
    cv = cv_ref[...].astype(BF16)
    blk_c = lax.broadcasted_iota(jnp.int32, (nblk, 1), 0)
    vis_t = (blk_c + 1) * CMP_BLOCK - 1 <= pos4_r
    s_t = jnp.where(vis_t, _dot_nt(ck, q4), NEG)
    p_t = jnp.exp2((s_t - jnp.max(s_t, axis=0, keepdims=True)) * c2)
    p_t = p_t * (1.0 / jnp.sum(p_t, axis=0, keepdims=True))
    p_t = p_t * jnp.where(pos4_r >= CMP_BLOCK - 1, 1.0, 0.0)
    o_cmp = _dot_tn(p_t.astype(BF16), cv)
    imp_t = p_t[:, 0:tq]
    for h in range(1, HPG):
        imp_t = imp_t + p_t[:, h * tq:(h + 1) * tq]
    cur = pos_r // SEL_BLOCK
    forced = (blk_c == 0) | (blk_c == cur) | (blk_c == cur - 1)
    score_t = jnp.where(forced, FORCED, jnp.where(blk_c <= cur, imp_t, -1.0))
    sel_t = _rank_select(score_t, nblk)
    sel_b = sel_t.astype(BF16)
    key_r = lax.broadcasted_iota(jnp.int32, (1, KEY_CHUNK), 1)

    def body(kc, carry):
        m, l, acc = carry
        start = pl.multiple_of(kc * KEY_CHUNK, KEY_CHUNK)
        k = sk_ref[pl.ds(start, KEY_CHUNK), :]
        v = sv_ref[pl.ds(start, KEY_CHUNK), :]
        member = _dot_tn(sel_b, e_ref[kc])
        bias = jnp.where(start + key_r <= pos_c, (member - 1.0) * (-NEG), NEG)
        s = _dot_nt(q4, k) + jnp.concatenate([bias] * HPG, axis=0)
        m_new = jnp.maximum(m, jnp.max(s, axis=-1, keepdims=True))
        a = jnp.exp2((m - m_new) * c2)
        pr = jnp.exp2((s - m_new) * c2)
        l = a * l + jnp.sum(pr, axis=-1, keepdims=True)
        acc = a * acc + _dot(pr.astype(BF16), v)
        return m_new, l, acc

    n_chunks = (i * tq + tq + KEY_CHUNK - 1) // KEY_CHUNK
    init = (jnp.full((HPG * tq, 1), NEG, F32), jnp.zeros((HPG * tq, 1), F32), jnp.zeros((HPG * tq, HEAD_DIM), F32))
    _, l_s, acc_s = lax.fori_loop(0, n_chunks, body, init)
    o_slc = acc_s * (1.0 / l_s)

    n_win = WINDOW + tq
    start = pl.multiple_of(jnp.maximum(i * tq - WINDOW, 0), tq)
    kw = wk_ref[pl.ds(start, n_win), :]
    vw = wv_ref[pl.ds(start, n_win), :]
    d = pos_c - (start + lax.broadcasted_iota(jnp.int32, (1, n_win), 1))
    bias_w = jnp.where((d >= 0) & (d < WINDOW), 0.0, NEG)
    sw = _dot_nt(q4, kw) + jnp.concatenate([bias_w] * HPG, axis=0)
    pw = jnp.exp2((sw - jnp.max(sw, axis=-1, keepdims=True)) * c2)
    o_win = _dot(pw.astype(BF16), vw) * (1.0 / jnp.sum(pw, axis=-1, keepdims=True))

    gt = gt_ref[...]
    for h in range(HPG):
        r0, r1 = h * tq, (h + 1) * tq
        c0 = h * N_BRANCH
        og = (gt[:, c0:c0 + 1] * o_cmp[r0:r1] + gt[:, c0 + 1:c0 + 2] * o_slc[r0:r1]
              + gt[:, c0 + 2:c0 + 3] * o_win[r0:r1])
        o_ref[:, h * HEAD_DIM:(h + 1) * HEAD_DIM] = og.astype(BF16)


def _nsa_prompt(q, gates, cmp_c, kvb, expand, batch, t_len):
    tq = Q_TILE
    nq = t_len // tq
    nblk = t_len // CMP_BLOCK
    gw = HPG * HEAD_DIM
    qrow = lambda b, g, i: (b * nq + i, g)

    def kv_spec(col0):
        return pl.BlockSpec((t_len, HEAD_DIM), lambda b, g, i: (b, col0 + g))

    return pl.pallas_call(
        _nsa_prompt_kernel,
        grid=(batch, N_KV, nq),
        in_specs=[
            pl.BlockSpec((tq, gw), qrow),
            pl.BlockSpec((tq, LANES), qrow),
            pl.BlockSpec((nblk, HEAD_DIM), lambda b, g, i: (b, g)),
            pl.BlockSpec((nblk, HEAD_DIM), lambda b, g, i: (b, N_KV + g)),
            kv_spec(2 * N_KV), kv_spec(3 * N_KV), kv_spec(4 * N_KV), kv_spec(5 * N_KV),
            pl.BlockSpec((t_len // KEY_CHUNK, nblk, KEY_CHUNK), lambda b, g, i: (0, 0, 0)),
        ],
        out_specs=pl.BlockSpec((tq, gw), qrow),
        out_shape=jax.ShapeDtypeStruct((batch * t_len, D_MODEL), BF16),
        compiler_params=_params("parallel", "parallel", "arbitrary"),
        name="nsa_prompt",
    )(q, gates, cmp_c, cmp_c, kvb, kvb, kvb, kvb, expand)


def _smp_cmp_kernel(q_ref, ck_ref, cv_ref, o_ref, imp_ref):
    nblk = ck_ref.shape[0]
    scale = HEAD_DIM ** -0.5
    q = q_ref[0, 0]
    rows = q.shape[0]
    tok = lax.broadcasted_iota(jnp.int32, (rows, 1), 0) % DEC_SEQ
    pos = PAST_LEN + tok
    blk = lax.broadcasted_iota(jnp.int32, (1, nblk), 1)
    vis = (blk + 1) * CMP_BLOCK - 1 <= pos
    p = _softmax_rows(jnp.where(vis, _dot_nt(q, ck_ref[...].astype(BF16)) * scale, NEG))
    p = p * jnp.where(pos >= CMP_BLOCK - 1, 1.0, 0.0)
    o_ref[0, 0] = _dot(p.astype(BF16), cv_ref[...].astype(BF16))
    imp = p[0:DEC_SEQ]
    for h in range(1, HPG):
        imp = imp + p[h * DEC_SEQ:(h + 1) * DEC_SEQ]
    imp_ref[0, 0] = imp


def _smp_cmp(qs, cmp_c, nblk):
    rows = HPG * DEC_SEQ
    return pl.pallas_call(
        _smp_cmp_kernel,
        grid=(DEC_BATCH, N_KV),
        in_specs=[
            pl.BlockSpec((1, 1, rows, HEAD_DIM), lambda b, g: (b, g, 0, 0)),
            pl.BlockSpec((nblk, HEAD_DIM), lambda b, g: (b, g)),
            pl.BlockSpec((nblk, HEAD_DIM), lambda b, g: (b, N_KV + g)),
        ],
        out_specs=[pl.BlockSpec((1, 1, rows, HEAD_DIM), lambda b, g: (b, g, 0, 0)),
                   pl.BlockSpec((1, 1, DEC_SEQ, nblk), lambda b, g: (b, g, 0, 0))],
        out_shape=[jax.ShapeDtypeStruct((DEC_BATCH, N_KV, rows, HEAD_DIM), F32),
                   jax.ShapeDtypeStruct((DEC_BATCH, N_KV, DEC_SEQ, nblk), F32)],
        compiler_params=_params("parallel", "parallel"),
        name="smp_cmp",
    )(qs, cmp_c, cmp_c)


def _smp_topk_kernel(imp_ref, idx_ref, *, n_sel_blocks):
    imp = imp_ref[...]
    rows, ncmp = imp.shape
    width = ncmp + LANES
    imp = jnp.concatenate([imp, jnp.zeros((rows, LANES), F32)], axis=1)
    j = lax.broadcasted_iota(jnp.int32, (rows, width), 1)
    tok = lax.broadcasted_iota(jnp.int32, (rows, 1), 0) % DEC_SEQ
    cur = (PAST_LEN + tok) // SEL_BLOCK
    forced = (j == 0) | (j == cur) | (j == cur - 1)
    score = jnp.where(forced, FORCED, jnp.where(j <= cur, imp, -1.0))
    score = jnp.where(j < n_sel_blocks, score, -jnp.inf)
    lane = lax.broadcasted_iota(jnp.int32, (rows, LANES), 1)
    out = jnp.zeros((rows, LANES), jnp.int32)
    jf = j.astype(F32)
    for n in range(min(TOP_N, n_sel_blocks)):
        m = jnp.max(score, axis=-1, keepdims=True)
        pick = jnp.min(jnp.where(score == m, jf, float(width)), axis=-1, keepdims=True)
        out = jnp.where(lane == n, pick.astype(jnp.int32), out)
        score = jnp.where(jf == pick, -jnp.inf, score)
    idx_ref[...] = out


def _smp_topk(imp, n_sel_blocks):
    rows = imp.shape[0]
    return pl.pallas_call(
        functools.partial(_smp_topk_kernel, n_sel_blocks=n_sel_blocks),
        out_shape=jax.ShapeDtypeStruct((rows, LANES), jnp.int32),
        name="smp_topk",
    )(imp)


def _smp_attn_kernel(idx_ref, pt_ref, q_ref, nsk_ref, nsv_ref, nwk_ref, nwv_ref, slc_hbm, win_hbm,
                     oslc_ref, owin_ref, kbuf, vbuf, wbuf, sem, *, n_pages, n_cache_blocks):
    n_sel = TOP_N
    per_page = PAGE_SIZE // SEL_BLOCK
    n_sel_keys = n_sel * SEL_BLOCK
    n_buf = wbuf.shape[2] - LANES
    s = pl.program_id(0)
    slot = s % 2

    def copies(step, slot_, for_wait):
        b, g = step // N_KV, step % N_KV
        out = []
        for t in range(DEC_SEQ):
            for n in range(n_sel):
                if for_wait:
                    row0 = 0
                else:
                    bid = jnp.minimum(idx_ref[(step * DEC_SEQ + t) * n_sel + n], n_cache_blocks - 1)
                    row0 = (pt_ref[b * n_pages + bid // per_page] * per_page + bid % per_page) * SEL_BLOCK
                dst = pl.ds(n * SEL_BLOCK, SEL_BLOCK)
                out.append(pltpu.make_async_copy(slc_hbm.at[pl.ds(row0, SEL_BLOCK), g, :],
                                                 kbuf.at[slot_, t, dst, :], sem.at[slot_]))
                out.append(pltpu.make_async_copy(slc_hbm.at[pl.ds(row0, SEL_BLOCK), N_KV + g, :],
                                                 vbuf.at[slot_, t, dst, :], sem.at[slot_]))
        for c in range(2):
            out.append(pltpu.make_async_copy(win_hbm.at[pl.ds(b * n_buf, n_buf), c * N_KV + g, :],
                                             wbuf.at[slot_, c, pl.ds(0, n_buf), :], sem.at[slot_]))
        return out

    @pl.when(s == 0)
    def _():
        for cp in copies(s, slot, False):
            cp.start()

    @pl.when(s + 1 < pl.num_programs(0))
    def _():
        for cp in copies(s + 1, 1 - slot, False):
            cp.start()

    for cp in copies(s, slot, True):
        cp.wait()

    scale = HEAD_DIM ** -0.5
    n_new = nsk_ref.shape[1]
    zeros = jnp.zeros((LANES, HEAD_DIM), F32)
    blk_lane = lax.broadcasted_iota(jnp.int32, (1, n_sel_keys + LANES), 1)
    in_cache = blk_lane < n_sel_keys
    off_new = blk_lane - n_sel_keys
    lane = lax.broadcasted_iota(jnp.int32, (1, n_buf + LANES), 1)
    kpos = jnp.where(lane < n_buf, PAST_LEN - n_buf + lane, PAST_LEN + lane - n_buf)
    real = (lane < n_buf) | (lane - n_buf < DEC_SEQ)

    wbuf[slot, 0, pl.ds(n_buf, LANES), :] = zeros
    wbuf[slot, 1, pl.ds(n_buf, LANES), :] = zeros
    wbuf[slot, 0, pl.ds(n_buf, n_new), :] = nwk_ref[0]
    wbuf[slot, 1, pl.ds(n_buf, n_new), :] = nwv_ref[0]
    kw = wbuf[slot, 0].astype(BF16)
    vw = wbuf[slot, 1].astype(BF16)

    for t in range(DEC_SEQ):
        pos = PAST_LEN + t
        q = q_ref[0, 0, t]
        ids = jnp.zeros((1, n_sel_keys + LANES), jnp.int32)
        has_new = jnp.zeros((1, 1), jnp.int32)
        for n in range(n_sel):
            bid = idx_ref[(s * DEC_SEQ + t) * n_sel + n]
            ids = jnp.where(blk_lane // SEL_BLOCK == n, bid, ids)
            has_new = jnp.maximum(has_new, jnp.where(bid == n_cache_blocks, 1, 0))
        kbuf[slot, t, pl.ds(n_sel_keys, LANES), :] = zeros
        vbuf[slot, t, pl.ds(n_sel_keys, LANES), :] = zeros
        kbuf[slot, t, pl.ds(n_sel_keys, n_new), :] = nsk_ref[0]
        vbuf[slot, t, pl.ds(n_sel_keys, n_new), :] = nsv_ref[0]
        tok = jnp.where(in_cache, ids * SEL_BLOCK + blk_lane % SEL_BLOCK, n_cache_blocks * SEL_BLOCK + off_new)
        src_ok = jnp.where(in_cache, jnp.where(ids != n_cache_blocks, 1, 0),
                           jnp.where(off_new < DEC_SEQ, 1, 0) * has_new)
        ok = (src_ok > 0) & (tok <= pos)
        ps = _softmax_rows(jnp.where(ok, _dot_nt(q, kbuf[slot, t].astype(BF16)) * scale, NEG))
        oslc_ref[0, 0, t] = _dot(ps.astype(BF16), vbuf[slot, t].astype(BF16))

        d = pos - kpos
        okw = real & (d >= 0) & (d < WINDOW)
        pw = _softmax_rows(jnp.where(okw, _dot_nt(q, kw) * scale, NEG))
        owin_ref[0, 0, t] = _dot(pw.astype(BF16), vw)


def _smp_attn(idx_flat, pt_flat, q8, cache_slc_rows, new_pad, cache_win_rows, n_pages, n_cache_blocks, n_buf):
    n_sel = TOP_N
    n_new = new_pad.shape[1]
    bg = lambda s, idx, pt: (s // N_KV, s % N_KV, 0, 0, 0)

    def new_spec(col0):
        return pl.BlockSpec((1, n_new, HEAD_DIM), lambda s, idx, pt: (s // N_KV, 0, col0 + s % N_KV))

    qo_spec = pl.BlockSpec((1, 1, DEC_SEQ, 8, HEAD_DIM), bg)
    grid_spec = pltpu.PrefetchScalarGridSpec(
        num_scalar_prefetch=2,
        grid=(DEC_BATCH * N_KV,),
        in_specs=[qo_spec, new_spec(2 * N_KV), new_spec(3 * N_KV), new_spec(4 * N_KV), new_spec(5 * N_KV),
                  pl.BlockSpec(memory_space=pl.ANY), pl.BlockSpec(memory_space=pl.ANY)],
        out_specs=[qo_spec, qo_spec],
        scratch_shapes=[pltpu.VMEM((2, DEC_SEQ, n_sel * SEL_BLOCK + LANES, HEAD_DIM), F32),
                        pltpu.VMEM((2, DEC_SEQ, n_sel * SEL_BLOCK + LANES, HEAD_DIM), F32),
                        pltpu.VMEM((2, 2, n_buf + LANES, HEAD_DIM), F32),
                        pltpu.SemaphoreType.DMA((2,))],
    )
    shape = jax.ShapeDtypeStruct((DEC_BATCH, N_KV, DEC_SEQ, 8, HEAD_DIM), F32)
    return pl.pallas_call(
        functools.partial(_smp_attn_kernel, n_pages=n_pages, n_cache_blocks=n_cache_blocks),
        grid_spec=grid_spec,
        out_shape=[shape, shape],
        compiler_params=_params("arbitrary"),
        name="smp_attn",
    )(idx_flat, pt_flat, q8, new_pad, new_pad, new_pad, new_pad, cache_slc_rows, cache_win_rows)


def _gate_combine_kernel(gt_ref, oc_ref, os_ref, ow_ref, o_ref):
    gt = gt_ref[...]
    for g in range(N_KV):
        for h in range(HPG):
            c0 = g * LANES + h * N_BRANCH
            lo = (g * HPG + h) * HEAD_DIM
            hi = lo + HEAD_DIM
            og = (gt[:, c0:c0 + 1] * oc_ref[:, lo:hi] + gt[:, c0 + 1:c0 + 2] * os_ref[:, lo:hi]
                  + gt[:, c0 + 2:c0 + 3] * ow_ref[:, lo:hi])
            o_ref[:, lo:hi] = og.astype(BF16)


def _gate_combine(gates, o_cmp, o_slc, o_win):
    return pl.pallas_call(
        _gate_combine_kernel,
        out_shape=jax.ShapeDtypeStruct(o_cmp.shape, BF16),
        name="gate_combine",
    )(gates, o_cmp, o_slc, o_win)


def _prep_weights(ln_g, ln_b, ffn_w_in, ffn_w_out, gmlp_w_in, gmlp_b_in, gmlp_ln_g, gmlp_ln_b, gmlp_w_s,
                  gmlp_b_s, gmlp_w_out, nsa_w_qg, nsa_w_o, w_kv, cmp_pos, w_cmp):
    nq = N_HEADS * HEAD_DIM
    wg = nsa_w_qg[:, :, nq:].reshape(N_B_LAYERS, D_MODEL, N_KV, GATE_COLS)
    wg = jnp.pad(wg, ((0, 0), (0, 0), (0, 0), (0, LANES - GATE_COLS))).reshape(N_B_LAYERS, D_MODEL, N_KV * LANES)
    pos_t = jnp.broadcast_to(cmp_pos[:, :, None, :], (CMP_BLOCK, 2, N_KV, HEAD_DIM)).reshape(CMP_BLOCK, KV_COLS)
    return dict(
        ln_g=ln_g.reshape(DEPTH, 3, 1, D_MODEL), ln_b=ln_b.reshape(DEPTH, 3, 1, D_MODEL),
        ffn_w_in=ffn_w_in.astype(BF16), ffn_w_out=ffn_w_out.astype(BF16),
        gmlp_w_in=gmlp_w_in.astype(BF16), gmlp_b_in=gmlp_b_in.reshape(N_A_LAYERS, 1, 2 * D_GATE),
        gmlp_ln_g=gmlp_ln_g.reshape(N_A_LAYERS, 1, D_GATE), gmlp_ln_b=gmlp_ln_b.reshape(N_A_LAYERS, 1, D_GATE),
        gmlp_w_s=gmlp_w_s, gmlp_b_s=gmlp_b_s, gmlp_w_out=gmlp_w_out.astype(BF16),
        w_q=nsa_w_qg[:, :, :nq].astype(BF16), w_g=wg.astype(BF16), w_o=nsa_w_o.astype(BF16),
        w_kv=w_kv.astype(BF16),
        pos8=pos_t.reshape(CMP_BLOCK, 2 * N_KV, HEAD_DIM),
        w_pair=jnp.concatenate([w_cmp[:, 0], w_cmp[:, 1]], axis=-1).astype(BF16).reshape(
            CMP_BLOCK // 2, 2 * HEAD_DIM, 2 * HEAD_DIM),
    )


def _trunk(x, w, tables, pos_rows, spatial, attend):
    xb = x.astype(BF16)
    v_rows, ctx = [], None
    for l in range(DEPTH):
        if l == N_A_LAYERS:
            ctx = _kv_proj(xb, w["w_kv"], tables, pos_rows)
        x, xb = _ffn(x, xb, w["ffn_w_in"], w["ffn_w_out"], l, 0, w["ln_g"][l, 0], w["ln_b"][l, 0])
        if l < N_A_LAYERS:
            rows, causal, w_sp, b_sp = spatial(l)
            z = _gmlp_in(xb, w["gmlp_w_in"][l], w["gmlp_b_in"][l], w["gmlp_ln_g"][l], w["gmlp_ln_b"][l])
            v_rows.append(z[:, D_GATE:])
            x, xb = _gmlp_out(z, x, w_sp, b_sp, w["gmlp_w_out"][l], w["ln_g"][l, 1], w["ln_b"][l, 1], rows, causal)
        else:
            o = attend(l - N_A_LAYERS, xb, ctx)
            x, xb = _proj_norm(o, x, w["w_o"][l - N_A_LAYERS], w["ln_g"][l, 1], w["ln_b"][l, 1])
        x, xb = _ffn(x, xb, w["ffn_w_in"], w["ffn_w_out"], l, 1, w["ln_g"][l, 2], w["ln_b"][l, 2])
    return x, ctx, v_rows


def kernel(x_prompt, x_sample, cache_cmp_kv, cache_slc_kv, cache_win_kv, page_table, ln_g, ln_b, ffn_w_in,
           ffn_w_out, gmlp_w_in, gmlp_b_in, gmlp_ln_g, gmlp_ln_b, gmlp_w_s, gmlp_b_s, gmlp_w_out, nsa_w_qg,
           nsa_w_o, w_kv, cmp_pos, w_cmp):
    w = _prep_weights(ln_g, ln_b, ffn_w_in, ffn_w_out, gmlp_w_in, gmlp_b_in, gmlp_ln_g, gmlp_ln_b, gmlp_w_s,
                      gmlp_b_s, gmlp_w_out, nsa_w_qg, nsa_w_o, w_kv, cmp_pos, w_cmp)
    kv_shape = (2, N_KV, HEAD_DIM)

    mp = BATCH * SEQ
    tables_p = _rotary_tables(np.arange(SEQ))
    n_blk_p = SEQ // CMP_BLOCK
    key_blk = (np.arange(SEQ) // SEL_BLOCK).reshape(SEQ // KEY_CHUNK, 1, KEY_CHUNK)
    expand = jnp.asarray(key_blk == np.arange(n_blk_p)[None, :, None], BF16)
    cmp_p = {}

    def spatial_p(l):
        return CHUNK, CHUNK, w["gmlp_w_s"][l], w["gmlp_b_s"][l].T

    def attend_p(bl, xb, ctx):
        kvb, cmp_rows = ctx[0], ctx[1]
        if "c" not in cmp_p:
            slabs = jnp.arange(mp // PAGE_SIZE, dtype=jnp.int32)
            cmp_p["c"] = _compress_pages(cmp_rows, slabs, w["pos8"], w["w_pair"]).reshape(-1, KV_COLS)
        q, gates = _qg_proj(xb, w["w_q"][bl], w["w_g"][bl], tables_p, SEQ)
        return _nsa_prompt(q, gates, cmp_p["c"], kvb, expand, BATCH, SEQ)

    y_p, ctx_p, _ = _trunk(x_prompt.reshape(mp, D_MODEL), w, tables_p, SEQ, spatial_p, attend_p)
    p_cmp, p_slc, p_win = (r.reshape(BATCH, SEQ, *kv_shape) for r in ctx_p[1:])
    p_win = p_win[:, -min(WINDOW, SEQ):]

    ms = DEC_BATCH * DEC_SEQ
    pos_s = PAST_LEN + np.arange(DEC_SEQ)
    tables_s = _rotary_tables(np.tile(pos_s, DEC_BATCH))
    n_pages = PAST_LEN // PAGE_SIZE
    n_cache_blocks = PAST_LEN // SEL_BLOCK
    n_sel_blocks = -(-(PAST_LEN + DEC_SEQ) // SEL_BLOCK)
    n_cmp_s = (PAST_LEN + DEC_SEQ) // CMP_BLOCK
    pt_flat = page_table.reshape(-1)
    cache_cmp_rows = cache_cmp_kv.reshape(-1, 2 * N_KV, HEAD_DIM)
    cache_slc_rows = cache_slc_kv.reshape(-1, 2 * N_KV, HEAD_DIM)
    cache_win_rows = cache_win_kv.reshape(-1, 2 * N_KV, HEAD_DIM)
    n_buf = cache_win_kv.shape[1]
    cmp_s = {}

    def spatial_s(l):
        c = min(DEC_SEQ, CHUNK)
        w_t = jnp.tile(w["gmlp_w_s"][l][:, :c, :c], (1, DEC_BATCH, DEC_BATCH))
        b_t = jnp.tile(w["gmlp_b_s"][l][:, :c].T, (DEC_BATCH, 1))
        return ms, c, w_t, b_t

    def attend_s(bl, xb, ctx):
        if "c" not in cmp_s:
            kv = jnp.concatenate([r.reshape(ms, KV_COLS) for r in ctx[1:]], axis=1)
            cmp_s["c"] = _compress_pages(cache_cmp_rows, pt_flat, w["pos8"], w["w_pair"]).reshape(-1, KV_COLS)
            new = kv.reshape(DEC_BATCH, DEC_SEQ, N_BRANCH * KV_COLS)
            cmp_s["new"] = jnp.pad(new, ((0, 0), (0, 8 - DEC_SEQ), (0, 0)))
        q, gates = _qg_proj(xb, w["w_q"][bl], w["w_g"][bl], tables_s, ms)
        q5 = q.reshape(DEC_BATCH, DEC_SEQ, N_KV, HPG, HEAD_DIM)
        qs = q5.transpose(0, 2, 3, 1, 4).reshape(DEC_BATCH, N_KV, HPG * DEC_SEQ, HEAD_DIM)
        o_cmp, imp = _smp_cmp(qs, cmp_s["c"], n_cmp_s)
        idx = _smp_topk(imp.reshape(DEC_BATCH * N_KV * DEC_SEQ, n_cmp_s), n_sel_blocks)[:, :TOP_N]
        q8 = jnp.pad(q5.transpose(0, 2, 1, 3, 4), ((0, 0), (0, 0), (0, 0), (0, 8 - HPG), (0, 0)))
        o_slc, o_win = _smp_attn(idx.reshape(-1), pt_flat, q8, cache_slc_rows, cmp_s["new"], cache_win_rows,
                                 n_pages, n_cache_blocks, n_buf)
        o_cmp = o_cmp.reshape(DEC_BATCH, N_KV, HPG, DEC_SEQ, HEAD_DIM).transpose(0, 3, 1, 2, 4).reshape(ms, D_MODEL)
        o_slc = o_slc[:, :, :, :HPG].transpose(0, 2, 1, 3, 4).reshape(ms, D_MODEL)
        o_win = o_win[:, :, :, :HPG].transpose(0, 2, 1, 3, 4).reshape(ms, D_MODEL)
        return _gate_combine(gates, o_cmp, o_slc, o_win)

    y_s, ctx_s, v_s = _trunk(x_sample.reshape(ms, D_MODEL), w, tables_s, ms, spatial_s, attend_s)
    s_cmp, s_slc, s_new = (r.reshape(DEC_BATCH, DEC_SEQ, *kv_shape) for r in ctx_s[1:])
    s_win = jnp.concatenate([cache_win_kv, s_new], axis=1)[:, -min(WINDOW, PAST_LEN + DEC_SEQ):]
    s_v = jnp.stack([v.reshape(DEC_BATCH, DEC_SEQ, D_GATE) for v in v_s])

    return (y_p.reshape(BATCH, SEQ, D_MODEL), y_s.reshape(DEC_BATCH, DEC_SEQ, D_MODEL), p_cmp, p_slc, p_win,
            s_cmp, s_slc, s_win, s_v)
```

```python
import functools
import math

import numpy as np
import jax
import jax.numpy as jnp
from jax import lax
from jax.experimental import pallas as pl
from jax.experimental.pallas import tpu as pltpu

D_MODEL = 2048
BATCH = 4
SEQ = 2048
DEPTH = 4
DEC_BATCH = 8
DEC_SEQ = 4
PAST_LEN = 16384
PAGE_SIZE = 128

N_A_LAYERS = DEPTH // 2
N_B_LAYERS = DEPTH - N_A_LAYERS
D_FF = 5632
D_GATE = D_MODEL
CHUNK = 128
N_GROUPS_A = 8
GROUP_DIM_A = D_GATE // N_GROUPS_A
HEAD_DIM = 128
N_HEADS = D_MODEL // HEAD_DIM
N_KV = 4
HPG = N_HEADS // N_KV
ROT_DIM = HEAD_DIM // 4
ROPE_THETA = 500000.0
CMP_BLOCK = 64
SEL_BLOCK = 64
TOP_N = 16
WINDOW = 512
N_BRANCH = 3
ALPHA = (2 * DEPTH) ** 0.25
LN_EPS = 1e-5
NEG = -1e30
FORCED = 1e4

F32 = jnp.float32
BF16 = jnp.bfloat16

V7X_VMEM_LIMIT_BYTES = 56 * 1024 * 1024
LOG2E = math.log2(math.e)
LANES = 128
KV_COLS = 2 * N_KV * HEAD_DIM
GATE_COLS = HPG * N_BRANCH
KEY_CHUNK = 512
PAGES_PER_STEP = 16
Q_TILE = 256


def _params(*sem):
    return pltpu.CompilerParams(dimension_semantics=sem, vmem_limit_bytes=V7X_VMEM_LIMIT_BYTES)


def _row_tile(m, cap=512):
    return m if m <= cap else cap


def _dot(a, b):
    return jnp.dot(a, b, preferred_element_type=F32)


def _dot_nt(a, b):
    return lax.dot_general(a, b, (((1,), (1,)), ((), ())), preferred_element_type=F32)


def _dot_tn(a, b):
    return lax.dot_general(a, b, (((0,), (0,)), ((), ())), preferred_element_type=F32)


def _layer_norm(x, g, b):
    mu = jnp.mean(x, axis=-1, keepdims=True)
    d = x - mu
    var = jnp.mean(d * d, axis=-1, keepdims=True)
    return d * lax.rsqrt(var + LN_EPS) * g + b


def _sigmoid(x):
    return 1.0 / (1.0 + jnp.exp(-x))


def _softmax_rows(s):
    m = jnp.max(s, axis=-1, keepdims=True)
    p = jnp.exp(s - m)
    return p / jnp.sum(p, axis=-1, keepdims=True)


def _rotate(x, cos_t, sin_lo, sin_hi):
    half = ROT_DIM // 2
    return (x * cos_t + pltpu.roll(x, LANES - half, 1) * sin_lo + pltpu.roll(x, half, 1) * sin_hi)


def _rotary_tables(pos):
    half = ROT_DIM // 2
    inv = ROPE_THETA ** (-jnp.arange(half, dtype=F32) / half)
    ang = jnp.asarray(pos).astype(F32)[:, None] * inv[None, :]
    cos, sin = jnp.cos(ang), jnp.sin(ang)
    rows = ang.shape[0]
    pad = jnp.zeros((rows, LANES - ROT_DIM), F32)
    zero = jnp.zeros((rows, half), F32)
    cos_t = jnp.concatenate([cos, cos, pad + 1.0], axis=1)
    sin_lo = jnp.concatenate([-sin, zero, pad], axis=1)
    sin_hi = jnp.concatenate([zero, sin, pad], axis=1)
    return cos_t, sin_lo, sin_hi


def _ffn_kernel(xb_ref, x_ref, xsb_ref, xs_ref, wg_ref, wu_ref, wo_ref, g_ref, b_ref,
                y_ref, yb_ref, ys_ref, ysb_ref, accs_ref):
    i, f = pl.program_id(0), pl.program_id(1)
    tm = xb_ref.shape[0]
    last = pl.num_programs(1) - 1

    def swiglu(lhs):
        gate = _dot(lhs, wg_ref[...])
        up = _dot(lhs, wu_ref[...])
        return _dot((gate * _sigmoid(gate) * up).astype(BF16), wo_ref[...])

    @pl.when(f == 0)
    def _():
        y_ref[...] = jnp.zeros_like(y_ref)

    @pl.when(i == 0)
    def _():
        @pl.when(f == 0)
        def _():
            accs_ref[...] = jnp.zeros_like(accs_ref)

        part = swiglu(jnp.concatenate([xb_ref[...], xsb_ref[...]], axis=0))
        y_ref[...] += part[:tm]
        accs_ref[...] += part[tm:]

        @pl.when(f == last)
        def _():
            ys = _layer_norm(ALPHA * xs_ref[...] + 0.5 * accs_ref[...], g_ref[...], b_ref[...])
            ys_ref[...] = ys
            ysb_ref[...] = ys.astype(BF16)

    @pl.when(i > 0)
    def _():
        y_ref[...] += swiglu(xb_ref[...])

    @pl.when(f == last)
    def _():
        y = _layer_norm(ALPHA * x_ref[...] + 0.5 * y_ref[...], g_ref[...], b_ref[...])
        y_ref[...] = y
        yb_ref[...] = y.astype(BF16)


def _ffn(x, xb, xs, xsb, w_in, w_out, l, s, g, b):
    m, ms = x.shape[0], xs.shape[0]
    tm = _row_tile(m)
    tf = 512
    nf = D_FF // tf
    row = lambda i, f: (i, 0)
    fixed = lambda i, f: (0, 0)
    return pl.pallas_call(
        _ffn_kernel,
        grid=(m // tm, nf),
        in_specs=[
            pl.BlockSpec((tm, D_MODEL), row),
            pl.BlockSpec((tm, D_MODEL), row),
            pl.BlockSpec((ms, D_MODEL), fixed),
            pl.BlockSpec((ms, D_MODEL), fixed),
            pl.BlockSpec((None, None, D_MODEL, tf), lambda i, f: (l, s, 0, f)),
            pl.BlockSpec((None, None, D_MODEL, tf), lambda i, f: (l, s, 0, nf + f)),
            pl.BlockSpec((None, None, tf, D_MODEL), lambda i, f: (l, s, f, 0)),
            pl.BlockSpec((1, D_MODEL), fixed),
            pl.BlockSpec((1, D_MODEL), fixed),
        ],
        out_specs=[pl.BlockSpec((tm, D_MODEL), row), pl.BlockSpec((tm, D_MODEL), row),
                   pl.BlockSpec((ms, D_MODEL), fixed), pl.BlockSpec((ms, D_MODEL), fixed)],
        out_shape=[jax.ShapeDtypeStruct((m, D_MODEL), F32), jax.ShapeDtypeStruct((m, D_MODEL), BF16),
                   jax.ShapeDtypeStruct((ms, D_MODEL), F32), jax.ShapeDtypeStruct((ms, D_MODEL), BF16)],
        scratch_shapes=[pltpu.VMEM((ms, D_MODEL), F32)],
        compiler_params=_params("arbitrary", "arbitrary"),
        name="ffn",
    )(xb, x, xsb, xs, w_in, w_in, w_out, g, b)


def _gmlp_in_kernel(xb_ref, w_ref, bias_ref, lg_ref, lb_ref, z_ref):
    n = pl.program_id(0)
    z = _dot(xb_ref[...], w_ref[...]) + bias_ref[...]
    z = 0.5 * z * (1.0 + lax.erf(z * (1.0 / math.sqrt(2.0))))

    @pl.when(n == 0)
    def _():
        z_ref[...] = z

    @pl.when(n == 1)
    def _():
        z_ref[...] = _layer_norm(z, lg_ref[...], lb_ref[...])


def _gmlp_in(xb, w, bias, lg, lb):
    m = xb.shape[0]
    tm = _row_tile(m)
    return pl.pallas_call(
        _gmlp_in_kernel,
        grid=(2, m // tm),
        in_specs=[
            pl.BlockSpec((tm, D_MODEL), lambda n, i: (i, 0)),
            pl.BlockSpec((D_MODEL, D_GATE), lambda n, i: (0, n)),
            pl.BlockSpec((1, D_GATE), lambda n, i: (0, n)),
            pl.BlockSpec((1, D_GATE), lambda n, i: (0, 0)),
            pl.BlockSpec((1, D_GATE), lambda n, i: (0, 0)),
        ],
        out_specs=pl.BlockSpec((tm, D_GATE), lambda n, i: (i, n)),
        out_shape=jax.ShapeDtypeStruct((m, 2 * D_GATE), F32),
        compiler_params=_params("arbitrary", "arbitrary"),
        name="gmlp_in",
    )(xb, w, bias, lg, lb)


def _gmlp_out_kernel(u_ref, v_ref, x_ref, ws_ref, bs_ref, wo_ref, g_ref, b_ref, y_ref, yb_ref, *, rows, causal):
    tm = u_ref.shape[0]
    r_i = lax.broadcasted_iota(jnp.int32, (rows, rows), 0)
    c_i = lax.broadcasted_iota(jnp.int32, (rows, rows), 1)
    keep = ((r_i // causal) == (c_i // causal)) & (c_i <= r_i)
    w_sp = [jnp.where(keep, ws_ref[g], 0.0).astype(BF16) for g in range(N_GROUPS_A)]
    bs = bs_ref[...]
    chunks = []
    for ch in range(tm // rows):
        u = u_ref[ch * rows:(ch + 1) * rows, :]
        v = v_ref[ch * rows:(ch + 1) * rows, :]
        parts = []
        for g in range(N_GROUPS_A):
            lo, hi = g * GROUP_DIM_A, (g + 1) * GROUP_DIM_A
            mixed = _dot(w_sp[g], v[:, lo:hi].astype(BF16)) + bs[:, g:g + 1]
            parts.append((u[:, lo:hi] * mixed).astype(BF16))
        chunks.append(jnp.concatenate(parts, axis=1))
    y = chunks[0] if len(chunks) == 1 else jnp.concatenate(chunks, axis=0)
    f = _dot(y, wo_ref[...])
    out = _layer_norm(ALPHA * x_ref[...] + f, g_ref[...], b_ref[...])
    y_ref[...] = out
    yb_ref[...] = out.astype(BF16)


def _gmlp_out(z, x, w_sp, b_sp, w_out, g, b, rows, causal):
    m = x.shape[0]
    tm = _row_tile(m)
    row = lambda i: (i, 0)
    kern = functools.partial(_gmlp_out_kernel, rows=rows, causal=causal)
    return pl.pallas_call(
        kern,
        grid=(m // tm,),
        in_specs=[
            pl.BlockSpec((tm, D_GATE), lambda i: (i, 0)),
            pl.BlockSpec((tm, D_GATE), lambda i: (i, 1)),
            pl.BlockSpec((tm, D_MODEL), row),
            pl.BlockSpec((N_GROUPS_A, rows, rows), lambda i: (0, 0, 0)),
            pl.BlockSpec((rows, N_GROUPS_A), lambda i: (0, 0)),
            pl.BlockSpec((D_GATE, D_MODEL), lambda i: (0, 0), pipeline_mode=pl.Buffered(1)),
            pl.BlockSpec((1, D_MODEL), lambda i: (0, 0)),
            pl.BlockSpec((1, D_MODEL), lambda i: (0, 0)),
        ],
        out_specs=[pl.BlockSpec((tm, D_MODEL), row), pl.BlockSpec((tm, D_MODEL), row)],
        out_shape=[jax.ShapeDtypeStruct((m, D_MODEL), F32), jax.ShapeDtypeStruct((m, D_MODEL), BF16)],
        compiler_params=_params("parallel"),
        name="gmlp_out",
    )(z, z, x, w_sp, b_sp, w_out, g, b)


def _proj_norm_kernel(a_ref, x_ref, w_ref, g_ref, b_ref, y_ref, yb_ref):
    out = _layer_norm(ALPHA * x_ref[...] + _dot(a_ref[...], w_ref[...]), g_ref[...], b_ref[...])
    y_ref[...] = out
    yb_ref[...] = out.astype(BF16)


def _proj_norm(a, x, w, g, b):
    m = x.shape[0]
    tm = _row_tile(m)
    row = lambda i: (i, 0)
    return pl.pallas_call(
        _proj_norm_kernel,
        grid=(m // tm,),
        in_specs=[
            pl.BlockSpec((tm, D_MODEL), row),
            pl.BlockSpec((tm, D_MODEL), row),
            pl.BlockSpec((D_MODEL, D_MODEL), lambda i: (0, 0), pipeline_mode=pl.Buffered(1)),
            pl.BlockSpec((1, D_MODEL), lambda i: (0, 0)),
            pl.BlockSpec((1, D_MODEL), lambda i: (0, 0)),
        ],
        out_specs=[pl.BlockSpec((tm, D_MODEL), row), pl.BlockSpec((tm, D_MODEL), row)],
        out_shape=[jax.ShapeDtypeStruct((m, D_MODEL), F32), jax.ShapeDtypeStruct((m, D_MODEL), BF16)],
        compiler_params=_params("parallel"),
        name="proj_norm",
    )(a, x, w, g, b)


def _kv_proj_kernel(xb_ref, w_ref, cos_ref, slo_ref, shi_ref, kvb_ref, cmp_hbm, slc_hbm, win_hbm, stage, sem):
    i, n = pl.program_id(0), pl.program_id(1)
    tm = xb_ref.shape[0]
    step = i * pl.num_programs(1) + n
    last = pl.num_programs(0) * pl.num_programs(1) - 1
    slot = step % 2
    outs = (cmp_hbm, slc_hbm, win_hbm)

    def copies(out_hbm, c, slot_):
        return [pltpu.make_async_copy(stage.at[slot_, :, pl.ds(g * HEAD_DIM, HEAD_DIM)],
                                      out_hbm.at[pl.ds(i * tm, tm), c * N_KV + g, :], sem.at[slot_])
                for g in range(N_KV)]

    @pl.when(step >= 2)
    def _():
        for cp in copies(cmp_hbm, 0, slot):
            cp.wait()

    acc = _dot(xb_ref[...], w_ref[...])
    for col in range(2 * N_BRANCH):
        @pl.when(n == col)
        def _(col=col):
            br, c = divmod(col, 2)
            if c == 0:
                cos_t, s_lo, s_hi = cos_ref[...], slo_ref[...], shi_ref[...]
                val = jnp.concatenate([_rotate(acc[:, h * HEAD_DIM:(h + 1) * HEAD_DIM], cos_t, s_lo, s_hi)
                                       for h in range(N_KV)], axis=1)
            else:
                val = acc
            stage[slot] = val
            kvb_ref[...] = val.astype(BF16)
            for cp in copies(outs[br], c, slot):
                cp.start()

    @pl.when(step == last)
    def _():
        for cp in copies(cmp_hbm, 0, 1 - slot) + copies(cmp_hbm, 0, slot):
            cp.wait()


def _kv_proj(xb, w, tables, pos_rows):
    m = xb.shape[0]
    tm = _row_tile(m, 1024)
    half = N_KV * HEAD_DIM
    ncol = N_BRANCH * 2
    nt = pos_rows // tm
    tab = lambda i, n: (i % nt, 0)
    rows = jax.ShapeDtypeStruct((m, 2 * N_KV, HEAD_DIM), F32)
    hbm = pl.BlockSpec(memory_space=pl.ANY)
    return pl.pallas_call(
        _kv_proj_kernel,
        grid=(m // tm, ncol),
        in_specs=[
            pl.BlockSpec((tm, D_MODEL), lambda i, n: (i, 0)),
            pl.BlockSpec((D_MODEL, half), lambda i, n: (0, n)),
            pl.BlockSpec((tm, LANES), tab),
            pl.BlockSpec((tm, LANES), tab),
            pl.BlockSpec((tm, LANES), tab),
        ],
        out_specs=[pl.BlockSpec((tm, half), lambda i, n: (i, n)), hbm, hbm, hbm],
        out_shape=[jax.ShapeDtypeStruct((m, ncol * half), BF16), rows, rows, rows],
        scratch_shapes=[pltpu.VMEM((2, tm, half), F32), pltpu.SemaphoreType.DMA((2,))],
        compiler_params=_params("arbitrary", "arbitrary"),
        name="kv_proj",
    )(xb, w, *tables)


def _qg_proj_kernel(xb_ref, wq_ref, wg_ref, cos_ref, slo_ref, shi_ref, q_ref, gt_ref):
    xb = xb_ref[...]
    acc = _dot(xb, wq_ref[...])
    cos_t, s_lo, s_hi = cos_ref[...], slo_ref[...], shi_ref[...]
    for h in range(N_HEADS):
        lo, hi = h * HEAD_DIM, (h + 1) * HEAD_DIM
        q_ref[:, lo:hi] = _rotate(acc[:, lo:hi], cos_t, s_lo, s_hi).astype(BF16)
    gt_ref[...] = _sigmoid(_dot(xb, wg_ref[...]))


def _qg_proj(xb, wq, wg, tables, pos_rows):
    m = xb.shape[0]
    tm = _row_tile(m)
    nt = pos_rows // tm
    tab = lambda i: (i % nt, 0)
    gcols = N_KV * LANES
    return pl.pallas_call(
        _qg_proj_kernel,
        grid=(m // tm,),
        in_specs=[
            pl.BlockSpec((tm, D_MODEL), lambda i: (i, 0)),
            pl.BlockSpec((D_MODEL, D_MODEL), lambda i: (0, 0), pipeline_mode=pl.Buffered(1)),
            pl.BlockSpec((D_MODEL, gcols), lambda i: (0, 0)),
            pl.BlockSpec((tm, LANES), tab),
            pl.BlockSpec((tm, LANES), tab),
            pl.BlockSpec((tm, LANES), tab),
        ],
        out_specs=[pl.BlockSpec((tm, D_MODEL), lambda i: (i, 0)), pl.BlockSpec((tm, gcols), lambda i: (i, 0))],
        out_shape=[jax.ShapeDtypeStruct((m, D_MODEL), BF16), jax.ShapeDtypeStruct((m, gcols), F32)],
        compiler_params=_params("parallel"),
        name="qg_proj",
    )(xb, wq, wg, *tables)


def _compress_pages_kernel(tbl_ref, *refs):
    del tbl_ref
    npg = PAGES_PER_STEP
    pages = refs[:npg]
    pos_ref, w_ref, out_ref = refs[npg:]
    per_page = PAGE_SIZE // CMP_BLOCK
    rows = npg * per_page * 2 * N_KV
    acc = jnp.zeros((rows, 2 * HEAD_DIM), F32)
    for j in range(CMP_BLOCK // 2):
        halves = []
        for l in (2 * j, 2 * j + 1):
            pos_l = pos_ref[l]
            tiles = [pages[p][l + CMP_BLOCK * h] + pos_l for p in range(npg) for h in range(per_page)]
            halves.append(jnp.concatenate(tiles, axis=0).astype(BF16))
        acc = acc + _dot(jnp.concatenate(halves, axis=1), w_ref[j])
    is_k = (lax.broadcasted_iota(jnp.int32, (rows, HEAD_DIM), 0) % (2 * N_KV)) < N_KV
    out_ref[...] = jnp.where(is_k, acc[:, :HEAD_DIM], acc[:, HEAD_DIM:])


def _compress_pages(cache, table, pos8, w_pair):
    npg = PAGES_PER_STEP
    rows = npg * (PAGE_SIZE // CMP_BLOCK) * 2 * N_KV

    def page_spec(p):
        return pl.BlockSpec((PAGE_SIZE, 2 * N_KV, HEAD_DIM), lambda i, tbl: (tbl[i * npg + p], 0, 0))

    grid_spec = pltpu.PrefetchScalarGridSpec(
        num_scalar_prefetch=1,
        grid=(table.shape[0] // npg,),
        in_specs=[page_spec(p) for p in range(npg)] + [
            pl.BlockSpec((CMP_BLOCK, 2 * N_KV, HEAD_DIM), lambda i, tbl: (0, 0, 0)),
            pl.BlockSpec((CMP_BLOCK // 2, 2 * HEAD_DIM, 2 * HEAD_DIM), lambda i, tbl: (0, 0, 0)),
        ],
        out_specs=pl.BlockSpec((rows, HEAD_DIM), lambda i, tbl: (i, 0)),
    )
    return pl.pallas_call(
        _compress_pages_kernel,
        grid_spec=grid_spec,
        out_shape=jax.ShapeDtypeStruct((table.shape[0] // npg * rows, HEAD_DIM), F32),
        compiler_params=_params("arbitrary"),
        name="compress_pages",
    )(table, *([cache] * npg), pos8, w_pair)


def _rank_select(score_t, n_blocks):
    idx = lax.broadcasted_iota(jnp.int32, score_t.shape, 0)
    rank = jnp.zeros(score_t.shape, F32)
    for i in range(n_blocks):
        row = score_t[i:i + 1, :]
        rank = rank + jnp.where(idx > i, jnp.where(row >= score_t, 1.0, 0.0), jnp.where(row > score_t, 1.0, 0.0))
    return jnp.where(rank < float(min(TOP_N, n_blocks)), 1.0, 0.0)


def _nsa_prompt_kernel(q_ref, gt_ref, ck_ref, cv_ref, sk_ref, sv_ref, wk_ref, wv_ref, e_ref, o_ref):
    tq = q_ref.shape[0]
    t_len = sk_ref.shape[0]
    nblk = ck_ref.shape[0]
    i = pl.program_id(2)
    scale = HEAD_DIM ** -0.5
    q = q_ref[...]
    q4 = jnp.concatenate([q[:, h * HEAD_DIM:(h + 1) * HEAD_DIM] for h in range(HPG)], axis=0)
    pos_c = i * tq + lax.broadcasted_iota(jnp.int32, (tq, 1), 0)
    pos_r = i * tq + lax.broadcasted_iota(jnp.int32, (1, tq), 1)
    pos4_r = jnp.concatenate([pos_r] * HPG, axis=1)

    c2 = scale * LOG2E

    ck = ck_ref[...].astype(BF16)
    cv = cv_ref[...].astype(BF16)
    blk_c = lax.broadcasted_iota(jnp.int32, (nblk, 1), 0)
    vis_t = (blk_c + 1) * CMP_BLOCK - 1 <= pos4_r
    s_t = jnp.where(vis_t, _dot_nt(ck, q4), NEG)
    p_t = jnp.exp2((s_t - jnp.max(s_t, axis=0, keepdims=True)) * c2)
    p_t = p_t * (1.0 / jnp.sum(p_t, axis=0, keepdims=True))
    p_t = p_t * jnp.where(pos4_r >= CMP_BLOCK - 1, 1.0, 0.0)
    o_cmp = _dot_tn(p_t.astype(BF16), cv)
    imp_t = p_t[:, 0:tq]
    for h in range(1, HPG):
        imp_t = imp_t + p_t[:, h * tq:(h + 1) * tq]
    cur = pos_r // SEL_BLOCK
    forced = (blk_c == 0) | (blk_c == cur) | (blk_c == cur - 1)
    score_t = jnp.where(forced, FORCED, jnp.where(blk_c <= cur, imp_t, -1.0))
    sel_t = _rank_select(score_t, nblk)
    sel_b = sel_t.astype(BF16)
    key_r = lax.broadcasted_iota(jnp.int32, (1, KEY_CHUNK), 1)

    def body(kc, carry):
        m, l, acc = carry
        start = pl.multiple_of(kc * KEY_CHUNK, KEY_CHUNK)
        k = sk_ref[pl.ds(start, KEY_CHUNK), :]
        v = sv_ref[pl.ds(start, KEY_CHUNK), :]
        member = _dot_tn(sel_b, e_ref[kc])
        bias = jnp.where(start + key_r <= pos_c, (member - 1.0) * (-NEG), NEG)
        s = _dot_nt(q4, k) + jnp.concatenate([bias] * HPG, axis=0)
        m_new = jnp.maximum(m, jnp.max(s, axis=-1, keepdims=True))
        a = jnp.exp2((m - m_new) * c2)
        pr = jnp.exp2((s - m_new) * c2)
        l = a * l + jnp.sum(pr, axis=-1, keepdims=True)
        acc = a * acc + _dot(pr.astype(BF16), v)
        return m_new, l, acc

    n_chunks = (i * tq + tq + KEY_CHUNK - 1) // KEY_CHUNK
    init = (jnp.full((HPG * tq, 1), NEG, F32), jnp.zeros((HPG * tq, 1), F32), jnp.zeros((HPG * tq, HEAD_DIM), F32))
    _, l_s, acc_s = lax.fori_loop(0, n_chunks, body, init)
    o_slc = acc_s * (1.0 / l_s)

    n_win = WINDOW + tq
    start = pl.multiple_of(jnp.maximum(i * tq - WINDOW, 0), tq)
    kw = wk_ref[pl.ds(start, n_win), :]
    vw = wv_ref[pl.ds(start, n_win), :]
    d = pos_c - (start + lax.broadcasted_iota(jnp.int32, (1, n_win), 1))
    bias_w = jnp.where((d >= 0) & (d < WINDOW), 0.0, NEG)
    sw = _dot_nt(q4, kw) + jnp.concatenate([bias_w] * HPG, axis=0)
    pw = jnp.exp2((sw - jnp.max(sw, axis=-1, keepdims=True)) * c2)
    o_win = _dot(pw.astype(BF16), vw) * (1.0 / jnp.sum(pw, axis=-1, keepdims=True))

    gt = gt_ref[...]
    for h in range(HPG):
        r0, r1 = h * tq, (h + 1) * tq
        c0 = h * N_BRANCH
        og = (gt[:, c0:c0 + 1] * o_cmp[r0:r1] + gt[:, c0 + 1:c0 + 2] * o_slc[r0:r1]
              + gt[:, c0 + 2:c0 + 3] * o_win[r0:r1])
        o_ref[:, h * HEAD_DIM:(h + 1) * HEAD_DIM] = og.astype(BF16)


def _nsa_prompt(q, gates, cmp_c, kvb, expand, batch, t_len):
    tq = Q_TILE
    nq = t_len // tq
    nblk = t_len // CMP_BLOCK
    gw = HPG * HEAD_DIM
    qrow = lambda b, g, i: (b * nq + i, g)

    def kv_spec(col0):
        return pl.BlockSpec((t_len, HEAD_DIM), lambda b, g, i: (b, col0 + g))

    return pl.pallas_call(
        _nsa_prompt_kernel,
        grid=(batch, N_KV, nq),
        in_specs=[
            pl.BlockSpec((tq, gw), qrow),
            pl.BlockSpec((tq, LANES), qrow),
            pl.BlockSpec((nblk, HEAD_DIM), lambda b, g, i: (b, g)),
            pl.BlockSpec((nblk, HEAD_DIM), lambda b, g, i: (b, N_KV + g)),
            kv_spec(2 * N_KV), kv_spec(3 * N_KV), kv_spec(4 * N_KV), kv_spec(5 * N_KV),
            pl.BlockSpec((t_len // KEY_CHUNK, nblk, KEY_CHUNK), lambda b, g, i: (0, 0, 0)),
        ],
        out_specs=pl.BlockSpec((tq, gw), qrow),
        out_shape=jax.ShapeDtypeStruct((batch * t_len, D_MODEL), BF16),
        compiler_params=_params("parallel", "parallel", "arbitrary"),
        name="nsa_prompt",
    )(q, gates, cmp_c, cmp_c, kvb, kvb, kvb, kvb, expand)


def _smp_cmp_kernel(q_ref, ck_ref, cv_ref, o_ref, imp_ref):
    nblk = ck_ref.shape[0]
    scale = HEAD_DIM ** -0.5
    q = q_ref[0, 0]
    rows = q.shape[0]
    tok = lax.broadcasted_iota(jnp.int32, (rows, 1), 0) % DEC_SEQ
    pos = PAST_LEN + tok
    blk = lax.broadcasted_iota(jnp.int32, (1, nblk), 1)
    vis = (blk + 1) * CMP_BLOCK - 1 <= pos
    p = _softmax_rows(jnp.where(vis, _dot_nt(q, ck_ref[...].astype(BF16)) * scale, NEG))
    p = p * jnp.where(pos >= CMP_BLOCK - 1, 1.0, 0.0)
    o_ref[0, 0] = _dot(p.astype(BF16), cv_ref[...].astype(BF16))
    imp = p[0:DEC_SEQ]
    for h in range(1, HPG):
        imp = imp + p[h * DEC_SEQ:(h + 1) * DEC_SEQ]
    imp_ref[0, 0] = imp


def _smp_cmp(qs, cmp_c, nblk):
    rows = HPG * DEC_SEQ
    return pl.pallas_call(
        _smp_cmp_kernel,
        grid=(DEC_BATCH, N_KV),
        in_specs=[
            pl.BlockSpec((1, 1, rows, HEAD_DIM), lambda b, g: (b, g, 0, 0)),
            pl.BlockSpec((nblk, HEAD_DIM), lambda b, g: (b, g)),
            pl.BlockSpec((nblk, HEAD_DIM), lambda b, g: (b, N_KV + g)),
        ],
        out_specs=[pl.BlockSpec((1, 1, rows, HEAD_DIM), lambda b, g: (b, g, 0, 0)),
                   pl.BlockSpec((1, 1, DEC_SEQ, nblk), lambda b, g: (b, g, 0, 0))],
        out_shape=[jax.ShapeDtypeStruct((DEC_BATCH, N_KV, rows, HEAD_DIM), F32),
                   jax.ShapeDtypeStruct((DEC_BATCH, N_KV, DEC_SEQ, nblk), F32)],
        compiler_params=_params("parallel", "parallel"),
        name="smp_cmp",
    )(qs, cmp_c, cmp_c)


def _smp_topk_kernel(imp_ref, idx_ref, *, n_sel_blocks):
    imp = imp_ref[...]
    rows, ncmp = imp.shape
    width = ncmp + LANES
    imp = jnp.concatenate([imp, jnp.zeros((rows, LANES), F32)], axis=1)
    j = lax.broadcasted_iota(jnp.int32, (rows, width), 1)
    tok = lax.broadcasted_iota(jnp.int32, (rows, 1), 0) % DEC_SEQ
    cur = (PAST_LEN + tok) // SEL_BLOCK
    forced = (j == 0) | (j == cur) | (j == cur - 1)
    score = jnp.where(forced, FORCED, jnp.where(j <= cur, imp, -1.0))
    score = jnp.where(j < n_sel_blocks, score, -jnp.inf)
    lane = lax.broadcasted_iota(jnp.int32, (rows, LANES), 1)
    out = jnp.zeros((rows, LANES), jnp.int32)
    jf = j.astype(F32)
    for n in range(min(TOP_N, n_sel_blocks)):
        m = jnp.max(score, axis=-1, keepdims=True)
        pick = jnp.min(jnp.where(score == m, jf, float(width)), axis=-1, keepdims=True)
        out = jnp.where(lane == n, pick.astype(jnp.int32), out)
        score = jnp.where(jf == pick, -jnp.inf, score)
    idx_ref[...] = out


def _smp_topk(imp, n_sel_blocks):
    rows = imp.shape[0]
    return pl.pallas_call(
        functools.partial(_smp_topk_kernel, n_sel_blocks=n_sel_blocks),
        out_shape=jax.ShapeDtypeStruct((rows, LANES), jnp.int32),
        name="smp_topk",
    )(imp)


def _smp_attn_kernel(idx_ref, pt_ref, q_ref, nsk_ref, nsv_ref, nwk_ref, nwv_ref, slc_hbm, win_hbm,
                     oslc_ref, owin_ref, kbuf, vbuf, wbuf, sem, *, n_pages, n_cache_blocks):
    n_sel = TOP_N
    per_page = PAGE_SIZE // SEL_BLOCK
    n_sel_keys = n_sel * SEL_BLOCK
    n_buf = wbuf.shape[2] - LANES
    s = pl.program_id(0)
    slot = s % 2

    def copies(step, slot_, for_wait):
        b, g = step // N_KV, step % N_KV
        out = []
        for t in range(DEC_SEQ):
            for n in range(n_sel):
                if for_wait:
                    row0 = 0
                else:
                    bid = jnp.minimum(idx_ref[(step * DEC_SEQ + t) * n_sel + n], n_cache_blocks - 1)
                    row0 = (pt_ref[b * n_pages + bid // per_page] * per_page + bid % per_page) * SEL_BLOCK
                dst = pl.ds(n * SEL_BLOCK, SEL_BLOCK)
                out.append(pltpu.make_async_copy(slc_hbm.at[pl.ds(row0, SEL_BLOCK), g, :],
                                                 kbuf.at[slot_, t, dst, :], sem.at[slot_]))
                out.append(pltpu.make_async_copy(slc_hbm.at[pl.ds(row0, SEL_BLOCK), N_KV + g, :],
                                                 vbuf.at[slot_, t, dst, :], sem.at[slot_]))
        for c in range(2):
            out.append(pltpu.make_async_copy(win_hbm.at[pl.ds(b * n_buf, n_buf), c * N_KV + g, :],
                                             wbuf.at[slot_, c, pl.ds(0, n_buf), :], sem.at[slot_]))
        return out

    @pl.when(s == 0)
    def _():
        for cp in copies(s, slot, False):
            cp.start()

    @pl.when(s + 1 < pl.num_programs(0))
    def _():
        for cp in copies(s + 1, 1 - slot, False):
            cp.start()

    for cp in copies(s, slot, True):
        cp.wait()

    scale = HEAD_DIM ** -0.5
    n_new = nsk_ref.shape[1]
    zeros = jnp.zeros((LANES, HEAD_DIM), F32)
    blk_lane = lax.broadcasted_iota(jnp.int32, (1, n_sel_keys + LANES), 1)
    in_cache = blk_lane < n_sel_keys
    off_new = blk_lane - n_sel_keys
    lane = lax.broadcasted_iota(jnp.int32, (1, n_buf + LANES), 1)
    kpos = jnp.where(lane < n_buf, PAST_LEN - n_buf + lane, PAST_LEN + lane - n_buf)
    real = (lane < n_buf) | (lane - n_buf < DEC_SEQ)

    wbuf[slot, 0, pl.ds(n_buf, LANES), :] = zeros
    wbuf[slot, 1, pl.ds(n_buf, LANES), :] = zeros
    wbuf[slot, 0, pl.ds(n_buf, n_new), :] = nwk_ref[0]
    wbuf[slot, 1, pl.ds(n_buf, n_new), :] = nwv_ref[0]
    kw = wbuf[slot, 0].astype(BF16)
    vw = wbuf[slot, 1].astype(BF16)

    for t in range(DEC_SEQ):
        pos = PAST_LEN + t
        q = q_ref[0, 0, t]
        ids = jnp.zeros((1, n_sel_keys + LANES), jnp.int32)
        has_new = jnp.zeros((1, 1), jnp.int32)
        for n in range(n_sel):
            bid = idx_ref[(s * DEC_SEQ + t) * n_sel + n]
            ids = jnp.where(blk_lane // SEL_BLOCK == n, bid, ids)
            has_new = jnp.maximum(has_new, jnp.where(bid == n_cache_blocks, 1, 0))
        kbuf[slot, t, pl.ds(n_sel_keys, LANES), :] = zeros
        vbuf[slot, t, pl.ds(n_sel_keys, LANES), :] = zeros
        kbuf[slot, t, pl.ds(n_sel_keys, n_new), :] = nsk_ref[0]
        vbuf[slot, t, pl.ds(n_sel_keys, n_new), :] = nsv_ref[0]
        tok = jnp.where(in_cache, ids * SEL_BLOCK + blk_lane % SEL_BLOCK, n_cache_blocks * SEL_BLOCK + off_new)
        src_ok = jnp.where(in_cache, jnp.where(ids != n_cache_blocks, 1, 0),
                           jnp.where(off_new < DEC_SEQ, 1, 0) * has_new)
        ok = (src_ok > 0) & (tok <= pos)
        ps = _softmax_rows(jnp.where(ok, _dot_nt(q, kbuf[slot, t].astype(BF16)) * scale, NEG))
        oslc_ref[0, 0, t] = _dot(ps.astype(BF16), vbuf[slot, t].astype(BF16))

        d = pos - kpos
        okw = real & (d >= 0) & (d < WINDOW)
        pw = _softmax_rows(jnp.where(okw, _dot_nt(q, kw) * scale, NEG))
        owin_ref[0, 0, t] = _dot(pw.astype(BF16), vw)


def _smp_attn(idx_flat, pt_flat, q8, cache_slc_rows, new_pad, cache_win_rows, n_pages, n_cache_blocks, n_buf):
    n_sel = TOP_N
    n_new = new_pad.shape[1]
    bg = lambda s, idx, pt: (s // N_KV, s % N_KV, 0, 0, 0)

    def new_spec(col0):
        return pl.BlockSpec((1, n_new, HEAD_DIM), lambda s, idx, pt: (s // N_KV, 0, col0 + s % N_KV))

    qo_spec = pl.BlockSpec((1, 1, DEC_SEQ, 8, HEAD_DIM), bg)
    grid_spec = pltpu.PrefetchScalarGridSpec(
        num_scalar_prefetch=2,
        grid=(DEC_BATCH * N_KV,),
        in_specs=[qo_spec, new_spec(2 * N_KV), new_spec(3 * N_KV), new_spec(4 * N_KV), new_spec(5 * N_KV),
                  pl.BlockSpec(memory_space=pl.ANY), pl.BlockSpec(memory_space=pl.ANY)],
        out_specs=[qo_spec, qo_spec],
        scratch_shapes=[pltpu.VMEM((2, DEC_SEQ, n_sel * SEL_BLOCK + LANES, HEAD_DIM), F32),
                        pltpu.VMEM((2, DEC_SEQ, n_sel * SEL_BLOCK + LANES, HEAD_DIM), F32),
                        pltpu.VMEM((2, 2, n_buf + LANES, HEAD_DIM), F32),
                        pltpu.SemaphoreType.DMA((2,))],
    )
    shape = jax.ShapeDtypeStruct((DEC_BATCH, N_KV, DEC_SEQ, 8, HEAD_DIM), F32)
    return pl.pallas_call(
        functools.partial(_smp_attn_kernel, n_pages=n_pages, n_cache_blocks=n_cache_blocks),
        grid_spec=grid_spec,
        out_shape=[shape, shape],
        compiler_params=_params("arbitrary"),
        name="smp_attn",
    )(idx_flat, pt_flat, q8, new_pad, new_pad, new_pad, new_pad, cache_slc_rows, cache_win_rows)


def _gate_combine_kernel(gt_ref, oc_ref, os_ref, ow_ref, o_ref):
    gt = gt_ref[...]
    for g in range(N_KV):
        for h in range(HPG):
            c0 = g * LANES + h * N_BRANCH
            lo = (g * HPG + h) * HEAD_DIM
            hi = lo + HEAD_DIM
            og = (gt[:, c0:c0 + 1] * oc_ref[:, lo:hi] + gt[:, c0 + 1:c0 + 2] * os_ref[:, lo:hi]
                  + gt[:, c0 + 2:c0 + 3] * ow_ref[:, lo:hi])
            o_ref[:, lo:hi] = og.astype(BF16)


def _gate_combine(gates, o_cmp, o_slc, o_win):
    return pl.pallas_call(
        _gate_combine_kernel,
        out_shape=jax.ShapeDtypeStruct(o_cmp.shape, BF16),
        name="gate_combine",
    )(gates, o_cmp, o_slc, o_win)


def _prep_weights(ln_g, ln_b, ffn_w_in, ffn_w_out, gmlp_w_in, gmlp_b_in, gmlp_ln_g, gmlp_ln_b, gmlp_w_s,
                  gmlp_b_s, gmlp_w_out, nsa_w_qg, nsa_w_o, w_kv, cmp_pos, w_cmp):
    nq = N_HEADS * HEAD_DIM
    wg = nsa_w_qg[:, :, nq:].reshape(N_B_LAYERS, D_MODEL, N_KV, GATE_COLS)
    wg = jnp.pad(wg, ((0, 0), (0, 0), (0, 0), (0, LANES - GATE_COLS))).reshape(N_B_LAYERS, D_MODEL, N_KV * LANES)
    pos_t = jnp.broadcast_to(cmp_pos[:, :, None, :], (CMP_BLOCK, 2, N_KV, HEAD_DIM)).reshape(CMP_BLOCK, KV_COLS)
    return dict(
        ln_g=ln_g.reshape(DEPTH, 3, 1, D_MODEL), ln_b=ln_b.reshape(DEPTH, 3, 1, D_MODEL),
        ffn_w_in=ffn_w_in.astype(BF16), ffn_w_out=ffn_w_out.astype(BF16),
        gmlp_w_in=gmlp_w_in.astype(BF16), gmlp_b_in=gmlp_b_in.reshape(N_A_LAYERS, 1, 2 * D_GATE),
        gmlp_ln_g=gmlp_ln_g.reshape(N_A_LAYERS, 1, D_GATE), gmlp_ln_b=gmlp_ln_b.reshape(N_A_LAYERS, 1, D_GATE),
        gmlp_w_s=gmlp_w_s, gmlp_b_s=gmlp_b_s, gmlp_w_out=gmlp_w_out.astype(BF16),
        w_q=nsa_w_qg[:, :, :nq].astype(BF16), w_g=wg.astype(BF16), w_o=nsa_w_o.astype(BF16),
        w_kv=w_kv.astype(BF16),
        pos8=pos_t.reshape(CMP_BLOCK, 2 * N_KV, HEAD_DIM),
        w_pair=jnp.concatenate([w_cmp[:, 0], w_cmp[:, 1]], axis=-1).astype(BF16).reshape(
            CMP_BLOCK // 2, 2 * HEAD_DIM, 2 * HEAD_DIM),
    )


def _trunks(xs, w, streams):
    xbs = [x.astype(BF16) for x in xs]
    ctxs = [None] * len(xs)
    v_rows = [[] for _ in xs]

    def ffn(l, s):
        norm = 2 * s
        out = _ffn(xs[0], xbs[0], xs[1], xbs[1], w["ffn_w_in"], w["ffn_w_out"], l, s,
                   w["ln_g"][l, norm], w["ln_b"][l, norm])
        xs[0], xbs[0], xs[1], xbs[1] = out

    for l in range(DEPTH):
        if l == N_A_LAYERS:
            for k, st in enumerate(streams):
                ctxs[k] = _kv_proj(xbs[k], w["w_kv"], st["tables"], st["pos_rows"])
        ffn(l, 0)
        for k, st in enumerate(streams):
            if l < N_A_LAYERS:
                rows, causal, w_sp, b_sp = st["spatial"](l)
                z = _gmlp_in(xbs[k], w["gmlp_w_in"][l], w["gmlp_b_in"][l], w["gmlp_ln_g"][l], w["gmlp_ln_b"][l])
                v_rows[k].append(z[:, D_GATE:])
                xs[k], xbs[k] = _gmlp_out(z, xs[k], w_sp, b_sp, w["gmlp_w_out"][l], w["ln_g"][l, 1],
                                          w["ln_b"][l, 1], rows, causal)
            else:
                o = st["attend"](l - N_A_LAYERS, xbs[k], ctxs[k])
                xs[k], xbs[k] = _proj_norm(o, xs[k], w["w_o"][l - N_A_LAYERS], w["ln_g"][l, 1], w["ln_b"][l, 1])
        ffn(l, 1)
    return [(xs[k], ctxs[k], v_rows[k]) for k in range(len(xs))]


def kernel(x_prompt, x_sample, cache_cmp_kv, cache_slc_kv, cache_win_kv, page_table, ln_g, ln_b, ffn_w_in,
           ffn_w_out, gmlp_w_in, gmlp_b_in, gmlp_ln_g, gmlp_ln_b, gmlp_w_s, gmlp_b_s, gmlp_w_out, nsa_w_qg,
           nsa_w_o, w_kv, cmp_pos, w_cmp):
    w = _prep_weights(ln_g, ln_b, ffn_w_in, ffn_w_out, gmlp_w_in, gmlp_b_in, gmlp_ln_g, gmlp_ln_b, gmlp_w_s,
                      gmlp_b_s, gmlp_w_out, nsa_w_qg, nsa_w_o, w_kv, cmp_pos, w_cmp)
    kv_shape = (2, N_KV, HEAD_DIM)

    mp = BATCH * SEQ
    tables_p = _rotary_tables(np.arange(SEQ))
    n_blk_p = SEQ // CMP_BLOCK
    key_blk = (np.arange(SEQ) // SEL_BLOCK).reshape(SEQ // KEY_CHUNK, 1, KEY_CHUNK)
    expand = jnp.asarray(key_blk == np.arange(n_blk_p)[None, :, None], BF16)
    cmp_p = {}

    def spatial_p(l):
        return CHUNK, CHUNK, w["gmlp_w_s"][l], w["gmlp_b_s"][l].T

    def attend_p(bl, xb, ctx):
        kvb, cmp_rows = ctx[0], ctx[1]
        if "c" not in cmp_p:
            slabs = jnp.arange(mp // PAGE_SIZE, dtype=jnp.int32)
            cmp_p["c"] = _compress_pages(cmp_rows, slabs, w["pos8"], w["w_pair"]).reshape(-1, KV_COLS)
        q, gates = _qg_proj(xb, w["w_q"][bl], w["w_g"][bl], tables_p, SEQ)
        return _nsa_prompt(q, gates, cmp_p["c"], kvb, expand, BATCH, SEQ)

    ms = DEC_BATCH * DEC_SEQ
    pos_s = PAST_LEN + np.arange(DEC_SEQ)
    tables_s = _rotary_tables(np.tile(pos_s, DEC_BATCH))
    n_pages = PAST_LEN // PAGE_SIZE
    n_cache_blocks = PAST_LEN // SEL_BLOCK
    n_sel_blocks = -(-(PAST_LEN + DEC_SEQ) // SEL_BLOCK)
    n_cmp_s = (PAST_LEN + DEC_SEQ) // CMP_BLOCK
    pt_flat = page_table.reshape(-1)
    cache_cmp_rows = cache_cmp_kv.reshape(-1, 2 * N_KV, HEAD_DIM)
    cache_slc_rows = cache_slc_kv.reshape(-1, 2 * N_KV, HEAD_DIM)
    cache_win_rows = cache_win_kv.reshape(-1, 2 * N_KV, HEAD_DIM)
    n_buf = cache_win_kv.shape[1]
    cmp_s = {}

    def spatial_s(l):
        c = min(DEC_SEQ, CHUNK)
        w_t = jnp.tile(w["gmlp_w_s"][l][:, :c, :c], (1, DEC_BATCH, DEC_BATCH))
        b_t = jnp.tile(w["gmlp_b_s"][l][:, :c].T, (DEC_BATCH, 1))
        return ms, c, w_t, b_t

    def attend_s(bl, xb, ctx):
        if "c" not in cmp_s:
            kv = jnp.concatenate([r.reshape(ms, KV_COLS) for r in ctx[1:]], axis=1)
            cmp_s["c"] = _compress_pages(cache_cmp_rows, pt_flat, w["pos8"], w["w_pair"]).reshape(-1, KV_COLS)
            new = kv.reshape(DEC_BATCH, DEC_SEQ, N_BRANCH * KV_COLS)
            cmp_s["new"] = jnp.pad(new, ((0, 0), (0, 8 - DEC_SEQ), (0, 0)))
        q, gates = _qg_proj(xb, w["w_q"][bl], w["w_g"][bl], tables_s, ms)
        q5 = q.reshape(DEC_BATCH, DEC_SEQ, N_KV, HPG, HEAD_DIM)
        qs = q5.transpose(0, 2, 3, 1, 4).reshape(DEC_BATCH, N_KV, HPG * DEC_SEQ, HEAD_DIM)
        o_cmp, imp = _smp_cmp(qs, cmp_s["c"], n_cmp_s)
        idx = _smp_topk(imp.reshape(DEC_BATCH * N_KV * DEC_SEQ, n_cmp_s), n_sel_blocks)[:, :TOP_N]
        q8 = jnp.pad(q5.transpose(0, 2, 1, 3, 4), ((0, 0), (0, 0), (0, 0), (0, 8 - HPG), (0, 0)))
        o_slc, o_win = _smp_attn(idx.reshape(-1), pt_flat, q8, cache_slc_rows, cmp_s["new"], cache_win_rows,
                                 n_pages, n_cache_blocks, n_buf)
        o_cmp = o_cmp.reshape(DEC_BATCH, N_KV, HPG, DEC_SEQ, HEAD_DIM).transpose(0, 3, 1, 2, 4).reshape(ms, D_MODEL)
        o_slc = o_slc[:, :, :, :HPG].transpose(0, 2, 1, 3, 4).reshape(ms, D_MODEL)
        o_win = o_win[:, :, :, :HPG].transpose(0, 2, 1, 3, 4).reshape(ms, D_MODEL)
        return _gate_combine(gates, o_cmp, o_slc, o_win)

    streams = [dict(tables=tables_p, pos_rows=SEQ, spatial=spatial_p, attend=attend_p),
               dict(tables=tables_s, pos_rows=ms, spatial=spatial_s, attend=attend_s)]
    (y_p, ctx_p, _), (y_s, ctx_s, v_s) = _trunks(
        [x_prompt.reshape(mp, D_MODEL), x_sample.reshape(ms, D_MODEL)], w, streams)
    p_cmp, p_slc, p_win = (r.reshape(BATCH, SEQ, *kv_shape) for r in ctx_p[1:])
    p_win = p_win[:, -min(WINDOW, SEQ):]
    s_cmp, s_slc, s_new = (r.reshape(DEC_BATCH, DEC_SEQ, *kv_shape) for r in ctx_s[1:])
    s_win = jnp.concatenate([cache_win_kv, s_new], axis=1)[:, -min(WINDOW, PAST_LEN + DEC_SEQ):]
    s_v = jnp.stack([v.reshape(DEC_BATCH, DEC_SEQ, D_GATE) for v in v_s])

    return (y_p.reshape(BATCH, SEQ, D_MODEL), y_s.reshape(DEC_BATCH, DEC_SEQ, D_MODEL), p_cmp, p_slc, p_win,
            s_cmp, s_slc, s_win, s_v)
```

```python
import functools
import math

import numpy as np
import jax
import jax.numpy as jnp
from jax import lax
from jax.experimental import pallas as pl
from jax.experimental.pallas import tpu as pltpu

D_MODEL = 2048
BATCH = 4
SEQ = 2048
DEPTH = 4
DEC_BATCH = 8
DEC_SEQ = 4
PAST_LEN = 16384
PAGE_SIZE = 128

N_A_LAYERS = DEPTH // 2
N_B_LAYERS = DEPTH - N_A_LAYERS
D_FF = 5632
D_GATE = D_MODEL
CHUNK = 128
N_GROUPS_A = 8
GROUP_DIM_A = D_GATE // N_GROUPS_A
HEAD_DIM = 128
N_HEADS = D_MODEL // HEAD_DIM
N_KV = 4
HPG = N_HEADS // N_KV
ROT_DIM = HEAD_DIM // 4
ROPE_THETA = 500000.0
CMP_BLOCK = 64
SEL_BLOCK = 64
TOP_N = 16
WINDOW = 512
N_BRANCH = 3
ALPHA = (2 * DEPTH) ** 0.25
LN_EPS = 1e-5
NEG = -1e30
FORCED = 1e4

F32 = jnp.float32
BF16 = jnp.bfloat16

V7X_VMEM_LIMIT_BYTES = 56 * 1024 * 1024
LOG2E = math.log2(math.e)
LANES = 128
KV_COLS = 2 * N_KV * HEAD_DIM
GATE_COLS = HPG * N_BRANCH
KEY_CHUNK = 512
PAGES_PER_STEP = 16
Q_TILE = 256


def _params(*sem):
    return pltpu.CompilerParams(dimension_semantics=sem, vmem_limit_bytes=V7X_VMEM_LIMIT_BYTES)


def _row_tile(m, cap=512):
    return m if m <= cap else cap


def _dot(a, b):
    return jnp.dot(a, b, preferred_element_type=F32)


def _dot_nt(a, b):
    return lax.dot_general(a, b, (((1,), (1,)), ((), ())), preferred_element_type=F32)


def _dot_tn(a, b):
    return lax.dot_general(a, b, (((0,), (0,)), ((), ())), preferred_element_type=F32)


def _layer_norm(x, g, b):
    mu = jnp.mean(x, axis=-1, keepdims=True)
    d = x - mu
    var = jnp.mean(d * d, axis=-1, keepdims=True)
    return d * lax.rsqrt(var + LN_EPS) * g + b


def _sigmoid(x):
    return 1.0 / (1.0 + jnp.exp(-x))


def _softmax_rows(s):
    m = jnp.max(s, axis=-1, keepdims=True)
    p = jnp.exp(s - m)
    return p / jnp.sum(p, axis=-1, keepdims=True)


def _rotate(x, cos_t, sin_lo, sin_hi):
    half = ROT_DIM // 2
    return (x * cos_t + pltpu.roll(x, LANES - half, 1) * sin_lo + pltpu.roll(x, half, 1) * sin_hi)


def _rotary_tables(pos):
    half = ROT_DIM // 2
    inv = ROPE_THETA ** (-jnp.arange(half, dtype=F32) / half)
    ang = jnp.asarray(pos).astype(F32)[:, None] * inv[None, :]
    cos, sin = jnp.cos(ang), jnp.sin(ang)
    rows = ang.shape[0]
    pad = jnp.zeros((rows, LANES - ROT_DIM), F32)
    zero = jnp.zeros((rows, half), F32)
    cos_t = jnp.concatenate([cos, cos, pad + 1.0], axis=1)
    sin_lo = jnp.concatenate([-sin, zero, pad], axis=1)
    sin_hi = jnp.concatenate([zero, sin, pad], axis=1)
    return cos_t, sin_lo, sin_hi


def _ffn_kernel(xb_ref, x_ref, xsb_ref, xs_ref, wg_ref, wu_ref, wo_ref, g_ref, b_ref, *refs, cast_next):
    i, f = pl.program_id(0), pl.program_id(1)
    tm = xb_ref.shape[0]
    last = pl.num_programs(1) - 1
    if cast_next:
        nwi_ref, nwo_ref, y_ref, yb_ref, ys_ref, ysb_ref, nwib_ref, nwob_ref, accs_ref = refs
    else:
        y_ref, yb_ref, ys_ref, ysb_ref, accs_ref = refs

    def swiglu(lhs):
        if cast_next:
            nwib_ref[...] = nwi_ref[...].astype(BF16)
            nwob_ref[...] = nwo_ref[...].astype(BF16)
        gate = _dot(lhs, wg_ref[...])
        up = _dot(lhs, wu_ref[...])
        return _dot((gate * _sigmoid(gate) * up).astype(BF16), wo_ref[...])

    @pl.when(f == 0)
    def _():
        y_ref[...] = jnp.zeros_like(y_ref)

    @pl.when(i == 0)
    def _():
        @pl.when(f == 0)
        def _():
            accs_ref[...] = jnp.zeros_like(accs_ref)

        part = swiglu(jnp.concatenate([xb_ref[...], xsb_ref[...]], axis=0))
        y_ref[...] += part[:tm]
        accs_ref[...] += part[tm:]

        @pl.when(f == last)
        def _():
            ys = _layer_norm(ALPHA * xs_ref[...] + 0.5 * accs_ref[...], g_ref[...], b_ref[...])
            ys_ref[...] = ys
            ysb_ref[...] = ys.astype(BF16)

    @pl.when(i > 0)
    def _():
        y_ref[...] += swiglu(xb_ref[...])

    @pl.when(f == last)
    def _():
        y = _layer_norm(ALPHA * x_ref[...] + 0.5 * y_ref[...], g_ref[...], b_ref[...])
        y_ref[...] = y
        yb_ref[...] = y.astype(BF16)


def _ffn(x, xb, xs, xsb, w_in_b, w_out_b, g, b, nxt=None):
    m, ms = x.shape[0], xs.shape[0]
    tm = _row_tile(m)
    tf = 512
    nf = D_FF // tf
    row = lambda i, f: (i, 0)
    fixed = lambda i, f: (0, 0)
    in_specs = [
        pl.BlockSpec((tm, D_MODEL), row),
        pl.BlockSpec((tm, D_MODEL), row),
        pl.BlockSpec((ms, D_MODEL), fixed),
        pl.BlockSpec((ms, D_MODEL), fixed),
        pl.BlockSpec((D_MODEL, tf), lambda i, f: (0, f)),
        pl.BlockSpec((D_MODEL, tf), lambda i, f: (0, nf + f)),
        pl.BlockSpec((tf, D_MODEL), lambda i, f: (f, 0)),
        pl.BlockSpec((1, D_MODEL), fixed),
        pl.BlockSpec((1, D_MODEL), fixed),
    ]
    out_specs = [pl.BlockSpec((tm, D_MODEL), row), pl.BlockSpec((tm, D_MODEL), row),
                 pl.BlockSpec((ms, D_MODEL), fixed), pl.BlockSpec((ms, D_MODEL), fixed)]
    out_shape = [jax.ShapeDtypeStruct((m, D_MODEL), F32), jax.ShapeDtypeStruct((m, D_MODEL), BF16),
                 jax.ShapeDtypeStruct((ms, D_MODEL), F32), jax.ShapeDtypeStruct((ms, D_MODEL), BF16)]
    args = [xb, x, xsb, xs, w_in_b, w_in_b, w_out_b, g, b]
    if nxt is not None:
        nw_in, nw_out, nl, ns = nxt
        half = (m // tm) * nf // 2
        ci, co = 2 * D_FF // half, D_FF // half
        assert ci * half == 2 * D_FF and co * half == D_FF and ci % LANES == 0 and co % 16 == 0
        in_col = lambda i, f: (0, jnp.minimum(i * nf + f, half - 1))
        out_row = lambda i, f: (jnp.maximum(i * nf + f - half, 0), 0)
        in_specs += [pl.BlockSpec((None, None, D_MODEL, ci), lambda i, f: (nl, ns) + in_col(i, f)),
                     pl.BlockSpec((None, None, co, D_MODEL), lambda i, f: (nl, ns) + out_row(i, f))]
        out_specs += [pl.BlockSpec((D_MODEL, ci), in_col), pl.BlockSpec((co, D_MODEL), out_row)]
        out_shape += [jax.ShapeDtypeStruct((D_MODEL, 2 * D_FF), BF16), jax.ShapeDtypeStruct((D_FF, D_MODEL), BF16)]
        args += [nw_in, nw_out]
    return pl.pallas_call(
        functools.partial(_ffn_kernel, cast_next=nxt is not None),
        grid=(m // tm, nf),
        in_specs=in_specs,
        out_specs=out_specs,
        out_shape=out_shape,
        scratch_shapes=[pltpu.VMEM((ms, D_MODEL), F32)],
        compiler_params=_params("arbitrary", "arbitrary"),
        name="ffn",
    )(*args)


def _gmlp_in_kernel(xb_ref, w_ref, bias_ref, lg_ref, lb_ref, z_ref):
    n = pl.program_id(0)
    z = _dot(xb_ref[...], w_ref[...]) + bias_ref[...]
    z = 0.5 * z * (1.0 + lax.erf(z * (1.0 / math.sqrt(2.0))))

    @pl.when(n == 0)
    def _():
        z_ref[...] = z

    @pl.when(n == 1)
    def _():
        z_ref[...] = _layer_norm(z, lg_ref[...], lb_ref[...])


def _gmlp_in(xb, w, bias, lg, lb):
    m = xb.shape[0]
    tm = _row_tile(m)
    return pl.pallas_call(
        _gmlp_in_kernel,
        grid=(2, m // tm),
        in_specs=[
            pl.BlockSpec((tm, D_MODEL), lambda n, i: (i, 0)),
            pl.BlockSpec((D_MODEL, D_GATE), lambda n, i: (0, n)),
            pl.BlockSpec((1, D_GATE), lambda n, i: (0, n)),
            pl.BlockSpec((1, D_GATE), lambda n, i: (0, 0)),
            pl.BlockSpec((1, D_GATE), lambda n, i: (0, 0)),
        ],
        out_specs=pl.BlockSpec((tm, D_GATE), lambda n, i: (i, n)),
        out_shape=jax.ShapeDtypeStruct((m, 2 * D_GATE), F32),
        compiler_params=_params("arbitrary", "arbitrary"),
        name="gmlp_in",
    )(xb, w, bias, lg, lb)


def _gmlp_out_kernel(u_ref, v_ref, x_ref, ws_ref, bs_ref, wo_ref, g_ref, b_ref, y_ref, yb_ref, *, rows, causal):
    tm = u_ref.shape[0]
    r_i = lax.broadcasted_iota(jnp.int32, (rows, rows), 0)
    c_i = lax.broadcasted_iota(jnp.int32, (rows, rows), 1)
    keep = ((r_i // causal) == (c_i // causal)) & (c_i <= r_i)
    w_sp = [jnp.where(keep, ws_ref[g], 0.0).astype(BF16) for g in range(N_GROUPS_A)]
    bs = bs_ref[...]
    chunks = []
    for ch in range(tm // rows):
        u = u_ref[ch * rows:(ch + 1) * rows, :]
        v = v_ref[ch * rows:(ch + 1) * rows, :]
        parts = []
        for g in range(N_GROUPS_A):
            lo, hi = g * GROUP_DIM_A, (g + 1) * GROUP_DIM_A
            mixed = _dot(w_sp[g], v[:, lo:hi].astype(BF16)) + bs[:, g:g + 1]
            parts.append((u[:, lo:hi] * mixed).astype(BF16))
        chunks.append(jnp.concatenate(parts, axis=1))
    y = chunks[0] if len(chunks) == 1 else jnp.concatenate(chunks, axis=0)
    f = _dot(y, wo_ref[...])
    out = _layer_norm(ALPHA * x_ref[...] + f, g_ref[...], b_ref[...])
    y_ref[...] = out
    yb_ref[...] = out.astype(BF16)


def _gmlp_out(z, x, w_sp, b_sp, w_out, g, b, rows, causal):
    m = x.shape[0]
    tm = _row_tile(m)
    row = lambda i: (i, 0)
    kern = functools.partial(_gmlp_out_kernel, rows=rows, causal=causal)
    return pl.pallas_call(
        kern,
        grid=(m // tm,),
        in_specs=[
            pl.BlockSpec((tm, D_GATE), lambda i: (i, 0)),
            pl.BlockSpec((tm, D_GATE), lambda i: (i, 1)),
            pl.BlockSpec((tm, D_MODEL), row),
            pl.BlockSpec((N_GROUPS_A, rows, rows), lambda i: (0, 0, 0)),
            pl.BlockSpec((rows, N_GROUPS_A), lambda i: (0, 0)),
            pl.BlockSpec((D_GATE, D_MODEL), lambda i: (0, 0), pipeline_mode=pl.Buffered(1)),
            pl.BlockSpec((1, D_MODEL), lambda i: (0, 0)),
            pl.BlockSpec((1, D_MODEL), lambda i: (0, 0)),
        ],
        out_specs=[pl.BlockSpec((tm, D_MODEL), row), pl.BlockSpec((tm, D_MODEL), row)],
        out_shape=[jax.ShapeDtypeStruct((m, D_MODEL), F32), jax.ShapeDtypeStruct((m, D_MODEL), BF16)],
        compiler_params=_params("parallel"),
        name="gmlp_out",
    )(z, z, x, w_sp, b_sp, w_out, g, b)


def _proj_norm_kernel(a_ref, x_ref, w_ref, g_ref, b_ref, y_ref, yb_ref):
    out = _layer_norm(ALPHA * x_ref[...] + _dot(a_ref[...], w_ref[...]), g_ref[...], b_ref[...])
    y_ref[...] = out
    yb_ref[...] = out.astype(BF16)


def _proj_norm(a, x, w, g, b):
    m = x.shape[0]
    tm = _row_tile(m)
    row = lambda i: (i, 0)
    return pl.pallas_call(
        _proj_norm_kernel,
        grid=(m // tm,),
        in_specs=[
            pl.BlockSpec((tm, D_MODEL), row),
            pl.BlockSpec((tm, D_MODEL), row),
            pl.BlockSpec((D_MODEL, D_MODEL), lambda i: (0, 0), pipeline_mode=pl.Buffered(1)),
            pl.BlockSpec((1, D_MODEL), lambda i: (0, 0)),
            pl.BlockSpec((1, D_MODEL), lambda i: (0, 0)),
        ],
        out_specs=[pl.BlockSpec((tm, D_MODEL), row), pl.BlockSpec((tm, D_MODEL), row)],
        out_shape=[jax.ShapeDtypeStruct((m, D_MODEL), F32), jax.ShapeDtypeStruct((m, D_MODEL), BF16)],
        compiler_params=_params("parallel"),
        name="proj_norm",
    )(a, x, w, g, b)


def _kv_proj_kernel(xb_ref, w_ref, cos_ref, slo_ref, shi_ref, kvb_ref, cmp_hbm, slc_hbm, win_hbm, stage, sem):
    i, n = pl.program_id(0), pl.program_id(1)
    tm = xb_ref.shape[0]
    step = i * pl.num_programs(1) + n
    last = pl.num_programs(0) * pl.num_programs(1) - 1
    slot = step % 2
    outs = (cmp_hbm, slc_hbm, win_hbm)

    def copies(out_hbm, c, slot_):
        return [pltpu.make_async_copy(stage.at[slot_, :, pl.ds(g * HEAD_DIM, HEAD_DIM)],
                                      out_hbm.at[pl.ds(i * tm, tm), c * N_KV + g, :], sem.at[slot_])
                for g in range(N_KV)]

    @pl.when(step >= 2)
    def _():
        for cp in copies(cmp_hbm, 0, slot):
            cp.wait()

    acc = _dot(xb_ref[...], w_ref[...])
    for col in range(2 * N_BRANCH):
        @pl.when(n == col)
        def _(col=col):
            br, c = divmod(col, 2)
            if c == 0:
                cos_t, s_lo, s_hi = cos_ref[...], slo_ref[...], shi_ref[...]
                val = jnp.concatenate([_rotate(acc[:, h * HEAD_DIM:(h + 1) * HEAD_DIM], cos_t, s_lo, s_hi)
                                       for h in range(N_KV)], axis=1)
            else:
                val = acc
            stage[slot] = val
            kvb_ref[...] = val.astype(BF16)
            for cp in copies(outs[br], c, slot):
                cp.start()

    @pl.when(step == last)
    def _():
        for cp in copies(cmp_hbm, 0, 1 - slot) + copies(cmp_hbm, 0, slot):
            cp.wait()


def _kv_proj(xb, w, tables, pos_rows):
    m = xb.shape[0]
    tm = _row_tile(m, 1024)
    half = N_KV * HEAD_DIM
    ncol = N_BRANCH * 2
    nt = pos_rows // tm
    tab = lambda i, n: (i % nt, 0)
    rows = jax.ShapeDtypeStruct((m, 2 * N_KV, HEAD_DIM), F32)
    hbm = pl.BlockSpec(memory_space=pl.ANY)
    return pl.pallas_call(
        _kv_proj_kernel,
        grid=(m // tm, ncol),
        in_specs=[
            pl.BlockSpec((tm, D_MODEL), lambda i, n: (i, 0)),
            pl.BlockSpec((D_MODEL, half), lambda i, n: (0, n)),
            pl.BlockSpec((tm, LANES), tab),
            pl.BlockSpec((tm, LANES), tab),
            pl.BlockSpec((tm, LANES), tab),
        ],
        out_specs=[pl.BlockSpec((tm, half), lambda i, n: (i, n)), hbm, hbm, hbm],
        out_shape=[jax.ShapeDtypeStruct((m, ncol * half), BF16), rows, rows, rows],
        scratch_shapes=[pltpu.VMEM((2, tm, half), F32), pltpu.SemaphoreType.DMA((2,))],
        compiler_params=_params("arbitrary", "arbitrary"),
        name="kv_proj",
    )(xb, w, *tables)


def _qg_proj_kernel(xb_ref, wq_ref, wg_ref, cos_ref, slo_ref, shi_ref, q_ref, gt_ref):
    xb = xb_ref[...]
    acc = _dot(xb, wq_ref[...])
    cos_t, s_lo, s_hi = cos_ref[...], slo_ref[...], shi_ref[...]
    for h in range(N_HEADS):
        lo, hi = h * HEAD_DIM, (h + 1) * HEAD_DIM
        q_ref[:, lo:hi] = _rotate(acc[:, lo:hi], cos_t, s_lo, s_hi).astype(BF16)
    gt_ref[...] = _sigmoid(_dot(xb, wg_ref[...]))


def _qg_proj(xb, wq, wg, tables, pos_rows):
    m = xb.shape[0]
    tm = _row_tile(m)
    nt = pos_rows // tm
    tab = lambda i: (i % nt, 0)
    gcols = N_KV * LANES
    return pl.pallas_call(
        _qg_proj_kernel,
        grid=(m // tm,),
        in_specs=[
            pl.BlockSpec((tm, D_MODEL), lambda i: (i, 0)),
            pl.BlockSpec((D_MODEL, D_MODEL), lambda i: (0, 0), pipeline_mode=pl.Buffered(1)),
            pl.BlockSpec((D_MODEL, gcols), lambda i: (0, 0)),
            pl.BlockSpec((tm, LANES), tab),
            pl.BlockSpec((tm, LANES), tab),
            pl.BlockSpec((tm, LANES), tab),
        ],
        out_specs=[pl.BlockSpec((tm, D_MODEL), lambda i: (i, 0)), pl.BlockSpec((tm, gcols), lambda i: (i, 0))],
        out_shape=[jax.ShapeDtypeStruct((m, D_MODEL), BF16), jax.ShapeDtypeStruct((m, gcols), F32)],
        compiler_params=_params("parallel"),
        name="qg_proj",
    )(xb, wq, wg, *tables)


def _compress_pages_kernel(tbl_ref, *refs):
    del tbl_ref
    npg = PAGES_PER_STEP
    pages = refs[:npg]
    pos_ref, w_ref, out_ref = refs[npg:]
    per_page = PAGE_SIZE // CMP_BLOCK
    rows = npg * per_page * 2 * N_KV
    acc = jnp.zeros((rows, 2 * HEAD_DIM), F32)
    for j in range(CMP_BLOCK // 2):
        halves = []
        for l in (2 * j, 2 * j + 1):
            pos_l = pos_ref[l]
            tiles = [pages[p][l + CMP_BLOCK * h] + pos_l for p in range(npg) for h in range(per_page)]
            halves.append(jnp.concatenate(tiles, axis=0).astype(BF16))
        acc = acc + _dot(jnp.concatenate(halves, axis=1), w_ref[j])
    is_k = (lax.broadcasted_iota(jnp.int32, (rows, HEAD_DIM), 0) % (2 * N_KV)) < N_KV
    out_ref[...] = jnp.where(is_k, acc[:, :HEAD_DIM], acc[:, HEAD_DIM:])


def _compress_pages(cache, table, pos8, w_pair):
    npg = PAGES_PER_STEP
    rows = npg * (PAGE_SIZE // CMP_BLOCK) * 2 * N_KV

    def page_spec(p):
        return pl.BlockSpec((PAGE_SIZE, 2 * N_KV, HEAD_DIM), lambda i, tbl: (tbl[i * npg + p], 0, 0))

    grid_spec = pltpu.PrefetchScalarGridSpec(
        num_scalar_prefetch=1,
        grid=(table.shape[0] // npg,),
        in_specs=[page_spec(p) for p in range(npg)] + [
            pl.BlockSpec((CMP_BLOCK, 2 * N_KV, HEAD_DIM), lambda i, tbl: (0, 0, 0)),
            pl.BlockSpec((CMP_BLOCK // 2, 2 * HEAD_DIM, 2 * HEAD_DIM), lambda i, tbl: (0, 0, 0)),
        ],
        out_specs=pl.BlockSpec((rows, HEAD_DIM), lambda i, tbl: (i, 0)),
    )
    return pl.pallas_call(
        _compress_pages_kernel,
        grid_spec=grid_spec,
        out_shape=jax.ShapeDtypeStruct((table.shape[0] // npg * rows, HEAD_DIM), F32),
        compiler_params=_params("arbitrary"),
        name="compress_pages",
    )(table, *([cache] * npg), pos8, w_pair)


def _rank_select(score_t, n_blocks):
    idx = lax.broadcasted_iota(jnp.int32, score_t.shape, 0)
    rank = jnp.zeros(score_t.shape, F32)
    for i in range(n_blocks):
        row = score_t[i:i + 1, :]
        rank = rank + jnp.where(idx > i, jnp.where(row >= score_t, 1.0, 0.0), jnp.where(row > score_t, 1.0, 0.0))
    return jnp.where(rank < float(min(TOP_N, n_blocks)), 1.0, 0.0)


def _nsa_prompt_kernel(q_ref, gt_ref, ck_ref, cv_ref, sk_ref, sv_ref, wk_ref, wv_ref, e_ref, o_ref):
    tq = q_ref.shape[0]
    t_len = sk_ref.shape[0]
    nblk = ck_ref.shape[0]
    i = pl.program_id(2)
    scale = HEAD_DIM ** -0.5
    q = q_ref[...]
    q4 = jnp.concatenate([q[:, h * HEAD_DIM:(h + 1) * HEAD_DIM] for h in range(HPG)], axis=0)
    pos_c = i * tq + lax.broadcasted_iota(jnp.int32, (tq, 1), 0)
    pos_r = i * tq + lax.broadcasted_iota(jnp.int32, (1, tq), 1)
    pos4_r = jnp.concatenate([pos_r] * HPG, axis=1)

    c2 = scale * LOG2E

    ck = ck_ref[...].astype(BF16)
    cv = cv_ref[...].astype(BF16)
    blk_c = lax.broadcasted_iota(jnp.int32, (nblk, 1), 0)
    vis_t = (blk_c + 1) * CMP_BLOCK - 1 <= pos4_r
    s_t = jnp.where(vis_t, _dot_nt(ck, q4), NEG)
    p_t = jnp.exp2((s_t - jnp.max(s_t, axis=0, keepdims=True)) * c2)
    p_t = p_t * (1.0 / jnp.sum(p_t, axis=0, keepdims=True))
    p_t = p_t * jnp.where(pos4_r >= CMP_BLOCK - 1, 1.0, 0.0)
    o_cmp = _dot_tn(p_t.astype(BF16), cv)
    imp_t = p_t[:, 0:tq]
    for h in range(1, HPG):
        imp_t = imp_t + p_t[:, h * tq:(h + 1) * tq]
    cur = pos_r // SEL_BLOCK
    forced = (blk_c == 0) | (blk_c == cur) | (blk_c == cur - 1)
    score_t = jnp.where(forced, FORCED, jnp.where(blk_c <= cur, imp_t, -1.0))
    sel_t = _rank_select(score_t, nblk)
    sel_b = sel_t.astype(BF16)
    key_r = lax.broadcasted_iota(jnp.int32, (1, KEY_CHUNK), 1)

    def body(kc, carry):
        m, l, acc = carry
        start = pl.multiple_of(kc * KEY_CHUNK, KEY_CHUNK)
        k = sk_ref[pl.ds(start, KEY_CHUNK), :]
        v = sv_ref[pl.ds(start, KEY_CHUNK), :]
        member = _dot_tn(sel_b, e_ref[kc])
        bias = jnp.where(start + key_r <= pos_c, (member - 1.0) * (-NEG), NEG)
        s = _dot_nt(q4, k) + jnp.concatenate([bias] * HPG, axis=0)
        m_new = jnp.maximum(m, jnp.max(s, axis=-1, keepdims=True))
        a = jnp.exp2((m - m_new) * c2)
        pr = jnp.exp2((s - m_new) * c2)
        l = a * l + jnp.sum(pr, axis=-1, keepdims=True)
        acc = a * acc + _dot(pr.astype(BF16), v)
        return m_new, l, acc

    n_chunks = (i * tq + tq + KEY_CHUNK - 1) // KEY_CHUNK
    init = (jnp.full((HPG * tq, 1), NEG, F32), jnp.zeros((HPG * tq, 1), F32), jnp.zeros((HPG * tq, HEAD_DIM), F32))
    _, l_s, acc_s = lax.fori_loop(0, n_chunks, body, init)
    o_slc = acc_s * (1.0 / l_s)

    n_win = WINDOW + tq
    start = pl.multiple_of(jnp.maximum(i * tq - WINDOW, 0), tq)
    kw = wk_ref[pl.ds(start, n_win), :]
    vw = wv_ref[pl.ds(start, n_win), :]
    d = pos_c - (start + lax.broadcasted_iota(jnp.int32, (1, n_win), 1))
    bias_w = jnp.where((d >= 0) & (d < WINDOW), 0.0, NEG)
    sw = _dot_nt(q4, kw) + jnp.concatenate([bias_w] * HPG, axis=0)
    pw = jnp.exp2((sw - jnp.max(sw, axis=-1, keepdims=True)) * c2)
    o_win = _dot(pw.astype(BF16), vw) * (1.0 / jnp.sum(pw, axis=-1, keepdims=True))

    gt = gt_ref[...]
    for h in range(HPG):
        r0, r1 = h * tq, (h + 1) * tq
        c0 = h * N_BRANCH
        og = (gt[:, c0:c0 + 1] * o_cmp[r0:r1] + gt[:, c0 + 1:c0 + 2] * o_slc[r0:r1]
              + gt[:, c0 + 2:c0 + 3] * o_win[r0:r1])
        o_ref[:, h * HEAD_DIM:(h + 1) * HEAD_DIM] = og.astype(BF16)


def _nsa_prompt(q, gates, cmp_c, kvb, expand, batch, t_len):
    tq = Q_TILE
    nq = t_len // tq
    nblk = t_len // CMP_BLOCK
    gw = HPG * HEAD_DIM
    qrow = lambda b, g, i: (b * nq + i, g)

    def kv_spec(col0):
        return pl.BlockSpec((t_len, HEAD_DIM), lambda b, g, i: (b, col0 + g))

    return pl.pallas_call(
        _nsa_prompt_kernel,
        grid=(batch, N_KV, nq),
        in_specs=[
            pl.BlockSpec((tq, gw), qrow),
            pl.BlockSpec((tq, LANES), qrow),
            pl.BlockSpec((nblk, HEAD_DIM), lambda b, g, i: (b, g)),
            pl.BlockSpec((nblk, HEAD_DIM), lambda b, g, i: (b, N_KV + g)),
            kv_spec(2 * N_KV), kv_spec(3 * N_KV), kv_spec(4 * N_KV), kv_spec(5 * N_KV),
            pl.BlockSpec((t_len // KEY_CHUNK, nblk, KEY_CHUNK), lambda b, g, i: (0, 0, 0)),
        ],
        out_specs=pl.BlockSpec((tq, gw), qrow),
        out_shape=jax.ShapeDtypeStruct((batch * t_len, D_MODEL), BF16),
        compiler_params=_params("parallel", "parallel", "arbitrary"),
        name="nsa_prompt",
    )(q, gates, cmp_c, cmp_c, kvb, kvb, kvb, kvb, expand)


def _smp_cmp_kernel(q_ref, ck_ref, cv_ref, o_ref, imp_ref):
    nblk = ck_ref.shape[0]
    scale = HEAD_DIM ** -0.5
    q = q_ref[0, 0]
    rows = q.shape[0]
    tok = lax.broadcasted_iota(jnp.int32, (rows, 1), 0) % DEC_SEQ
    pos = PAST_LEN + tok
    blk = lax.broadcasted_iota(jnp.int32, (1, nblk), 1)
    vis = (blk + 1) * CMP_BLOCK - 1 <= pos
    p = _softmax_rows(jnp.where(vis, _dot_nt(q, ck_ref[...].astype(BF16)) * scale, NEG))
    p = p * jnp.where(pos >= CMP_BLOCK - 1, 1.0, 0.0)
    o_ref[0, 0] = _dot(p.astype(BF16), cv_ref[...].astype(BF16))
    imp = p[0:DEC_SEQ]
    for h in range(1, HPG):
        imp = imp + p[h * DEC_SEQ:(h + 1) * DEC_SEQ]
    imp_ref[0, 0] = imp


def _smp_cmp(qs, cmp_c, nblk):
    rows = HPG * DEC_SEQ
    return pl.pallas_call(
        _smp_cmp_kernel,
        grid=(DEC_BATCH, N_KV),
        in_specs=[
            pl.BlockSpec((1, 1, rows, HEAD_DIM), lambda b, g: (b, g, 0, 0)),
            pl.BlockSpec((nblk, HEAD_DIM), lambda b, g: (b, g)),
            pl.BlockSpec((nblk, HEAD_DIM), lambda b, g: (b, N_KV + g)),
        ],
        out_specs=[pl.BlockSpec((1, 1, rows, HEAD_DIM), lambda b, g: (b, g, 0, 0)),
                   pl.BlockSpec((1, 1, DEC_SEQ, nblk), lambda b, g: (b, g, 0, 0))],
        out_shape=[jax.ShapeDtypeStruct((DEC_BATCH, N_KV, rows, HEAD_DIM), F32),
                   jax.ShapeDtypeStruct((DEC_BATCH, N_KV, DEC_SEQ, nblk), F32)],
        compiler_params=_params("parallel", "parallel"),
        name="smp_cmp",
    )(qs, cmp_c, cmp_c)


def _smp_topk_kernel(imp_ref, idx_ref, *, n_sel_blocks):
    imp = imp_ref[...]
    rows, ncmp = imp.shape
    width = ncmp + LANES
    imp = jnp.concatenate([imp, jnp.zeros((rows, LANES), F32)], axis=1)
    j = lax.broadcasted_iota(jnp.int32, (rows, width), 1)
    tok = lax.broadcasted_iota(jnp.int32, (rows, 1), 0) % DEC_SEQ
    cur = (PAST_LEN + tok) // SEL_BLOCK
    forced = (j == 0) | (j == cur) | (j == cur - 1)
    score = jnp.where(forced, FORCED, jnp.where(j <= cur, imp, -1.0))
    score = jnp.where(j < n_sel_blocks, score, -jnp.inf)
    lane = lax.broadcasted_iota(jnp.int32, (rows, LANES), 1)
    out = jnp.zeros((rows, LANES), jnp.int32)
    jf = j.astype(F32)
    for n in range(min(TOP_N, n_sel_blocks)):
        m = jnp.max(score, axis=-1, keepdims=True)
        pick = jnp.min(jnp.where(score == m, jf, float(width)), axis=-1, keepdims=True)
        out = jnp.where(lane == n, pick.astype(jnp.int32), out)
        score = jnp.where(jf == pick, -jnp.inf, score)
    idx_ref[...] = out


def _smp_topk(imp, n_sel_blocks):
    rows = imp.shape[0]
    return pl.pallas_call(
        functools.partial(_smp_topk_kernel, n_sel_blocks=n_sel_blocks),
        out_shape=jax.ShapeDtypeStruct((rows, LANES), jnp.int32),
        name="smp_topk",
    )(imp)


def _smp_attn_kernel(idx_ref, pt_ref, q_ref, nsk_ref, nsv_ref, nwk_ref, nwv_ref, slc_hbm, win_hbm,
                     oslc_ref, owin_ref, kbuf, vbuf, wbuf, sem, *, n_pages, n_cache_blocks):
    n_sel = TOP_N
    per_page = PAGE_SIZE // SEL_BLOCK
    n_sel_keys = n_sel * SEL_BLOCK
    n_buf = wbuf.shape[2] - LANES
    s = pl.program_id(0)
    slot = s % 2

    def copies(step, slot_, for_wait):
        b, g = step // N_KV, step % N_KV
        out = []
        for t in range(DEC_SEQ):
            for n in range(n_sel):
                if for_wait:
                    row0 = 0
                else:
                    bid = jnp.minimum(idx_ref[(step * DEC_SEQ + t) * n_sel + n], n_cache_blocks - 1)
                    row0 = (pt_ref[b * n_pages + bid // per_page] * per_page + bid % per_page) * SEL_BLOCK
                dst = pl.ds(n * SEL_BLOCK, SEL_BLOCK)
                out.append(pltpu.make_async_copy(slc_hbm.at[pl.ds(row0, SEL_BLOCK), g, :],
                                                 kbuf.at[slot_, t, dst, :], sem.at[slot_]))
                out.append(pltpu.make_async_copy(slc_hbm.at[pl.ds(row0, SEL_BLOCK), N_KV + g, :],
                                                 vbuf.at[slot_, t, dst, :], sem.at[slot_]))
        for c in range(2):
            out.append(pltpu.make_async_copy(win_hbm.at[pl.ds(b * n_buf, n_buf), c * N_KV + g, :],
                                             wbuf.at[slot_, c, pl.ds(0, n_buf), :], sem.at[slot_]))
        return out

    @pl.when(s == 0)
    def _():
        for cp in copies(s, slot, False):
            cp.start()

    @pl.when(s + 1 < pl.num_programs(0))
    def _():
        for cp in copies(s + 1, 1 - slot, False):
            cp.start()

    for cp in copies(s, slot, True):
        cp.wait()

    scale = HEAD_DIM ** -0.5
    n_new = nsk_ref.shape[1]
    zeros = jnp.zeros((LANES, HEAD_DIM), F32)
    blk_lane = lax.broadcasted_iota(jnp.int32, (1, n_sel_keys + LANES), 1)
    in_cache = blk_lane < n_sel_keys
    off_new = blk_lane - n_sel_keys
    lane = lax.broadcasted_iota(jnp.int32, (1, n_buf + LANES), 1)
    kpos = jnp.where(lane < n_buf, PAST_LEN - n_buf + lane, PAST_LEN + lane - n_buf)
    real = (lane < n_buf) | (lane - n_buf < DEC_SEQ)

    wbuf[slot, 0, pl.ds(n_buf, LANES), :] = zeros
    wbuf[slot, 1, pl.ds(n_buf, LANES), :] = zeros
    wbuf[slot, 0, pl.ds(n_buf, n_new), :] = nwk_ref[0]
    wbuf[slot, 1, pl.ds(n_buf, n_new), :] = nwv_ref[0]
    kw = wbuf[slot, 0].astype(BF16)
    vw = wbuf[slot, 1].astype(BF16)

    for t in range(DEC_SEQ):
        pos = PAST_LEN + t
        q = q_ref[0, 0, t]
        ids = jnp.zeros((1, n_sel_keys + LANES), jnp.int32)
        has_new = jnp.zeros((1, 1), jnp.int32)
        for n in range(n_sel):
            bid = idx_ref[(s * DEC_SEQ + t) * n_sel + n]
            ids = jnp.where(blk_lane // SEL_BLOCK == n, bid, ids)
            has_new = jnp.maximum(has_new, jnp.where(bid == n_cache_blocks, 1, 0))
        kbuf[slot, t, pl.ds(n_sel_keys, LANES), :] = zeros
        vbuf[slot, t, pl.ds(n_sel_keys, LANES), :] = zeros
        kbuf[slot, t, pl.ds(n_sel_keys, n_new), :] = nsk_ref[0]
        vbuf[slot, t, pl.ds(n_sel_keys, n_new), :] = nsv_ref[0]
        tok = jnp.where(in_cache, ids * SEL_BLOCK + blk_lane % SEL_BLOCK, n_cache_blocks * SEL_BLOCK + off_new)
        src_ok = jnp.where(in_cache, jnp.where(ids != n_cache_blocks, 1, 0),
                           jnp.where(off_new < DEC_SEQ, 1, 0) * has_new)
        ok = (src_ok > 0) & (tok <= pos)
        ps = _softmax_rows(jnp.where(ok, _dot_nt(q, kbuf[slot, t].astype(BF16)) * scale, NEG))
        oslc_ref[0, 0, t] = _dot(ps.astype(BF16), vbuf[slot, t].astype(BF16))

        d = pos - kpos
        okw = real & (d >= 0) & (d < WINDOW)
        pw = _softmax_rows(jnp.where(okw, _dot_nt(q, kw) * scale, NEG))
        owin_ref[0, 0, t] = _dot(pw.astype(BF16), vw)


def _smp_attn(idx_flat, pt_flat, q8, cache_slc_rows, new_pad, cache_win_rows, n_pages, n_cache_blocks, n_buf):
    n_sel = TOP_N
    n_new = new_pad.shape[1]
    bg = lambda s, idx, pt: (s // N_KV, s % N_KV, 0, 0, 0)

    def new_spec(col0):
        return pl.BlockSpec((1, n_new, HEAD_DIM), lambda s, idx, pt: (s // N_KV, 0, col0 + s % N_KV))

    qo_spec = pl.BlockSpec((1, 1, DEC_SEQ, 8, HEAD_DIM), bg)
    grid_spec = pltpu.PrefetchScalarGridSpec(
        num_scalar_prefetch=2,
        grid=(DEC_BATCH * N_KV,),
        in_specs=[qo_spec, new_spec(2 * N_KV), new_spec(3 * N_KV), new_spec(4 * N_KV), new_spec(5 * N_KV),
                  pl.BlockSpec(memory_space=pl.ANY), pl.BlockSpec(memory_space=pl.ANY)],
        out_specs=[qo_spec, qo_spec],
        scratch_shapes=[pltpu.VMEM((2, DEC_SEQ, n_sel * SEL_BLOCK + LANES, HEAD_DIM), F32),
                        pltpu.VMEM((2, DEC_SEQ, n_sel * SEL_BLOCK + LANES, HEAD_DIM), F32),
                        pltpu.VMEM((2, 2, n_buf + LANES, HEAD_DIM), F32),
                        pltpu.SemaphoreType.DMA((2,))],
    )
    shape = jax.ShapeDtypeStruct((DEC_BATCH, N_KV, DEC_SEQ, 8, HEAD_DIM), F32)
    return pl.pallas_call(
        functools.partial(_smp_attn_kernel, n_pages=n_pages, n_cache_blocks=n_cache_blocks),
        grid_spec=grid_spec,
        out_shape=[shape, shape],
        compiler_params=_params("arbitrary"),
        name="smp_attn",
    )(idx_flat, pt_flat, q8, new_pad, new_pad, new_pad, new_pad, cache_slc_rows, cache_win_rows)


def _gate_combine_kernel(gt_ref, oc_ref, os_ref, ow_ref, o_ref):
    gt = gt_ref[...]
    for g in range(N_KV):
        for h in range(HPG):
            c0 = g * LANES + h * N_BRANCH
            lo = (g * HPG + h) * HEAD_DIM
            hi = lo + HEAD_DIM
            og = (gt[:, c0:c0 + 1] * oc_ref[:, lo:hi] + gt[:, c0 + 1:c0 + 2] * os_ref[:, lo:hi]
                  + gt[:, c0 + 2:c0 + 3] * ow_ref[:, lo:hi])
            o_ref[:, lo:hi] = og.astype(BF16)


def _gate_combine(gates, o_cmp, o_slc, o_win):
    return pl.pallas_call(
        _gate_combine_kernel,
        out_shape=jax.ShapeDtypeStruct(o_cmp.shape, BF16),
        name="gate_combine",
    )(gates, o_cmp, o_slc, o_win)


def _prep_weights(ln_g, ln_b, ffn_w_in, ffn_w_out, gmlp_w_in, gmlp_b_in, gmlp_ln_g, gmlp_ln_b, gmlp_w_s,
                  gmlp_b_s, gmlp_w_out, nsa_w_qg, nsa_w_o, w_kv, cmp_pos, w_cmp):
    nq = N_HEADS * HEAD_DIM
    wg = nsa_w_qg[:, :, nq:].reshape(N_B_LAYERS, D_MODEL, N_KV, GATE_COLS)
    wg = jnp.pad(wg, ((0, 0), (0, 0), (0, 0), (0, LANES - GATE_COLS))).reshape(N_B_LAYERS, D_MODEL, N_KV * LANES)
    pos_t = jnp.broadcast_to(cmp_pos[:, :, None, :], (CMP_BLOCK, 2, N_KV, HEAD_DIM)).reshape(CMP_BLOCK, KV_COLS)
    return dict(
        ln_g=ln_g.reshape(DEPTH, 3, 1, D_MODEL), ln_b=ln_b.reshape(DEPTH, 3, 1, D_MODEL),
        ffn_w_in=ffn_w_in, ffn_w_out=ffn_w_out,
        ffn_w_in0=ffn_w_in[0, 0].astype(BF16), ffn_w_out0=ffn_w_out[0, 0].astype(BF16),
        gmlp_w_in=gmlp_w_in.astype(BF16), gmlp_b_in=gmlp_b_in.reshape(N_A_LAYERS, 1, 2 * D_GATE),
        gmlp_ln_g=gmlp_ln_g.reshape(N_A_LAYERS, 1, D_GATE), gmlp_ln_b=gmlp_ln_b.reshape(N_A_LAYERS, 1, D_GATE),
        gmlp_w_s=gmlp_w_s, gmlp_b_s=gmlp_b_s, gmlp_w_out=gmlp_w_out.astype(BF16),
        w_q=nsa_w_qg[:, :, :nq].astype(BF16), w_g=wg.astype(BF16), w_o=nsa_w_o.astype(BF16),
        w_kv=w_kv.astype(BF16),
        pos8=pos_t.reshape(CMP_BLOCK, 2 * N_KV, HEAD_DIM),
        w_pair=jnp.concatenate([w_cmp[:, 0], w_cmp[:, 1]], axis=-1).astype(BF16).reshape(
            CMP_BLOCK // 2, 2 * HEAD_DIM, 2 * HEAD_DIM),
    )


def _trunks(xs, w, streams):
    xbs = [x.astype(BF16) for x in xs]
    ctxs = [None] * len(xs)
    v_rows = [[] for _ in xs]

    order = [(l, s) for l in range(DEPTH) for s in (0, 1)]
    ffn_w = [w["ffn_w_in0"], w["ffn_w_out0"]]

    def ffn(l, s):
        norm = 2 * s
        k = order.index((l, s))
        nxt = (w["ffn_w_in"], w["ffn_w_out"]) + order[k + 1] if k + 1 < len(order) else None
        out = _ffn(xs[0], xbs[0], xs[1], xbs[1], ffn_w[0], ffn_w[1], w["ln_g"][l, norm], w["ln_b"][l, norm], nxt)
        xs[0], xbs[0], xs[1], xbs[1] = out[:4]
        if nxt is not None:
            ffn_w[0], ffn_w[1] = out[4], out[5]

    for l in range(DEPTH):
        if l == N_A_LAYERS:
            for k, st in enumerate(streams):
                ctxs[k] = _kv_proj(xbs[k], w["w_kv"], st["tables"], st["pos_rows"])
        ffn(l, 0)
        for k, st in enumerate(streams):
            if l < N_A_LAYERS:
                rows, causal, w_sp, b_sp = st["spatial"](l)
                z = _gmlp_in(xbs[k], w["gmlp_w_in"][l], w["gmlp_b_in"][l], w["gmlp_ln_g"][l], w["gmlp_ln_b"][l])
                v_rows[k].append(z[:, D_GATE:])
                xs[k], xbs[k] = _gmlp_out(z, xs[k], w_sp, b_sp, w["gmlp_w_out"][l], w["ln_g"][l, 1],
                                          w["ln_b"][l, 1], rows, causal)
            else:
                o = st["attend"](l - N_A_LAYERS, xbs[k], ctxs[k])
                xs[k], xbs[k] = _proj_norm(o, xs[k], w["w_o"][l - N_A_LAYERS], w["ln_g"][l, 1], w["ln_b"][l, 1])
        ffn(l, 1)
    return [(xs[k], ctxs[k], v_rows[k]) for k in range(len(xs))]


def kernel(x_prompt, x_sample, cache_cmp_kv, cache_slc_kv, cache_win_kv, page_table, ln_g, ln_b, ffn_w_in,
           ffn_w_out, gmlp_w_in, gmlp_b_in, gmlp_ln_g, gmlp_ln_b, gmlp_w_s, gmlp_b_s, gmlp_w_out, nsa_w_qg,
           nsa_w_o, w_kv, cmp_pos, w_cmp):
    w = _prep_weights(ln_g, ln_b, ffn_w_in, ffn_w_out, gmlp_w_in, gmlp_b_in, gmlp_ln_g, gmlp_ln_b, gmlp_w_s,
                      gmlp_b_s, gmlp_w_out, nsa_w_qg, nsa_w_o, w_kv, cmp_pos, w_cmp)
    kv_shape = (2, N_KV, HEAD_DIM)

    mp = BATCH * SEQ
    tables_p = _rotary_tables(np.arange(SEQ))
    n_blk_p = SEQ // CMP_BLOCK
    key_blk = (np.arange(SEQ) // SEL_BLOCK).reshape(SEQ // KEY_CHUNK, 1, KEY_CHUNK)
    expand = jnp.asarray(key_blk == np.arange(n_blk_p)[None, :, None], BF16)
    cmp_p = {}

    def spatial_p(l):
        return CHUNK, CHUNK, w["gmlp_w_s"][l], w["gmlp_b_s"][l].T

    def attend_p(bl, xb, ctx):
        kvb, cmp_rows = ctx[0], ctx[1]
        if "c" not in cmp_p:
            slabs = jnp.arange(mp // PAGE_SIZE, dtype=jnp.int32)
            cmp_p["c"] = _compress_pages(cmp_rows, slabs, w["pos8"], w["w_pair"]).reshape(-1, KV_COLS)
        q, gates = _qg_proj(xb, w["w_q"][bl], w["w_g"][bl], tables_p, SEQ)
        return _nsa_prompt(q, gates, cmp_p["c"], kvb, expand, BATCH, SEQ)

    ms = DEC_BATCH * DEC_SEQ
    pos_s = PAST_LEN + np.arange(DEC_SEQ)
    tables_s = _rotary_tables(np.tile(pos_s, DEC_BATCH))
    n_pages = PAST_LEN // PAGE_SIZE
    n_cache_blocks = PAST_LEN // SEL_BLOCK
    n_sel_blocks = -(-(PAST_LEN + DEC_SEQ) // SEL_BLOCK)
    n_cmp_s = (PAST_LEN + DEC_SEQ) // CMP_BLOCK
    pt_flat = page_table.reshape(-1)
    cache_cmp_rows = cache_cmp_kv.reshape(-1, 2 * N_KV, HEAD_DIM)
    cache_slc_rows = cache_slc_kv.reshape(-1, 2 * N_KV, HEAD_DIM)
    cache_win_rows = cache_win_kv.reshape(-1, 2 * N_KV, HEAD_DIM)
    n_buf = cache_win_kv.shape[1]
    cmp_s = {}

    def spatial_s(l):
        c = min(DEC_SEQ, CHUNK)
        w_t = jnp.tile(w["gmlp_w_s"][l][:, :c, :c], (1, DEC_BATCH, DEC_BATCH))
        b_t = jnp.tile(w["gmlp_b_s"][l][:, :c].T, (DEC_BATCH, 1))
        return ms, c, w_t, b_t

    def attend_s(bl, xb, ctx):
        if "c" not in cmp_s:
            kv = jnp.concatenate([r.reshape(ms, KV_COLS) for r in ctx[1:]], axis=1)
            cmp_s["c"] = _compress_pages(cache_cmp_rows, pt_flat, w["pos8"], w["w_pair"]).reshape(-1, KV_COLS)
            new = kv.reshape(DEC_BATCH, DEC_SEQ, N_BRANCH * KV_COLS)
            cmp_s["new"] = jnp.pad(new, ((0, 0), (0, 8 - DEC_SEQ), (0, 0)))
        q, gates = _qg_proj(xb, w["w_q"][bl], w["w_g"][bl], tables_s, ms)
        q5 = q.reshape(DEC_BATCH, DEC_SEQ, N_KV, HPG, HEAD_DIM)
        qs = q5.transpose(0, 2, 3, 1, 4).reshape(DEC_BATCH, N_KV, HPG * DEC_SEQ, HEAD_DIM)
        o_cmp, imp = _smp_cmp(qs, cmp_s["c"], n_cmp_s)
        idx = _smp_topk(imp.reshape(DEC_BATCH * N_KV * DEC_SEQ, n_cmp_s), n_sel_blocks)[:, :TOP_N]
        q8 = jnp.pad(q5.transpose(0, 2, 1, 3, 4), ((0, 0), (0, 0), (0, 0), (0, 8 - HPG), (0, 0)))
        o_slc, o_win = _smp_attn(idx.reshape(-1), pt_flat, q8, cache_slc_rows, cmp_s["new"], cache_win_rows,
                                 n_pages, n_cache_blocks, n_buf)
        o_cmp = o_cmp.reshape(DEC_BATCH, N_KV, HPG, DEC_SEQ, HEAD_DIM).transpose(0, 3, 1, 2, 4).reshape(ms, D_MODEL)
        o_slc = o_slc[:, :, :, :HPG].transpose(0, 2, 1, 3, 4).reshape(ms, D_MODEL)
        o_win = o_win[:, :, :, :HPG].transpose(0, 2, 1, 3, 4).reshape(ms, D_MODEL)
        return _gate_combine(gates, o_cmp, o_slc, o_win)

    streams = [dict(tables=tables_p, pos_rows=SEQ, spatial=spatial_p, attend=attend_p),
               dict(tables=tables_s, pos_rows=ms, spatial=spatial_s, attend=attend_s)]
    (y_p, ctx_p, _), (y_s, ctx_s, v_s) = _trunks(
        [x_prompt.reshape(mp, D_MODEL), x_sample.reshape(ms, D_MODEL)], w, streams)
    p_cmp, p_slc, p_win = (r.reshape(BATCH, SEQ, *kv_shape) for r in ctx_p[1:])
    p_win = p_win[:, -min(WINDOW, SEQ):]
    s_cmp, s_slc, s_new = (r.reshape(DEC_BATCH, DEC_SEQ, *kv_shape) for r in ctx_s[1:])
    s_win = jnp.concatenate([cache_win_kv, s_new], axis=1)[:, -min(WINDOW, PAST_LEN + DEC_SEQ):]
    s_v = jnp.stack([v.reshape(DEC_BATCH, DEC_SEQ, D_GATE) for v in v_s])

    return (y_p.reshape(BATCH, SEQ, D_MODEL), y_s.reshape(DEC_BATCH, DEC_SEQ, D_MODEL), p_cmp, p_slc, p_win,
            s_cmp, s_slc, s_win, s_v)
```

```python
import functools
import math

import numpy as np
import jax
import jax.numpy as jnp
from jax import lax
from jax.experimental import pallas as pl
from jax.experimental.pallas import tpu as pltpu

D_MODEL = 2048
BATCH = 4
SEQ = 2048
DEPTH = 4
DEC_BATCH = 8
DEC_SEQ = 4
PAST_LEN = 16384
PAGE_SIZE = 128

N_A_LAYERS = DEPTH // 2
N_B_LAYERS = DEPTH - N_A_LAYERS
D_FF = 5632
D_GATE = D_MODEL
CHUNK = 128
N_GROUPS_A = 8
GROUP_DIM_A = D_GATE // N_GROUPS_A
HEAD_DIM = 128
N_HEADS = D_MODEL // HEAD_DIM
N_KV = 4
HPG = N_HEADS // N_KV
ROT_DIM = HEAD_DIM // 4
ROPE_THETA = 500000.0
CMP_BLOCK = 64
SEL_BLOCK = 64
TOP_N = 16
WINDOW = 512
N_BRANCH = 3
ALPHA = (2 * DEPTH) ** 0.25
LN_EPS = 1e-5
NEG = -1e30
FORCED = 1e4

F32 = jnp.float32
BF16 = jnp.bfloat16

V7X_VMEM_LIMIT_BYTES = 56 * 1024 * 1024
LOG2E = math.log2(math.e)
LANES = 128
KV_COLS = 2 * N_KV * HEAD_DIM
GATE_COLS = HPG * N_BRANCH
KEY_CHUNK = 512
PAGES_PER_STEP = 16
Q_TILE = 256
WIN_Q = 128


def _params(*sem):
    return pltpu.CompilerParams(dimension_semantics=sem, vmem_limit_bytes=V7X_VMEM_LIMIT_BYTES)


def _row_tile(m, cap=512):
    return m if m <= cap else cap


def _dot(a, b):
    return jnp.dot(a, b, preferred_element_type=F32)


def _dot_nt(a, b):
    return lax.dot_general(a, b, (((1,), (1,)), ((), ())), preferred_element_type=F32)


def _dot_tn(a, b):
    return lax.dot_general(a, b, (((0,), (0,)), ((), ())), preferred_element_type=F32)


def _layer_norm(x, g, b):
    mu = jnp.mean(x, axis=-1, keepdims=True)
    d = x - mu
    var = jnp.mean(d * d, axis=-1, keepdims=True)
    return d * lax.rsqrt(var + LN_EPS) * g + b


def _sigmoid(x):
    return 1.0 / (1.0 + jnp.exp(-x))


def _softmax_rows(s):
    m = jnp.max(s, axis=-1, keepdims=True)
    p = jnp.exp(s - m)
    return p / jnp.sum(p, axis=-1, keepdims=True)


def _rotate(x, cos_t, sin_lo, sin_hi):
    half = ROT_DIM // 2
    return (x * cos_t + pltpu.roll(x, LANES - half, 1) * sin_lo + pltpu.roll(x, half, 1) * sin_hi)


def _rotary_tables(pos):
    half = ROT_DIM // 2
    inv = ROPE_THETA ** (-jnp.arange(half, dtype=F32) / half)
    ang = jnp.asarray(pos).astype(F32)[:, None] * inv[None, :]
    cos, sin = jnp.cos(ang), jnp.sin(ang)
    rows = ang.shape[0]
    pad = jnp.zeros((rows, LANES - ROT_DIM), F32)
    zero = jnp.zeros((rows, half), F32)
    cos_t = jnp.concatenate([cos, cos, pad + 1.0], axis=1)
    sin_lo = jnp.concatenate([-sin, zero, pad], axis=1)
    sin_hi = jnp.concatenate([zero, sin, pad], axis=1)
    return cos_t, sin_lo, sin_hi


def _ffn_kernel(xb_ref, x_ref, xsb_ref, xs_ref, wg_ref, wu_ref, wo_ref, g_ref, b_ref, *refs, cast_next):
    i, f = pl.program_id(0), pl.program_id(1)
    tm = xb_ref.shape[0]
    last = pl.num_programs(1) - 1
    if cast_next:
        nwi_ref, nwo_ref, y_ref, yb_ref, ys_ref, ysb_ref, nwib_ref, nwob_ref, accs_ref = refs
    else:
        y_ref, yb_ref, ys_ref, ysb_ref, accs_ref = refs

    def swiglu(lhs):
        if cast_next:
            nwib_ref[...] = nwi_ref[...].astype(BF16)
            nwob_ref[...] = nwo_ref[...].astype(BF16)
        gate = _dot(lhs, wg_ref[...])
        up = _dot(lhs, wu_ref[...])
        return _dot((gate * _sigmoid(gate) * up).astype(BF16), wo_ref[...])

    @pl.when(f == 0)
    def _():
        y_ref[...] = jnp.zeros_like(y_ref)

    @pl.when(i == 0)
    def _():
        @pl.when(f == 0)
        def _():
            accs_ref[...] = jnp.zeros_like(accs_ref)

        part = swiglu(jnp.concatenate([xb_ref[...], xsb_ref[...]], axis=0))
        y_ref[...] += part[:tm]
        accs_ref[...] += part[tm:]

        @pl.when(f == last)
        def _():
            ys = _layer_norm(ALPHA * xs_ref[...] + 0.5 * accs_ref[...], g_ref[...], b_ref[...])
            ys_ref[...] = ys
            ysb_ref[...] = ys.astype(BF16)

    @pl.when(i > 0)
    def _():
        y_ref[...] += swiglu(xb_ref[...])

    @pl.when(f == last)
    def _():
        y = _layer_norm(ALPHA * x_ref[...] + 0.5 * y_ref[...], g_ref[...], b_ref[...])
        y_ref[...] = y
        yb_ref[...] = y.astype(BF16)


def _ffn(x, xb, xs, xsb, w_in_b, w_out_b, g, b, nxt=None):
    m, ms = x.shape[0], xs.shape[0]
    tm = _row_tile(m)
    tf = 512
    nf = D_FF // tf
    row = lambda i, f: (i, 0)
    fixed = lambda i, f: (0, 0)
    in_specs = [
        pl.BlockSpec((tm, D_MODEL), row),
        pl.BlockSpec((tm, D_MODEL), row),
        pl.BlockSpec((ms, D_MODEL), fixed),
        pl.BlockSpec((ms, D_MODEL), fixed),
        pl.BlockSpec((D_MODEL, tf), lambda i, f: (0, f)),
        pl.BlockSpec((D_MODEL, tf), lambda i, f: (0, nf + f)),
        pl.BlockSpec((tf, D_MODEL), lambda i, f: (f, 0)),
        pl.BlockSpec((1, D_MODEL), fixed),
        pl.BlockSpec((1, D_MODEL), fixed),
    ]
    out_specs = [pl.BlockSpec((tm, D_MODEL), row), pl.BlockSpec((tm, D_MODEL), row),
                 pl.BlockSpec((ms, D_MODEL), fixed), pl.BlockSpec((ms, D_MODEL), fixed)]
    out_shape = [jax.ShapeDtypeStruct((m, D_MODEL), F32), jax.ShapeDtypeStruct((m, D_MODEL), BF16),
                 jax.ShapeDtypeStruct((ms, D_MODEL), F32), jax.ShapeDtypeStruct((ms, D_MODEL), BF16)]
    args = [xb, x, xsb, xs, w_in_b, w_in_b, w_out_b, g, b]
    if nxt is not None:
        nw_in, nw_out, nl, ns = nxt
        steps = (m // tm) * nf
        ri = D_MODEL // 2
        ci, co = 2 * (2 * D_FF) // steps, D_FF // steps
        assert 2 * (2 * D_FF // ci) == steps and co * steps == D_FF and ci % LANES == 0 and co % 16 == 0
        in_slab = lambda i, f: ((i * nf + f) % 2, (i * nf + f) // 2)
        out_slab = lambda i, f: (i * nf + f, 0)
        in_specs += [pl.BlockSpec((None, None, ri, ci), lambda i, f: (nl, ns) + in_slab(i, f)),
                     pl.BlockSpec((None, None, co, D_MODEL), lambda i, f: (nl, ns) + out_slab(i, f))]
        out_specs += [pl.BlockSpec((ri, ci), in_slab), pl.BlockSpec((co, D_MODEL), out_slab)]
        out_shape += [jax.ShapeDtypeStruct((D_MODEL, 2 * D_FF), BF16), jax.ShapeDtypeStruct((D_FF, D_MODEL), BF16)]
        args += [nw_in, nw_out]
    return pl.pallas_call(
        functools.partial(_ffn_kernel, cast_next=nxt is not None),
        grid=(m // tm, nf),
        in_specs=in_specs,
        out_specs=out_specs,
        out_shape=out_shape,
        scratch_shapes=[pltpu.VMEM((ms, D_MODEL), F32)],
        compiler_params=_params("arbitrary", "arbitrary"),
        name="ffn",
    )(*args)


def _gmlp_in_kernel(xb_ref, w_ref, bias_ref, lg_ref, lb_ref, z_ref):
    n = pl.program_id(0)
    z = _dot(xb_ref[...], w_ref[...]) + bias_ref[...]
    z = 0.5 * z * (1.0 + lax.erf(z * (1.0 / math.sqrt(2.0))))

    @pl.when(n == 0)
    def _():
        z_ref[...] = z

    @pl.when(n == 1)
    def _():
        z_ref[...] = _layer_norm(z, lg_ref[...], lb_ref[...])


def _gmlp_in(xb, w, bias, lg, lb):
    m = xb.shape[0]
    tm = _row_tile(m)
    return pl.pallas_call(
        _gmlp_in_kernel,
        grid=(2, m // tm),
        in_specs=[
            pl.BlockSpec((tm, D_MODEL), lambda n, i: (i, 0)),
            pl.BlockSpec((D_MODEL, D_GATE), lambda n, i: (0, n)),
            pl.BlockSpec((1, D_GATE), lambda n, i: (0, n)),
            pl.BlockSpec((1, D_GATE), lambda n, i: (0, 0)),
            pl.BlockSpec((1, D_GATE), lambda n, i: (0, 0)),
        ],
        out_specs=pl.BlockSpec((tm, D_GATE), lambda n, i: (i, n)),
        out_shape=jax.ShapeDtypeStruct((m, 2 * D_GATE), F32),
        compiler_params=_params("arbitrary", "arbitrary"),
        name="gmlp_in",
    )(xb, w, bias, lg, lb)


def _gmlp_out_kernel(u_ref, v_ref, x_ref, ws_ref, bs_ref, wo_ref, g_ref, b_ref, y_ref, yb_ref, *, rows, causal):
    tm = u_ref.shape[0]
    r_i = lax.broadcasted_iota(jnp.int32, (rows, rows), 0)
    c_i = lax.broadcasted_iota(jnp.int32, (rows, rows), 1)
    keep = ((r_i // causal) == (c_i // causal)) & (c_i <= r_i)
    w_sp = [jnp.where(keep, ws_ref[g], 0.0).astype(BF16) for g in range(N_GROUPS_A)]
    bs = bs_ref[...]
    chunks = []
    for ch in range(tm // rows):
        u = u_ref[ch * rows:(ch + 1) * rows, :]
        v = v_ref[ch * rows:(ch + 1) * rows, :]
        parts = []
        for g in range(N_GROUPS_A):
            lo, hi = g * GROUP_DIM_A, (g + 1) * GROUP_DIM_A
            mixed = _dot(w_sp[g], v[:, lo:hi].astype(BF16)) + bs[:, g:g + 1]
            parts.append((u[:, lo:hi] * mixed).astype(BF16))
        chunks.append(jnp.concatenate(parts, axis=1))
    y = chunks[0] if len(chunks) == 1 else jnp.concatenate(chunks, axis=0)
    f = _dot(y, wo_ref[...])
    out = _layer_norm(ALPHA * x_ref[...] + f, g_ref[...], b_ref[...])
    y_ref[...] = out
    yb_ref[...] = out.astype(BF16)


def _gmlp_out(z, x, w_sp, b_sp, w_out, g, b, rows, causal):
    m = x.shape[0]
    tm = _row_tile(m)
    row = lambda i: (i, 0)
    kern = functools.partial(_gmlp_out_kernel, rows=rows, causal=causal)
    return pl.pallas_call(
        kern,
        grid=(m // tm,),
        in_specs=[
            pl.BlockSpec((tm, D_GATE), lambda i: (i, 0)),
            pl.BlockSpec((tm, D_GATE), lambda i: (i, 1)),
            pl.BlockSpec((tm, D_MODEL), row),
            pl.BlockSpec((N_GROUPS_A, rows, rows), lambda i: (0, 0, 0)),
            pl.BlockSpec((rows, N_GROUPS_A), lambda i: (0, 0)),
            pl.BlockSpec((D_GATE, D_MODEL), lambda i: (0, 0), pipeline_mode=pl.Buffered(1)),
            pl.BlockSpec((1, D_MODEL), lambda i: (0, 0)),
            pl.BlockSpec((1, D_MODEL), lambda i: (0, 0)),
        ],
        out_specs=[pl.BlockSpec((tm, D_MODEL), row), pl.BlockSpec((tm, D_MODEL), row)],
        out_shape=[jax.ShapeDtypeStruct((m, D_MODEL), F32), jax.ShapeDtypeStruct((m, D_MODEL), BF16)],
        compiler_params=_params("parallel"),
        name="gmlp_out",
    )(z, z, x, w_sp, b_sp, w_out, g, b)


def _proj_norm_kernel(a_ref, x_ref, w_ref, g_ref, b_ref, y_ref, yb_ref):
    out = _layer_norm(ALPHA * x_ref[...] + _dot(a_ref[...], w_ref[...]), g_ref[...], b_ref[...])
    y_ref[...] = out
    yb_ref[...] = out.astype(BF16)


def _proj_norm(a, x, w, g, b):
    m = x.shape[0]
    tm = _row_tile(m)
    row = lambda i: (i, 0)
    return pl.pallas_call(
        _proj_norm_kernel,
        grid=(m // tm,),
        in_specs=[
            pl.BlockSpec((tm, D_MODEL), row),
            pl.BlockSpec((tm, D_MODEL), row),
            pl.BlockSpec((D_MODEL, D_MODEL), lambda i: (0, 0), pipeline_mode=pl.Buffered(1)),
            pl.BlockSpec((1, D_MODEL), lambda i: (0, 0)),
            pl.BlockSpec((1, D_MODEL), lambda i: (0, 0)),
        ],
        out_specs=[pl.BlockSpec((tm, D_MODEL), row), pl.BlockSpec((tm, D_MODEL), row)],
        out_shape=[jax.ShapeDtypeStruct((m, D_MODEL), F32), jax.ShapeDtypeStruct((m, D_MODEL), BF16)],
        compiler_params=_params("parallel"),
        name="proj_norm",
    )(a, x, w, g, b)


def _kv_proj_kernel(xb_ref, w_ref, cos_ref, slo_ref, shi_ref, kvb_ref, cmp_hbm, slc_hbm, win_hbm, stage, sem):
    i, n = pl.program_id(0), pl.program_id(1)
    tm = xb_ref.shape[0]
    step = i * pl.num_programs(1) + n
    last = pl.num_programs(0) * pl.num_programs(1) - 1
    slot = step % 2
    outs = (cmp_hbm, slc_hbm, win_hbm)

    def copies(out_hbm, c, slot_):
        return [pltpu.make_async_copy(stage.at[slot_, :, pl.ds(g * HEAD_DIM, HEAD_DIM)],
                                      out_hbm.at[pl.ds(i * tm, tm), c * N_KV + g, :], sem.at[slot_])
                for g in range(N_KV)]

    @pl.when(step >= 2)
    def _():
        for cp in copies(cmp_hbm, 0, slot):
            cp.wait()

    acc = _dot(xb_ref[...], w_ref[...])
    for col in range(2 * N_BRANCH):
        @pl.when(n == col)
        def _(col=col):
            br, c = divmod(col, 2)
            if c == 0:
                cos_t, s_lo, s_hi = cos_ref[...], slo_ref[...], shi_ref[...]
                val = jnp.concatenate([_rotate(acc[:, h * HEAD_DIM:(h + 1) * HEAD_DIM], cos_t, s_lo, s_hi)
                                       for h in range(N_KV)], axis=1)
            else:
                val = acc
            stage[slot] = val
            kvb_ref[...] = val.astype(BF16)
            for cp in copies(outs[br], c, slot):
                cp.start()

    @pl.when(step == last)
    def _():
        for cp in copies(cmp_hbm, 0, 1 - slot) + copies(cmp_hbm, 0, slot):
            cp.wait()


def _kv_proj(xb, w, tables, pos_rows):
    m = xb.shape[0]
    tm = _row_tile(m, 1024)
    half = N_KV * HEAD_DIM
    ncol = N_BRANCH * 2
    nt = pos_rows // tm
    tab = lambda i, n: (i % nt, 0)
    rows = jax.ShapeDtypeStruct((m, 2 * N_KV, HEAD_DIM), F32)
    hbm = pl.BlockSpec(memory_space=pl.ANY)
    return pl.pallas_call(
        _kv_proj_kernel,
        grid=(m // tm, ncol),
        in_specs=[
            pl.BlockSpec((tm, D_MODEL), lambda i, n: (i, 0)),
            pl.BlockSpec((D_MODEL, half), lambda i, n: (0, n)),
            pl.BlockSpec((tm, LANES), tab),
            pl.BlockSpec((tm, LANES), tab),
            pl.BlockSpec((tm, LANES), tab),
        ],
        out_specs=[pl.BlockSpec((tm, half), lambda i, n: (i, n)), hbm, hbm, hbm],
        out_shape=[jax.ShapeDtypeStruct((m, ncol * half), BF16), rows, rows, rows],
        scratch_shapes=[pltpu.VMEM((2, tm, half), F32), pltpu.SemaphoreType.DMA((2,))],
        compiler_params=_params("arbitrary", "arbitrary"),
        name="kv_proj",
    )(xb, w, *tables)


def _qg_proj_kernel(xb_ref, wq_ref, wg_ref, cos_ref, slo_ref, shi_ref, q_ref, gt_ref):
    xb = xb_ref[...]
    acc = _dot(xb, wq_ref[...])
    cos_t, s_lo, s_hi = cos_ref[...], slo_ref[...], shi_ref[...]
    for h in range(N_HEADS):
        lo, hi = h * HEAD_DIM, (h + 1) * HEAD_DIM
        q_ref[:, lo:hi] = _rotate(acc[:, lo:hi], cos_t, s_lo, s_hi).astype(BF16)
    gt_ref[...] = _sigmoid(_dot(xb, wg_ref[...]))


def _qg_proj(xb, wq, wg, tables, pos_rows):
    m = xb.shape[0]
    tm = _row_tile(m)
    nt = pos_rows // tm
    tab = lambda i: (i % nt, 0)
    gcols = N_KV * LANES
    return pl.pallas_call(
        _qg_proj_kernel,
        grid=(m // tm,),
        in_specs=[
            pl.BlockSpec((tm, D_MODEL), lambda i: (i, 0)),
            pl.BlockSpec((D_MODEL, D_MODEL), lambda i: (0, 0), pipeline_mode=pl.Buffered(1)),
            pl.BlockSpec((D_MODEL, gcols), lambda i: (0, 0)),
            pl.BlockSpec((tm, LANES), tab),
            pl.BlockSpec((tm, LANES), tab),
            pl.BlockSpec((tm, LANES), tab),
        ],
        out_specs=[pl.BlockSpec((tm, D_MODEL), lambda i: (i, 0)), pl.BlockSpec((tm, gcols), lambda i: (i, 0))],
        out_shape=[jax.ShapeDtypeStruct((m, D_MODEL), BF16), jax.ShapeDtypeStruct((m, gcols), F32)],
        compiler_params=_params("parallel"),
        name="qg_proj",
    )(xb, wq, wg, *tables)


def _compress_pages_kernel(tbl_ref, *refs):
    del tbl_ref
    npg = PAGES_PER_STEP
    pages = refs[:npg]
    pos_ref, w_ref, out_ref = refs[npg:]
    per_page = PAGE_SIZE // CMP_BLOCK
    rows = npg * per_page * 2 * N_KV
    acc = jnp.zeros((rows, 2 * HEAD_DIM), F32)
    for j in range(CMP_BLOCK // 2):
        halves = []
        for l in (2 * j, 2 * j + 1):
            pos_l = pos_ref[l]
            tiles = [pages[p][l + CMP_BLOCK * h] + pos_l for p in range(npg) for h in range(per_page)]
            halves.append(jnp.concatenate(tiles, axis=0).astype(BF16))
        acc = acc + _dot(jnp.concatenate(halves, axis=1), w_ref[j])
    is_k = (lax.broadcasted_iota(jnp.int32, (rows, HEAD_DIM), 0) % (2 * N_KV)) < N_KV
    out_ref[...] = jnp.where(is_k, acc[:, :HEAD_DIM], acc[:, HEAD_DIM:])


def _compress_pages(cache, table, pos8, w_pair):
    npg = PAGES_PER_STEP
    rows = npg * (PAGE_SIZE // CMP_BLOCK) * 2 * N_KV

    def page_spec(p):
        return pl.BlockSpec((PAGE_SIZE, 2 * N_KV, HEAD_DIM), lambda i, tbl: (tbl[i * npg + p], 0, 0))

    grid_spec = pltpu.PrefetchScalarGridSpec(
        num_scalar_prefetch=1,
        grid=(table.shape[0] // npg,),
        in_specs=[page_spec(p) for p in range(npg)] + [
            pl.BlockSpec((CMP_BLOCK, 2 * N_KV, HEAD_DIM), lambda i, tbl: (0, 0, 0)),
            pl.BlockSpec((CMP_BLOCK // 2, 2 * HEAD_DIM, 2 * HEAD_DIM), lambda i, tbl: (0, 0, 0)),
        ],
        out_specs=pl.BlockSpec((rows, HEAD_DIM), lambda i, tbl: (i, 0)),
    )
    return pl.pallas_call(
        _compress_pages_kernel,
        grid_spec=grid_spec,
        out_shape=jax.ShapeDtypeStruct((table.shape[0] // npg * rows, HEAD_DIM), F32),
        compiler_params=_params("arbitrary"),
        name="compress_pages",
    )(table, *([cache] * npg), pos8, w_pair)


def _rank_select(score_t, n_blocks):
    idx = lax.broadcasted_iota(jnp.int32, score_t.shape, 0)
    rank = jnp.zeros(score_t.shape, F32)
    for i in range(n_blocks):
        row = score_t[i:i + 1, :]
        rank = rank + jnp.where(idx > i, jnp.where(row >= score_t, 1.0, 0.0), jnp.where(row > score_t, 1.0, 0.0))
    return jnp.where(rank < float(min(TOP_N, n_blocks)), 1.0, 0.0)


def _nsa_prompt_kernel(q_ref, gt_ref, ck_ref, cv_ref, sk_ref, sv_ref, wk_ref, wv_ref, e_ref, o_ref):
    tq = q_ref.shape[0]
    t_len = sk_ref.shape[0]
    nblk = ck_ref.shape[0]
    i = pl.program_id(2)
    scale = HEAD_DIM ** -0.5
    q = q_ref[...]
    q4 = jnp.concatenate([q[:, h * HEAD_DIM:(h + 1) * HEAD_DIM] for h in range(HPG)], axis=0)
    pos_c = i * tq + lax.broadcasted_iota(jnp.int32, (tq, 1), 0)
    pos_r = i * tq + lax.broadcasted_iota(jnp.int32, (1, tq), 1)
    pos4_r = jnp.concatenate([pos_r] * HPG, axis=1)

    c2 = scale * LOG2E

    ck = ck_ref[...].astype(BF16)
    cv = cv_ref[...].astype(BF16)
    blk_c = lax.broadcasted_iota(jnp.int32, (nblk, 1), 0)
    vis_t = (blk_c + 1) * CMP_BLOCK - 1 <= pos4_r
    s_t = jnp.where(vis_t, _dot_nt(ck, q4), NEG)
    p_t = jnp.exp2((s_t - jnp.max(s_t, axis=0, keepdims=True)) * c2)
    p_t = p_t * (1.0 / jnp.sum(p_t, axis=0, keepdims=True))
    p_t = p_t * jnp.where(pos4_r >= CMP_BLOCK - 1, 1.0, 0.0)
    o_cmp = _dot_tn(p_t.astype(BF16), cv)
    imp_t = p_t[:, 0:tq]
    for h in range(1, HPG):
        imp_t = imp_t + p_t[:, h * tq:(h + 1) * tq]
    cur = pos_r // SEL_BLOCK
    forced = (blk_c == 0) | (blk_c == cur) | (blk_c == cur - 1)
    score_t = jnp.where(forced, FORCED, jnp.where(blk_c <= cur, imp_t, -1.0))
    sel_t = _rank_select(score_t, nblk)
    sel_b = sel_t.astype(BF16)
    key_r = lax.broadcasted_iota(jnp.int32, (1, KEY_CHUNK), 1)

    def body(kc, carry):
        m, l, acc = carry
        start = pl.multiple_of(kc * KEY_CHUNK, KEY_CHUNK)
        k = sk_ref[pl.ds(start, KEY_CHUNK), :]
        v = sv_ref[pl.ds(start, KEY_CHUNK), :]
        member = _dot_tn(sel_b, e_ref[kc])
        bias = jnp.where(start + key_r <= pos_c, (member - 1.0) * (-NEG), NEG)
        s = _dot_nt(q4, k) + jnp.concatenate([bias] * HPG, axis=0)
        m_new = jnp.maximum(m, jnp.max(s, axis=-1, keepdims=True))
        a = jnp.exp2((m - m_new) * c2)
        pr = jnp.exp2((s - m_new) * c2)
        l = a * l + jnp.sum(pr, axis=-1, keepdims=True)
        acc = a * acc + _dot(pr.astype(BF16), v)
        return m_new, l, acc

    n_chunks = (i * tq + tq + KEY_CHUNK - 1) // KEY_CHUNK
    init = (jnp.full((HPG * tq, 1), NEG, F32), jnp.zeros((HPG * tq, 1), F32), jnp.zeros((HPG * tq, HEAD_DIM), F32))
    _, l_s, acc_s = lax.fori_loop(0, n_chunks, body, init)
    o_slc = acc_s * (1.0 / l_s)

    n_win = WINDOW + WIN_Q
    o_win_parts = []
    for sub in range(tq // WIN_Q):
        q0 = i * tq + sub * WIN_Q
        start = pl.multiple_of(jnp.maximum(q0 - WINDOW, 0), WIN_Q)
        kw = wk_ref[pl.ds(start, n_win), :]
        vw = wv_ref[pl.ds(start, n_win), :]
        d = (q0 + lax.broadcasted_iota(jnp.int32, (WIN_Q, 1), 0)) - (start + lax.broadcasted_iota(jnp.int32, (1, n_win), 1))
        bias_w = jnp.where((d >= 0) & (d < WINDOW), 0.0, NEG)
        qs = jnp.concatenate([q[sub * WIN_Q:(sub + 1) * WIN_Q, h * HEAD_DIM:(h + 1) * HEAD_DIM] for h in range(HPG)],
                             axis=0)
        sw = _dot_nt(qs, kw) + jnp.concatenate([bias_w] * HPG, axis=0)
        pw = jnp.exp2((sw - jnp.max(sw, axis=-1, keepdims=True)) * c2)
        o_win_parts.append(_dot(pw.astype(BF16), vw) * (1.0 / jnp.sum(pw, axis=-1, keepdims=True)))

    gt = gt_ref[...]
    for h in range(HPG):
        r0, r1 = h * tq, (h + 1) * tq
        c0 = h * N_BRANCH
        o_win = jnp.concatenate([part[h * WIN_Q:(h + 1) * WIN_Q] for part in o_win_parts], axis=0)
        og = (gt[:, c0:c0 + 1] * o_cmp[r0:r1] + gt[:, c0 + 1:c0 + 2] * o_slc[r0:r1]
              + gt[:, c0 + 2:c0 + 3] * o_win)
        o_ref[:, h * HEAD_DIM:(h + 1) * HEAD_DIM] = og.astype(BF16)


def _nsa_prompt(q, gates, cmp_c, kvb, expand, batch, t_len):
    tq = Q_TILE
    nq = t_len // tq
    nblk = t_len // CMP_BLOCK
    gw = HPG * HEAD_DIM
    qrow = lambda b, g, i: (b * nq + i, g)

    def kv_spec(col0):
        return pl.BlockSpec((t_len, HEAD_DIM), lambda b, g, i: (b, col0 + g))

    return pl.pallas_call(
        _nsa_prompt_kernel,
        grid=(batch, N_KV, nq),
        in_specs=[
            pl.BlockSpec((tq, gw), qrow),
            pl.BlockSpec((tq, LANES), qrow),
            pl.BlockSpec((nblk, HEAD_DIM), lambda b, g, i: (b, g)),
            pl.BlockSpec((nblk, HEAD_DIM), lambda b, g, i: (b, N_KV + g)),
            kv_spec(2 * N_KV), kv_spec(3 * N_KV), kv_spec(4 * N_KV), kv_spec(5 * N_KV),
            pl.BlockSpec((t_len // KEY_CHUNK, nblk, KEY_CHUNK), lambda b, g, i: (0, 0, 0)),
        ],
        out_specs=pl.BlockSpec((tq, gw), qrow),
        out_shape=jax.ShapeDtypeStruct((batch * t_len, D_MODEL), BF16),
        compiler_params=_params("parallel", "parallel", "arbitrary"),
        name="nsa_prompt",
    )(q, gates, cmp_c, cmp_c, kvb, kvb, kvb, kvb, expand)


def _smp_cmp_kernel(q_ref, ck_ref, cv_ref, o_ref, imp_ref):
    nblk = ck_ref.shape[0]
    scale = HEAD_DIM ** -0.5
    q = q_ref[0, 0]
    rows = q.shape[0]
    tok = lax.broadcasted_iota(jnp.int32, (rows, 1), 0) % DEC_SEQ
    pos = PAST_LEN + tok
    blk = lax.broadcasted_iota(jnp.int32, (1, nblk), 1)
    vis = (blk + 1) * CMP_BLOCK - 1 <= pos
    p = _softmax_rows(jnp.where(vis, _dot_nt(q, ck_ref[...].astype(BF16)) * scale, NEG))
    p = p * jnp.where(pos >= CMP_BLOCK - 1, 1.0, 0.0)
    o_ref[0, 0] = _dot(p.astype(BF16), cv_ref[...].astype(BF16))
    imp = p[0:DEC_SEQ]
    for h in range(1, HPG):
        imp = imp + p[h * DEC_SEQ:(h + 1) * DEC_SEQ]
    imp_ref[0, 0] = imp


def _smp_cmp(qs, cmp_c, nblk):
    rows = HPG * DEC_SEQ
    return pl.pallas_call(
        _smp_cmp_kernel,
        grid=(DEC_BATCH, N_KV),
        in_specs=[
            pl.BlockSpec((1, 1, rows, HEAD_DIM), lambda b, g: (b, g, 0, 0)),
            pl.BlockSpec((nblk, HEAD_DIM), lambda b, g: (b, g)),
            pl.BlockSpec((nblk, HEAD_DIM), lambda b, g: (b, N_KV + g)),
        ],
        out_specs=[pl.BlockSpec((1, 1, rows, HEAD_DIM), lambda b, g: (b, g, 0, 0)),
                   pl.BlockSpec((1, 1, DEC_SEQ, nblk), lambda b, g: (b, g, 0, 0))],
        out_shape=[jax.ShapeDtypeStruct((DEC_BATCH, N_KV, rows, HEAD_DIM), F32),
                   jax.ShapeDtypeStruct((DEC_BATCH, N_KV, DEC_SEQ, nblk), F32)],
        compiler_params=_params("parallel", "parallel"),
        name="smp_cmp",
    )(qs, cmp_c, cmp_c)


def _smp_topk_kernel(imp_ref, idx_ref, *, n_sel_blocks):
    imp = imp_ref[...]
    rows, ncmp = imp.shape
    width = ncmp + LANES
    imp = jnp.concatenate([imp, jnp.zeros((rows, LANES), F32)], axis=1)
    j = lax.broadcasted_iota(jnp.int32, (rows, width), 1)
    tok = lax.broadcasted_iota(jnp.int32, (rows, 1), 0) % DEC_SEQ
    cur = (PAST_LEN + tok) // SEL_BLOCK
    forced = (j == 0) | (j == cur) | (j == cur - 1)
    score = jnp.where(forced, FORCED, jnp.where(j <= cur, imp, -1.0))
    score = jnp.where(j < n_sel_blocks, score, -jnp.inf)
    lane = lax.broadcasted_iota(jnp.int32, (rows, LANES), 1)
    out = jnp.zeros((rows, LANES), jnp.int32)
    jf = j.astype(F32)
    for n in range(min(TOP_N, n_sel_blocks)):
        m = jnp.max(score, axis=-1, keepdims=True)
        pick = jnp.min(jnp.where(score == m, jf, float(width)), axis=-1, keepdims=True)
        out = jnp.where(lane == n, pick.astype(jnp.int32), out)
        score = jnp.where(jf == pick, -jnp.inf, score)
    idx_ref[...] = out


def _smp_topk(imp, n_sel_blocks):
    rows = imp.shape[0]
    return pl.pallas_call(
        functools.partial(_smp_topk_kernel, n_sel_blocks=n_sel_blocks),
        out_shape=jax.ShapeDtypeStruct((rows, LANES), jnp.int32),
        name="smp_topk",
    )(imp)


def _smp_attn_kernel(idx_ref, pt_ref, q_ref, nsk_ref, nsv_ref, nwk_ref, nwv_ref, slc_hbm, win_hbm,
                     oslc_ref, owin_ref, kbuf, vbuf, wbuf, sem, *, n_pages, n_cache_blocks):
    n_sel = TOP_N
    per_page = PAGE_SIZE // SEL_BLOCK
    n_sel_keys = n_sel * SEL_BLOCK
    n_buf = wbuf.shape[2] - LANES
    s = pl.program_id(0)
    slot = s % 2

    def copies(step, slot_, for_wait):
        b, g = step // N_KV, step % N_KV
        out = []
        for t in range(DEC_SEQ):
            for n in range(n_sel):
                if for_wait:
                    row0 = 0
                else:
                    bid = jnp.minimum(idx_ref[(step * DEC_SEQ + t) * n_sel + n], n_cache_blocks - 1)
                    row0 = (pt_ref[b * n_pages + bid // per_page] * per_page + bid % per_page) * SEL_BLOCK
                dst = pl.ds(n * SEL_BLOCK, SEL_BLOCK)
                out.append(pltpu.make_async_copy(slc_hbm.at[pl.ds(row0, SEL_BLOCK), g, :],
                                                 kbuf.at[slot_, t, dst, :], sem.at[slot_]))
                out.append(pltpu.make_async_copy(slc_hbm.at[pl.ds(row0, SEL_BLOCK), N_KV + g, :],
                                                 vbuf.at[slot_, t, dst, :], sem.at[slot_]))
        for c in range(2):
            out.append(pltpu.make_async_copy(win_hbm.at[pl.ds(b * n_buf, n_buf), c * N_KV + g, :],
                                             wbuf.at[slot_, c, pl.ds(0, n_buf), :], sem.at[slot_]))
        return out

    @pl.when(s == 0)
    def _():
        for cp in copies(s, slot, False):
            cp.start()

    @pl.when(s + 1 < pl.num_programs(0))
    def _():
        for cp in copies(s + 1, 1 - slot, False):
            cp.start()

    for cp in copies(s, slot, True):
        cp.wait()

    scale = HEAD_DIM ** -0.5
    n_new = nsk_ref.shape[1]
    zeros = jnp.zeros((LANES, HEAD_DIM), F32)
    blk_lane = lax.broadcasted_iota(jnp.int32, (1, n_sel_keys + LANES), 1)
    in_cache = blk_lane < n_sel_keys
    off_new = blk_lane - n_sel_keys
    lane = lax.broadcasted_iota(jnp.int32, (1, n_buf + LANES), 1)
    kpos = jnp.where(lane < n_buf, PAST_LEN - n_buf + lane, PAST_LEN + lane - n_buf)
    real = (lane < n_buf) | (lane - n_buf < DEC_SEQ)

    wbuf[slot, 0, pl.ds(n_buf, LANES), :] = zeros
    wbuf[slot, 1, pl.ds(n_buf, LANES), :] = zeros
    wbuf[slot, 0, pl.ds(n_buf, n_new), :] = nwk_ref[0]
    wbuf[slot, 1, pl.ds(n_buf, n_new), :] = nwv_ref[0]
    kw = wbuf[slot, 0].astype(BF16)
    vw = wbuf[slot, 1].astype(BF16)

    for t in range(DEC_SEQ):
        pos = PAST_LEN + t
        q = q_ref[0, 0, t]
        ids = jnp.zeros((1, n_sel_keys + LANES), jnp.int32)
        has_new = jnp.zeros((1, 1), jnp.int32)
        for n in range(n_sel):
            bid = idx_ref[(s * DEC_SEQ + t) * n_sel + n]
            ids = jnp.where(blk_lane // SEL_BLOCK == n, bid, ids)
            has_new = jnp.maximum(has_new, jnp.where(bid == n_cache_blocks, 1, 0))
        kbuf[slot, t, pl.ds(n_sel_keys, LANES), :] = zeros
        vbuf[slot, t, pl.ds(n_sel_keys, LANES), :] = zeros
        kbuf[slot, t, pl.ds(n_sel_keys, n_new), :] = nsk_ref[0]
        vbuf[slot, t, pl.ds(n_sel_keys, n_new), :] = nsv_ref[0]
        tok = jnp.where(in_cache, ids * SEL_BLOCK + blk_lane % SEL_BLOCK, n_cache_blocks * SEL_BLOCK + off_new)
        src_ok = jnp.where(in_cache, jnp.where(ids != n_cache_blocks, 1, 0),
                           jnp.where(off_new < DEC_SEQ, 1, 0) * has_new)
        ok = (src_ok > 0) & (tok <= pos)
        ps = _softmax_rows(jnp.where(ok, _dot_nt(q, kbuf[slot, t].astype(BF16)) * scale, NEG))
        oslc_ref[0, 0, t] = _dot(ps.astype(BF16), vbuf[slot, t].astype(BF16))

        d = pos - kpos
        okw = real & (d >= 0) & (d < WINDOW)
        pw = _softmax_rows(jnp.where(okw, _dot_nt(q, kw) * scale, NEG))
        owin_ref[0, 0, t] = _dot(pw.astype(BF16), vw)


def _smp_attn(idx_flat, pt_flat, q8, cache_slc_rows, new_pad, cache_win_rows, n_pages, n_cache_blocks, n_buf):
    n_sel = TOP_N
    n_new = new_pad.shape[1]
    bg = lambda s, idx, pt: (s // N_KV, s % N_KV, 0, 0, 0)

    def new_spec(col0):
        return pl.BlockSpec((1, n_new, HEAD_DIM), lambda s, idx, pt: (s // N_KV, 0, col0 + s % N_KV))

    qo_spec = pl.BlockSpec((1, 1, DEC_SEQ, 8, HEAD_DIM), bg)
    grid_spec = pltpu.PrefetchScalarGridSpec(
        num_scalar_prefetch=2,
        grid=(DEC_BATCH * N_KV,),
        in_specs=[qo_spec, new_spec(2 * N_KV), new_spec(3 * N_KV), new_spec(4 * N_KV), new_spec(5 * N_KV),
                  pl.BlockSpec(memory_space=pl.ANY), pl.BlockSpec(memory_space=pl.ANY)],
        out_specs=[qo_spec, qo_spec],
        scratch_shapes=[pltpu.VMEM((2, DEC_SEQ, n_sel * SEL_BLOCK + LANES, HEAD_DIM), F32),
                        pltpu.VMEM((2, DEC_SEQ, n_sel * SEL_BLOCK + LANES, HEAD_DIM), F32),
                        pltpu.VMEM((2, 2, n_buf + LANES, HEAD_DIM), F32),
                        pltpu.SemaphoreType.DMA((2,))],
    )
    shape = jax.ShapeDtypeStruct((DEC_BATCH, N_KV, DEC_SEQ, 8, HEAD_DIM), F32)
    return pl.pallas_call(
        functools.partial(_smp_attn_kernel, n_pages=n_pages, n_cache_blocks=n_cache_blocks),
        grid_spec=grid_spec,
        out_shape=[shape, shape],
        compiler_params=_params("arbitrary"),
        name="smp_attn",
    )(idx_flat, pt_flat, q8, new_pad, new_pad, new_pad, new_pad, cache_slc_rows, cache_win_rows)


def _gate_combine_kernel(gt_ref, oc_ref, os_ref, ow_ref, o_ref):
    gt = gt_ref[...]
    for g in range(N_KV):
        for h in range(HPG):
            c0 = g * LANES + h * N_BRANCH
            lo = (g * HPG + h) * HEAD_DIM
            hi = lo + HEAD_DIM
            og = (gt[:, c0:c0 + 1] * oc_ref[:, lo:hi] + gt[:, c0 + 1:c0 + 2] * os_ref[:, lo:hi]
                  + gt[:, c0 + 2:c0 + 3] * ow_ref[:, lo:hi])
            o_ref[:, lo:hi] = og.astype(BF16)


def _gate_combine(gates, o_cmp, o_slc, o_win):
    return pl.pallas_call(
        _gate_combine_kernel,
        out_shape=jax.ShapeDtypeStruct(o_cmp.shape, BF16),
        name="gate_combine",
    )(gates, o_cmp, o_slc, o_win)


def _prep_weights(ln_g, ln_b, ffn_w_in, ffn_w_out, gmlp_w_in, gmlp_b_in, gmlp_ln_g, gmlp_ln_b, gmlp_w_s,
                  gmlp_b_s, gmlp_w_out, nsa_w_qg, nsa_w_o, w_kv, cmp_pos, w_cmp):
    nq = N_HEADS * HEAD_DIM
    wg = nsa_w_qg[:, :, nq:].reshape(N_B_LAYERS, D_MODEL, N_KV, GATE_COLS)
    wg = jnp.pad(wg, ((0, 0), (0, 0), (0, 0), (0, LANES - GATE_COLS))).reshape(N_B_LAYERS, D_MODEL, N_KV * LANES)
    pos_t = jnp.broadcast_to(cmp_pos[:, :, None, :], (CMP_BLOCK, 2, N_KV, HEAD_DIM)).reshape(CMP_BLOCK, KV_COLS)
    return dict(
        ln_g=ln_g.reshape(DEPTH, 3, 1, D_MODEL), ln_b=ln_b.reshape(DEPTH, 3, 1, D_MODEL),
        ffn_w_in=ffn_w_in, ffn_w_out=ffn_w_out,
        ffn_w_in0=ffn_w_in[0, 0].astype(BF16), ffn_w_out0=ffn_w_out[0, 0].astype(BF16),
        gmlp_w_in=gmlp_w_in.astype(BF16), gmlp_b_in=gmlp_b_in.reshape(N_A_LAYERS, 1, 2 * D_GATE),
        gmlp_ln_g=gmlp_ln_g.reshape(N_A_LAYERS, 1, D_GATE), gmlp_ln_b=gmlp_ln_b.reshape(N_A_LAYERS, 1, D_GATE),
        gmlp_w_s=gmlp_w_s, gmlp_b_s=gmlp_b_s, gmlp_w_out=gmlp_w_out.astype(BF16),
        w_q=nsa_w_qg[:, :, :nq].astype(BF16), w_g=wg.astype(BF16), w_o=nsa_w_o.astype(BF16),
        w_kv=w_kv.astype(BF16),
        pos8=pos_t.reshape(CMP_BLOCK, 2 * N_KV, HEAD_DIM),
        w_pair=jnp.concatenate([w_cmp[:, 0], w_cmp[:, 1]], axis=-1).astype(BF16).reshape(
            CMP_BLOCK // 2, 2 * HEAD_DIM, 2 * HEAD_DIM),
    )


def _trunks(xs, w, streams):
    xbs = [x.astype(BF16) for x in xs]
    ctxs = [None] * len(xs)
    v_rows = [[] for _ in xs]

    order = [(l, s) for l in range(DEPTH) for s in (0, 1)]
    ffn_w = [w["ffn_w_in0"], w["ffn_w_out0"]]

    def ffn(l, s):
        norm = 2 * s
        k = order.index((l, s))
        nxt = (w["ffn_w_in"], w["ffn_w_out"]) + order[k + 1] if k + 1 < len(order) else None
        out = _ffn(xs[0], xbs[0], xs[1], xbs[1], ffn_w[0], ffn_w[1], w["ln_g"][l, norm], w["ln_b"][l, norm], nxt)
        xs[0], xbs[0], xs[1], xbs[1] = out[:4]
        if nxt is not None:
            ffn_w[0], ffn_w[1] = out[4], out[5]

    for l in range(DEPTH):
        if l == N_A_LAYERS:
            for k, st in enumerate(streams):
                ctxs[k] = _kv_proj(xbs[k], w["w_kv"], st["tables"], st["pos_rows"])
        ffn(l, 0)
        for k, st in enumerate(streams):
            if l < N_A_LAYERS:
                rows, causal, w_sp, b_sp = st["spatial"](l)
                z = _gmlp_in(xbs[k], w["gmlp_w_in"][l], w["gmlp_b_in"][l], w["gmlp_ln_g"][l], w["gmlp_ln_b"][l])
                v_rows[k].append(z[:, D_GATE:])
                xs[k], xbs[k] = _gmlp_out(z, xs[k], w_sp, b_sp, w["gmlp_w_out"][l], w["ln_g"][l, 1],
                                          w["ln_b"][l, 1], rows, causal)
            else:
                o = st["attend"](l - N_A_LAYERS, xbs[k], ctxs[k])
                xs[k], xbs[k] = _proj_norm(o, xs[k], w["w_o"][l - N_A_LAYERS], w["ln_g"][l, 1], w["ln_b"][l, 1])
        ffn(l, 1)
    return [(xs[k], ctxs[k], v_rows[k]) for k in range(len(xs))]


def kernel(x_prompt, x_sample, cache_cmp_kv, cache_slc_kv, cache_win_kv, page_table, ln_g, ln_b, ffn_w_in,
           ffn_w_out, gmlp_w_in, gmlp_b_in, gmlp_ln_g, gmlp_ln_b, gmlp_w_s, gmlp_b_s, gmlp_w_out, nsa_w_qg,
           nsa_w_o, w_kv, cmp_pos, w_cmp):
    w = _prep_weights(ln_g, ln_b, ffn_w_in, ffn_w_out, gmlp_w_in, gmlp_b_in, gmlp_ln_g, gmlp_ln_b, gmlp_w_s,
                      gmlp_b_s, gmlp_w_out, nsa_w_qg, nsa_w_o, w_kv, cmp_pos, w_cmp)
    kv_shape = (2, N_KV, HEAD_DIM)

    mp = BATCH * SEQ
    tables_p = _rotary_tables(np.arange(SEQ))
    n_blk_p = SEQ // CMP_BLOCK
    key_blk = (np.arange(SEQ) // SEL_BLOCK).reshape(SEQ // KEY_CHUNK, 1, KEY_CHUNK)
    expand = jnp.asarray(key_blk == np.arange(n_blk_p)[None, :, None], BF16)
    cmp_p = {}

    def spatial_p(l):
        return CHUNK, CHUNK, w["gmlp_w_s"][l], w["gmlp_b_s"][l].T

    def attend_p(bl, xb, ctx):
        kvb, cmp_rows = ctx[0], ctx[1]
        if "c" not in cmp_p:
            slabs = jnp.arange(mp // PAGE_SIZE, dtype=jnp.int32)
            cmp_p["c"] = _compress_pages(cmp_rows, slabs, w["pos8"], w["w_pair"]).reshape(-1, KV_COLS)
        q, gates = _qg_proj(xb, w["w_q"][bl], w["w_g"][bl], tables_p, SEQ)
        return _nsa_prompt(q, gates, cmp_p["c"], kvb, expand, BATCH, SEQ)

    ms = DEC_BATCH * DEC_SEQ
    pos_s = PAST_LEN + np.arange(DEC_SEQ)
    tables_s = _rotary_tables(np.tile(pos_s, DEC_BATCH))
    n_pages = PAST_LEN // PAGE_SIZE
    n_cache_blocks = PAST_LEN // SEL_BLOCK
    n_sel_blocks = -(-(PAST_LEN + DEC_SEQ) // SEL_BLOCK)
    n_cmp_s = (PAST_LEN + DEC_SEQ) // CMP_BLOCK
    pt_flat = page_table.reshape(-1)
    cache_cmp_rows = cache_cmp_kv.reshape(-1, 2 * N_KV, HEAD_DIM)
    cache_slc_rows = cache_slc_kv.reshape(-1, 2 * N_KV, HEAD_DIM)
    cache_win_rows = cache_win_kv.reshape(-1, 2 * N_KV, HEAD_DIM)
    n_buf = cache_win_kv.shape[1]
    cmp_s = {}

    def spatial_s(l):
        c = min(DEC_SEQ, CHUNK)
        w_t = jnp.tile(w["gmlp_w_s"][l][:, :c, :c], (1, DEC_BATCH, DEC_BATCH))
        b_t = jnp.tile(w["gmlp_b_s"][l][:, :c].T, (DEC_BATCH, 1))
        return ms, c, w_t, b_t

    def attend_s(bl, xb, ctx):
        if "c" not in cmp_s:
            kv = jnp.concatenate([r.reshape(ms, KV_COLS) for r in ctx[1:]], axis=1)
            cmp_s["c"] = _compress_pages(cache_cmp_rows, pt_flat, w["pos8"], w["w_pair"]).reshape(-1, KV_COLS)
            new = kv.reshape(DEC_BATCH, DEC_SEQ, N_BRANCH * KV_COLS)
            cmp_s["new"] = jnp.pad(new, ((0, 0), (0, 8 - DEC_SEQ), (0, 0)))
        q, gates = _qg_proj(xb, w["w_q"][bl], w["w_g"][bl], tables_s, ms)
        q5 = q.reshape(DEC_BATCH, DEC_SEQ, N_KV, HPG, HEAD_DIM)
        qs = q5.transpose(0, 2, 3, 1, 4).reshape(DEC_BATCH, N_KV, HPG * DEC_SEQ, HEAD_DIM)
        o_cmp, imp = _smp_cmp(qs, cmp_s["c"], n_cmp_s)
        idx = _smp_topk(imp.reshape(DEC_BATCH * N_KV * DEC_SEQ, n_cmp_s), n_sel_blocks)[:, :TOP_N]
        q8 = jnp.pad(q5.transpose(0, 2, 1, 3, 4), ((0, 0), (0, 0), (0, 0), (0, 8 - HPG), (0, 0)))
        o_slc, o_win = _smp_attn(idx.reshape(-1), pt_flat, q8, cache_slc_rows, cmp_s["new"], cache_win_rows,
                                 n_pages, n_cache_blocks, n_buf)
        o_cmp = o_cmp.reshape(DEC_BATCH, N_KV, HPG, DEC_SEQ, HEAD_DIM).transpose(0, 3, 1, 2, 4).reshape(ms, D_MODEL)
        o_slc = o_slc[:, :, :, :HPG].transpose(0, 2, 1, 3, 4).reshape(ms, D_MODEL)
        o_win = o_win[:, :, :, :HPG].transpose(0, 2, 1, 3, 4).reshape(ms, D_MODEL)
        return _gate_combine(gates, o_cmp, o_slc, o_win)

    streams = [dict(tables=tables_p, pos_rows=SEQ, spatial=spatial_p, attend=attend_p),
               dict(tables=tables_s, pos_rows=ms, spatial=spatial_s, attend=attend_s)]
    (y_p, ctx_p, _), (y_s, ctx_s, v_s) = _trunks(
        [x_prompt.reshape(mp, D_MODEL), x_sample.reshape(ms, D_MODEL)], w, streams)
    p_cmp, p_slc, p_win = (r.reshape(BATCH, SEQ, *kv_shape) for r in ctx_p[1:])
    p_win = p_win[:, -min(WINDOW, SEQ):]
    s_cmp, s_slc, s_new = (r.reshape(DEC_BATCH, DEC_SEQ, *kv_shape) for r in ctx_s[1:])
    s_win = jnp.concatenate([cache_win_kv, s_new], axis=1)[:, -min(WINDOW, PAST_LEN + DEC_SEQ):]
    s_v = jnp.stack([v.reshape(DEC_BATCH, DEC_SEQ, D_GATE) for v in v_s])

    return (y_p.reshape(BATCH, SEQ, D_MODEL), y_s.reshape(DEC_BATCH, DEC_SEQ, D_MODEL), p_cmp, p_slc, p_win,
            s_cmp, s_slc, s_win, s_v)
```

```python
import functools
import math

import numpy as np
import jax
import jax.numpy as jnp
from jax import lax
from jax.experimental import pallas as pl
from jax.experimental.pallas import tpu as pltpu

D_MODEL = 2048
BATCH = 4
SEQ = 2048
DEPTH = 4
DEC_BATCH = 8
DEC_SEQ = 4
PAST_LEN = 16384
PAGE_SIZE = 128

N_A_LAYERS = DEPTH // 2
N_B_LAYERS = DEPTH - N_A_LAYERS
D_FF = 5632
D_GATE = D_MODEL
CHUNK = 128
N_GROUPS_A = 8
GROUP_DIM_A = D_GATE // N_GROUPS_A
HEAD_DIM = 128
N_HEADS = D_MODEL // HEAD_DIM
N_KV = 4
HPG = N_HEADS // N_KV
ROT_DIM = HEAD_DIM // 4
ROPE_THETA = 500000.0
CMP_BLOCK = 64
SEL_BLOCK = 64
TOP_N = 16
WINDOW = 512
N_BRANCH = 3
ALPHA = (2 * DEPTH) ** 0.25
LN_EPS = 1e-5
NEG = -1e30
FORCED = 1e4

F32 = jnp.float32
BF16 = jnp.bfloat16

V7X_VMEM_LIMIT_BYTES = 56 * 1024 * 1024
LOG2E = math.log2(math.e)
LANES = 128
KV_COLS = 2 * N_KV * HEAD_DIM
GATE_COLS = HPG * N_BRANCH
KEY_CHUNK = 512
PAGES_PER_STEP = 16
Q_TILE = 512
WIN_Q = 128


def _params(*sem):
    return pltpu.CompilerParams(dimension_semantics=sem, vmem_limit_bytes=V7X_VMEM_LIMIT_BYTES)


def _row_tile(m, cap=512):
    return m if m <= cap else cap


def _dot(a, b):
    return jnp.dot(a, b, preferred_element_type=F32)


def _dot_nt(a, b):
    return lax.dot_general(a, b, (((1,), (1,)), ((), ())), preferred_element_type=F32)


def _dot_tn(a, b):
    return lax.dot_general(a, b, (((0,), (0,)), ((), ())), preferred_element_type=F32)


def _layer_norm(x, g, b):
    mu = jnp.mean(x, axis=-1, keepdims=True)
    d = x - mu
    var = jnp.mean(d * d, axis=-1, keepdims=True)
    return d * lax.rsqrt(var + LN_EPS) * g + b


def _sigmoid(x):
    return 1.0 / (1.0 + jnp.exp(-x))


def _softmax_rows(s):
    m = jnp.max(s, axis=-1, keepdims=True)
    p = jnp.exp(s - m)
    return p / jnp.sum(p, axis=-1, keepdims=True)


def _rotate(x, cos_t, sin_lo, sin_hi):
    half = ROT_DIM // 2
    return (x * cos_t + pltpu.roll(x, LANES - half, 1) * sin_lo + pltpu.roll(x, half, 1) * sin_hi)


def _rotary_tables(pos):
    half = ROT_DIM // 2
    inv = ROPE_THETA ** (-jnp.arange(half, dtype=F32) / half)
    ang = jnp.asarray(pos).astype(F32)[:, None] * inv[None, :]
    cos, sin = jnp.cos(ang), jnp.sin(ang)
    rows = ang.shape[0]
    pad = jnp.zeros((rows, LANES - ROT_DIM), F32)
    zero = jnp.zeros((rows, half), F32)
    cos_t = jnp.concatenate([cos, cos, pad + 1.0], axis=1)
    sin_lo = jnp.concatenate([-sin, zero, pad], axis=1)
    sin_hi = jnp.concatenate([zero, sin, pad], axis=1)
    return cos_t, sin_lo, sin_hi


def _ffn_kernel(xb_ref, x_ref, xsb_ref, xs_ref, wg_ref, wu_ref, wo_ref, g_ref, b_ref, *refs, cast_next):
    i, f = pl.program_id(0), pl.program_id(1)
    tm = xb_ref.shape[0]
    last = pl.num_programs(1) - 1
    if cast_next:
        nwi_ref, nwo_ref, y_ref, yb_ref, ys_ref, ysb_ref, nwib_ref, nwob_ref, accs_ref = refs
    else:
        y_ref, yb_ref, ys_ref, ysb_ref, accs_ref = refs

    def swiglu(lhs):
        if cast_next:
            nwib_ref[...] = nwi_ref[...].astype(BF16)
            nwob_ref[...] = nwo_ref[...].astype(BF16)
        gate = _dot(lhs, wg_ref[...])
        up = _dot(lhs, wu_ref[...])
        return _dot((gate * _sigmoid(gate) * up).astype(BF16), wo_ref[...])

    @pl.when(f == 0)
    def _():
        y_ref[...] = jnp.zeros_like(y_ref)

    @pl.when(i == 0)
    def _():
        @pl.when(f == 0)
        def _():
            accs_ref[...] = jnp.zeros_like(accs_ref)

        part = swiglu(jnp.concatenate([xb_ref[...], xsb_ref[...]], axis=0))
        y_ref[...] += part[:tm]
        accs_ref[...] += part[tm:]

        @pl.when(f == last)
        def _():
            ys = _layer_norm(ALPHA * xs_ref[...] + 0.5 * accs_ref[...], g_ref[...], b_ref[...])
            ys_ref[...] = ys
            ysb_ref[...] = ys.astype(BF16)

    @pl.when(i > 0)
    def _():
        y_ref[...] += swiglu(xb_ref[...])

    @pl.when(f == last)
    def _():
        y = _layer_norm(ALPHA * x_ref[...] + 0.5 * y_ref[...], g_ref[...], b_ref[...])
        y_ref[...] = y
        yb_ref[...] = y.astype(BF16)


def _ffn(x, xb, xs, xsb, w_in_b, w_out_b, g, b, nxt=None):
    m, ms = x.shape[0], xs.shape[0]
    tm = _row_tile(m)
    tf = 512
    nf = D_FF // tf
    row = lambda i, f: (i, 0)
    fixed = lambda i, f: (0, 0)
    in_specs = [
        pl.BlockSpec((tm, D_MODEL), row),
        pl.BlockSpec((tm, D_MODEL), row),
        pl.BlockSpec((ms, D_MODEL), fixed),
        pl.BlockSpec((ms, D_MODEL), fixed),
        pl.BlockSpec((D_MODEL, tf), lambda i, f: (0, f)),
        pl.BlockSpec((D_MODEL, tf), lambda i, f: (0, nf + f)),
        pl.BlockSpec((tf, D_MODEL), lambda i, f: (f, 0)),
        pl.BlockSpec((1, D_MODEL), fixed),
        pl.BlockSpec((1, D_MODEL), fixed),
    ]
    out_specs = [pl.BlockSpec((tm, D_MODEL), row), pl.BlockSpec((tm, D_MODEL), row),
                 pl.BlockSpec((ms, D_MODEL), fixed), pl.BlockSpec((ms, D_MODEL), fixed)]
    out_shape = [jax.ShapeDtypeStruct((m, D_MODEL), F32), jax.ShapeDtypeStruct((m, D_MODEL), BF16),
                 jax.ShapeDtypeStruct((ms, D_MODEL), F32), jax.ShapeDtypeStruct((ms, D_MODEL), BF16)]
    args = [xb, x, xsb, xs, w_in_b, w_in_b, w_out_b, g, b]
    if nxt is not None:
        nw_in, nw_out, nl, ns = nxt
        steps = (m // tm) * nf
        ri = D_MODEL // 2
        ci, co = 2 * (2 * D_FF) // steps, D_FF // steps
        assert 2 * (2 * D_FF // ci) == steps and co * steps == D_FF and ci % LANES == 0 and co % 16 == 0
        in_slab = lambda i, f: ((i * nf + f) % 2, (i * nf + f) // 2)
        out_slab = lambda i, f: (i * nf + f, 0)
        in_specs += [pl.BlockSpec((None, None, ri, ci), lambda i, f: (nl, ns) + in_slab(i, f)),
                     pl.BlockSpec((None, None, co, D_MODEL), lambda i, f: (nl, ns) + out_slab(i, f))]
        out_specs += [pl.BlockSpec((ri, ci), in_slab), pl.BlockSpec((co, D_MODEL), out_slab)]
        out_shape += [jax.ShapeDtypeStruct((D_MODEL, 2 * D_FF), BF16), jax.ShapeDtypeStruct((D_FF, D_MODEL), BF16)]
        args += [nw_in, nw_out]
    return pl.pallas_call(
        functools.partial(_ffn_kernel, cast_next=nxt is not None),
        grid=(m // tm, nf),
        in_specs=in_specs,
        out_specs=out_specs,
        out_shape=out_shape,
        scratch_shapes=[pltpu.VMEM((ms, D_MODEL), F32)],
        compiler_params=_params("arbitrary", "arbitrary"),
        name="ffn",
    )(*args)


def _gmlp_in_kernel(xb_ref, w_ref, bias_ref, lg_ref, lb_ref, z_ref):
    n = pl.program_id(0)
    z = _dot(xb_ref[...], w_ref[...]) + bias_ref[...]
    z = 0.5 * z * (1.0 + lax.erf(z * (1.0 / math.sqrt(2.0))))

    @pl.when(n == 0)
    def _():
        z_ref[...] = z

    @pl.when(n == 1)
    def _():
        z_ref[...] = _layer_norm(z, lg_ref[...], lb_ref[...])


def _gmlp_in(xb, w, bias, lg, lb):
    m = xb.shape[0]
    tm = _row_tile(m)
    return pl.pallas_call(
        _gmlp_in_kernel,
        grid=(2, m // tm),
        in_specs=[
            pl.BlockSpec((tm, D_MODEL), lambda n, i: (i, 0)),
            pl.BlockSpec((D_MODEL, D_GATE), lambda n, i: (0, n)),
            pl.BlockSpec((1, D_GATE), lambda n, i: (0, n)),
            pl.BlockSpec((1, D_GATE), lambda n, i: (0, 0)),
            pl.BlockSpec((1, D_GATE), lambda n, i: (0, 0)),
        ],
        out_specs=pl.BlockSpec((tm, D_GATE), lambda n, i: (i, n)),
        out_shape=jax.ShapeDtypeStruct((m, 2 * D_GATE), F32),
        compiler_params=_params("arbitrary", "arbitrary"),
        name="gmlp_in",
    )(xb, w, bias, lg, lb)


def _gmlp_out_kernel(u_ref, v_ref, x_ref, ws_ref, bs_ref, wo_ref, g_ref, b_ref, y_ref, yb_ref, *, rows, causal):
    tm = u_ref.shape[0]
    r_i = lax.broadcasted_iota(jnp.int32, (rows, rows), 0)
    c_i = lax.broadcasted_iota(jnp.int32, (rows, rows), 1)
    keep = ((r_i // causal) == (c_i // causal)) & (c_i <= r_i)
    w_sp = [jnp.where(keep, ws_ref[g], 0.0).astype(BF16) for g in range(N_GROUPS_A)]
    bs = bs_ref[...]
    chunks = []
    for ch in range(tm // rows):
        u = u_ref[ch * rows:(ch + 1) * rows, :]
        v = v_ref[ch * rows:(ch + 1) * rows, :]
        parts = []
        for g in range(N_GROUPS_A):
            lo, hi = g * GROUP_DIM_A, (g + 1) * GROUP_DIM_A
            mixed = _dot(w_sp[g], v[:, lo:hi].astype(BF16)) + bs[:, g:g + 1]
            parts.append((u[:, lo:hi] * mixed).astype(BF16))
        chunks.append(jnp.concatenate(parts, axis=1))
    y = chunks[0] if len(chunks) == 1 else jnp.concatenate(chunks, axis=0)
    f = _dot(y, wo_ref[...])
    out = _layer_norm(ALPHA * x_ref[...] + f, g_ref[...], b_ref[...])
    y_ref[...] = out
    yb_ref[...] = out.astype(BF16)


def _gmlp_out(z, x, w_sp, b_sp, w_out, g, b, rows, causal):
    m = x.shape[0]
    tm = _row_tile(m)
    row = lambda i: (i, 0)
    kern = functools.partial(_gmlp_out_kernel, rows=rows, causal=causal)
    return pl.pallas_call(
        kern,
        grid=(m // tm,),
        in_specs=[
            pl.BlockSpec((tm, D_GATE), lambda i: (i, 0)),
            pl.BlockSpec((tm, D_GATE), lambda i: (i, 1)),
            pl.BlockSpec((tm, D_MODEL), row),
            pl.BlockSpec((N_GROUPS_A, rows, rows), lambda i: (0, 0, 0)),
            pl.BlockSpec((rows, N_GROUPS_A), lambda i: (0, 0)),
            pl.BlockSpec((D_GATE, D_MODEL), lambda i: (0, 0), pipeline_mode=pl.Buffered(1)),
            pl.BlockSpec((1, D_MODEL), lambda i: (0, 0)),
            pl.BlockSpec((1, D_MODEL), lambda i: (0, 0)),
        ],
        out_specs=[pl.BlockSpec((tm, D_MODEL), row), pl.BlockSpec((tm, D_MODEL), row)],
        out_shape=[jax.ShapeDtypeStruct((m, D_MODEL), F32), jax.ShapeDtypeStruct((m, D_MODEL), BF16)],
        compiler_params=_params("parallel"),
        name="gmlp_out",
    )(z, z, x, w_sp, b_sp, w_out, g, b)


def _proj_norm_kernel(a_ref, x_ref, w_ref, g_ref, b_ref, y_ref, yb_ref):
    out = _layer_norm(ALPHA * x_ref[...] + _dot(a_ref[...], w_ref[...]), g_ref[...], b_ref[...])
    y_ref[...] = out
    yb_ref[...] = out.astype(BF16)


def _proj_norm(a, x, w, g, b):
    m = x.shape[0]
    tm = _row_tile(m)
    row = lambda i: (i, 0)
    return pl.pallas_call(
        _proj_norm_kernel,
        grid=(m // tm,),
        in_specs=[
            pl.BlockSpec((tm, D_MODEL), row),
            pl.BlockSpec((tm, D_MODEL), row),
            pl.BlockSpec((D_MODEL, D_MODEL), lambda i: (0, 0), pipeline_mode=pl.Buffered(1)),
            pl.BlockSpec((1, D_MODEL), lambda i: (0, 0)),
            pl.BlockSpec((1, D_MODEL), lambda i: (0, 0)),
        ],
        out_specs=[pl.BlockSpec((tm, D_MODEL), row), pl.BlockSpec((tm, D_MODEL), row)],
        out_shape=[jax.ShapeDtypeStruct((m, D_MODEL), F32), jax.ShapeDtypeStruct((m, D_MODEL), BF16)],
        compiler_params=_params("parallel"),
        name="proj_norm",
    )(a, x, w, g, b)


def _kv_proj_kernel(xb_ref, w_ref, cos_ref, slo_ref, shi_ref, kvb_ref, cmp_hbm, slc_hbm, win_hbm, stage, sem):
    i, n = pl.program_id(0), pl.program_id(1)
    tm = xb_ref.shape[0]
    step = i * pl.num_programs(1) + n
    last = pl.num_programs(0) * pl.num_programs(1) - 1
    slot = step % 2
    outs = (cmp_hbm, slc_hbm, win_hbm)

    def copies(out_hbm, c, slot_):
        return [pltpu.make_async_copy(stage.at[slot_, :, pl.ds(g * HEAD_DIM, HEAD_DIM)],
                                      out_hbm.at[pl.ds(i * tm, tm), c * N_KV + g, :], sem.at[slot_])
                for g in range(N_KV)]

    @pl.when(step >= 2)
    def _():
        for cp in copies(cmp_hbm, 0, slot):
            cp.wait()

    acc = _dot(xb_ref[...], w_ref[...])
    for col in range(2 * N_BRANCH):
        @pl.when(n == col)
        def _(col=col):
            br, c = divmod(col, 2)
            if c == 0:
                cos_t, s_lo, s_hi = cos_ref[...], slo_ref[...], shi_ref[...]
                val = jnp.concatenate([_rotate(acc[:, h * HEAD_DIM:(h + 1) * HEAD_DIM], cos_t, s_lo, s_hi)
                                       for h in range(N_KV)], axis=1)
            else:
                val = acc
            stage[slot] = val
            kvb_ref[...] = val.astype(BF16)
            for cp in copies(outs[br], c, slot):
                cp.start()

    @pl.when(step == last)
    def _():
        for cp in copies(cmp_hbm, 0, 1 - slot) + copies(cmp_hbm, 0, slot):
            cp.wait()


def _kv_proj(xb, w, tables, pos_rows):
    m = xb.shape[0]
    tm = _row_tile(m, 1024)
    half = N_KV * HEAD_DIM
    ncol = N_BRANCH * 2
    nt = pos_rows // tm
    tab = lambda i, n: (i % nt, 0)
    rows = jax.ShapeDtypeStruct((m, 2 * N_KV, HEAD_DIM), F32)
    hbm = pl.BlockSpec(memory_space=pl.ANY)
    return pl.pallas_call(
        _kv_proj_kernel,
        grid=(m // tm, ncol),
        in_specs=[
            pl.BlockSpec((tm, D_MODEL), lambda i, n: (i, 0)),
            pl.BlockSpec((D_MODEL, half), lambda i, n: (0, n)),
            pl.BlockSpec((tm, LANES), tab),
            pl.BlockSpec((tm, LANES), tab),
            pl.BlockSpec((tm, LANES), tab),
        ],
        out_specs=[pl.BlockSpec((tm, half), lambda i, n: (i, n)), hbm, hbm, hbm],
        out_shape=[jax.ShapeDtypeStruct((m, ncol * half), BF16), rows, rows, rows],
        scratch_shapes=[pltpu.VMEM((2, tm, half), F32), pltpu.SemaphoreType.DMA((2,))],
        compiler_params=_params("arbitrary", "arbitrary"),
        name="kv_proj",
    )(xb, w, *tables)


def _qg_proj_kernel(xb_ref, wq_ref, wg_ref, cos_ref, slo_ref, shi_ref, q_ref, gt_ref):
    xb = xb_ref[...]
    acc = _dot(xb, wq_ref[...])
    cos_t, s_lo, s_hi = cos_ref[...], slo_ref[...], shi_ref[...]
    for h in range(N_HEADS):
        lo, hi = h * HEAD_DIM, (h + 1) * HEAD_DIM
        q_ref[:, lo:hi] = _rotate(acc[:, lo:hi], cos_t, s_lo, s_hi).astype(BF16)
    gt_ref[...] = _sigmoid(_dot(xb, wg_ref[...]))


def _qg_proj(xb, wq, wg, tables, pos_rows):
    m = xb.shape[0]
    tm = _row_tile(m)
    nt = pos_rows // tm
    tab = lambda i: (i % nt, 0)
    gcols = N_KV * LANES
    return pl.pallas_call(
        _qg_proj_kernel,
        grid=(m // tm,),
        in_specs=[
            pl.BlockSpec((tm, D_MODEL), lambda i: (i, 0)),
            pl.BlockSpec((D_MODEL, D_MODEL), lambda i: (0, 0), pipeline_mode=pl.Buffered(1)),
            pl.BlockSpec((D_MODEL, gcols), lambda i: (0, 0)),
            pl.BlockSpec((tm, LANES), tab),
            pl.BlockSpec((tm, LANES), tab),
            pl.BlockSpec((tm, LANES), tab),
        ],
        out_specs=[pl.BlockSpec((tm, D_MODEL), lambda i: (i, 0)), pl.BlockSpec((tm, gcols), lambda i: (i, 0))],
        out_shape=[jax.ShapeDtypeStruct((m, D_MODEL), BF16), jax.ShapeDtypeStruct((m, gcols), F32)],
        compiler_params=_params("parallel"),
        name="qg_proj",
    )(xb, wq, wg, *tables)


def _compress_pages_kernel(tbl_ref, *refs):
    del tbl_ref
    npg = PAGES_PER_STEP
    pages = refs[:npg]
    pos_ref, w_ref, out_ref = refs[npg:]
    per_page = PAGE_SIZE // CMP_BLOCK
    rows = npg * per_page * 2 * N_KV
    acc = jnp.zeros((rows, 2 * HEAD_DIM), F32)
    for j in range(CMP_BLOCK // 2):
        halves = []
        for l in (2 * j, 2 * j + 1):
            pos_l = pos_ref[l]
            tiles = [pages[p][l + CMP_BLOCK * h] + pos_l for p in range(npg) for h in range(per_page)]
            halves.append(jnp.concatenate(tiles, axis=0).astype(BF16))
        acc = acc + _dot(jnp.concatenate(halves, axis=1), w_ref[j])
    is_k = (lax.broadcasted_iota(jnp.int32, (rows, HEAD_DIM), 0) % (2 * N_KV)) < N_KV
    out_ref[...] = jnp.where(is_k, acc[:, :HEAD_DIM], acc[:, HEAD_DIM:])


def _compress_pages(cache, table, pos8, w_pair):
    npg = PAGES_PER_STEP
    rows = npg * (PAGE_SIZE // CMP_BLOCK) * 2 * N_KV

    def page_spec(p):
        return pl.BlockSpec((PAGE_SIZE, 2 * N_KV, HEAD_DIM), lambda i, tbl: (tbl[i * npg + p], 0, 0))

    grid_spec = pltpu.PrefetchScalarGridSpec(
        num_scalar_prefetch=1,
        grid=(table.shape[0] // npg,),
        in_specs=[page_spec(p) for p in range(npg)] + [
            pl.BlockSpec((CMP_BLOCK, 2 * N_KV, HEAD_DIM), lambda i, tbl: (0, 0, 0)),
            pl.BlockSpec((CMP_BLOCK // 2, 2 * HEAD_DIM, 2 * HEAD_DIM), lambda i, tbl: (0, 0, 0)),
        ],
        out_specs=pl.BlockSpec((rows, HEAD_DIM), lambda i, tbl: (i, 0)),
    )
    return pl.pallas_call(
        _compress_pages_kernel,
        grid_spec=grid_spec,
        out_shape=jax.ShapeDtypeStruct((table.shape[0] // npg * rows, HEAD_DIM), F32),
        compiler_params=_params("arbitrary"),
        name="compress_pages",
    )(table, *([cache] * npg), pos8, w_pair)


def _rank_select(score_t, n_blocks):
    idx = lax.broadcasted_iota(jnp.int32, score_t.shape, 0)
    rank = jnp.zeros(score_t.shape, F32)
    for i in range(n_blocks):
        row = score_t[i:i + 1, :]
        rank = rank + jnp.where(idx > i, jnp.where(row >= score_t, 1.0, 0.0), jnp.where(row > score_t, 1.0, 0.0))
    return jnp.where(rank < float(min(TOP_N, n_blocks)), 1.0, 0.0)


def _nsa_prompt_kernel(q_ref, gt_ref, ck_ref, cv_ref, sk_ref, sv_ref, wk_ref, wv_ref, e_ref, o_ref):
    tq = q_ref.shape[0]
    t_len = sk_ref.shape[0]
    nblk = ck_ref.shape[0]
    i = pl.program_id(2)
    scale = HEAD_DIM ** -0.5
    q = q_ref[...]
    q4 = jnp.concatenate([q[:, h * HEAD_DIM:(h + 1) * HEAD_DIM] for h in range(HPG)], axis=0)
    pos_c = i * tq + lax.broadcasted_iota(jnp.int32, (tq, 1), 0)
    pos_r = i * tq + lax.broadcasted_iota(jnp.int32, (1, tq), 1)
    pos4_r = jnp.concatenate([pos_r] * HPG, axis=1)

    c2 = scale * LOG2E

    ck = ck_ref[...].astype(BF16)
    cv = cv_ref[...].astype(BF16)
    blk_c = lax.broadcasted_iota(jnp.int32, (nblk, 1), 0)
    vis_t = (blk_c + 1) * CMP_BLOCK - 1 <= pos4_r
    s_t = jnp.where(vis_t, _dot_nt(ck, q4), NEG)
    p_t = jnp.exp2((s_t - jnp.max(s_t, axis=0, keepdims=True)) * c2)
    p_t = p_t * (1.0 / jnp.sum(p_t, axis=0, keepdims=True))
    p_t = p_t * jnp.where(pos4_r >= CMP_BLOCK - 1, 1.0, 0.0)
    o_cmp = _dot_tn(p_t.astype(BF16), cv)
    imp_t = p_t[:, 0:tq]
    for h in range(1, HPG):
        imp_t = imp_t + p_t[:, h * tq:(h + 1) * tq]
    cur = pos_r // SEL_BLOCK
    forced = (blk_c == 0) | (blk_c == cur) | (blk_c == cur - 1)
    score_t = jnp.where(forced, FORCED, jnp.where(blk_c <= cur, imp_t, -1.0))
    sel_t = _rank_select(score_t, nblk)
    sel_b = sel_t.astype(BF16)
    key_r = lax.broadcasted_iota(jnp.int32, (1, KEY_CHUNK), 1)

    def body(kc, carry):
        m, l, acc = carry
        start = pl.multiple_of(kc * KEY_CHUNK, KEY_CHUNK)
        k = sk_ref[pl.ds(start, KEY_CHUNK), :]
        v = sv_ref[pl.ds(start, KEY_CHUNK), :]
        member = _dot_tn(sel_b, e_ref[kc])
        bias = jnp.where(start + key_r <= pos_c, (member - 1.0) * (-NEG), NEG)
        s = _dot_nt(q4, k) + jnp.concatenate([bias] * HPG, axis=0)
        m_new = jnp.maximum(m, jnp.max(s, axis=-1, keepdims=True))
        a = jnp.exp2((m - m_new) * c2)
        pr = jnp.exp2((s - m_new) * c2)
        l = a * l + jnp.sum(pr, axis=-1, keepdims=True)
        acc = a * acc + _dot(pr.astype(BF16), v)
        return m_new, l, acc

    n_chunks = (i * tq + tq + KEY_CHUNK - 1) // KEY_CHUNK
    init = (jnp.full((HPG * tq, 1), NEG, F32), jnp.zeros((HPG * tq, 1), F32), jnp.zeros((HPG * tq, HEAD_DIM), F32))
    _, l_s, acc_s = lax.fori_loop(0, n_chunks, body, init)
    o_slc = acc_s * (1.0 / l_s)

    n_win = WINDOW + WIN_Q
    o_win_parts = []
    for sub in range(tq // WIN_Q):
        q0 = i * tq + sub * WIN_Q
        start = pl.multiple_of(jnp.maximum(q0 - WINDOW, 0), WIN_Q)
        kw = wk_ref[pl.ds(start, n_win), :]
        vw = wv_ref[pl.ds(start, n_win), :]
        d = (q0 + lax.broadcasted_iota(jnp.int32, (WIN_Q, 1), 0)) - (start + lax.broadcasted_iota(jnp.int32, (1, n_win), 1))
        bias_w = jnp.where((d >= 0) & (d < WINDOW), 0.0, NEG)
        qs = jnp.concatenate([q[sub * WIN_Q:(sub + 1) * WIN_Q, h * HEAD_DIM:(h + 1) * HEAD_DIM] for h in range(HPG)],
                             axis=0)
        sw = _dot_nt(qs, kw) + jnp.concatenate([bias_w] * HPG, axis=0)
        pw = jnp.exp2((sw - jnp.max(sw, axis=-1, keepdims=True)) * c2)
        o_win_parts.append(_dot(pw.astype(BF16), vw) * (1.0 / jnp.sum(pw, axis=-1, keepdims=True)))

    gt = gt_ref[...]
    for h in range(HPG):
        r0, r1 = h * tq, (h + 1) * tq
        c0 = h * N_BRANCH
        o_win = jnp.concatenate([part[h * WIN_Q:(h + 1) * WIN_Q] for part in o_win_parts], axis=0)
        og = (gt[:, c0:c0 + 1] * o_cmp[r0:r1] + gt[:, c0 + 1:c0 + 2] * o_slc[r0:r1]
              + gt[:, c0 + 2:c0 + 3] * o_win)
        o_ref[:, h * HEAD_DIM:(h + 1) * HEAD_DIM] = og.astype(BF16)


def _nsa_prompt(q, gates, cmp_c, kvb, expand, batch, t_len):
    tq = Q_TILE
    nq = t_len // tq
    nblk = t_len // CMP_BLOCK
    gw = HPG * HEAD_DIM
    qrow = lambda b, g, i: (b * nq + i, g)

    def kv_spec(col0):
        return pl.BlockSpec((t_len, HEAD_DIM), lambda b, g, i: (b, col0 + g))

    return pl.pallas_call(
        _nsa_prompt_kernel,
        grid=(batch, N_KV, nq),
        in_specs=[
            pl.BlockSpec((tq, gw), qrow),
            pl.BlockSpec((tq, LANES), qrow),
            pl.BlockSpec((nblk, HEAD_DIM), lambda b, g, i: (b, g)),
            pl.BlockSpec((nblk, HEAD_DIM), lambda b, g, i: (b, N_KV + g)),
            kv_spec(2 * N_KV), kv_spec(3 * N_KV), kv_spec(4 * N_KV), kv_spec(5 * N_KV),
            pl.BlockSpec((t_len // KEY_CHUNK, nblk, KEY_CHUNK), lambda b, g, i: (0, 0, 0)),
        ],
        out_specs=pl.BlockSpec((tq, gw), qrow),
        out_shape=jax.ShapeDtypeStruct((batch * t_len, D_MODEL), BF16),
        compiler_params=_params("parallel", "parallel", "arbitrary"),
        name="nsa_prompt",
    )(q, gates, cmp_c, cmp_c, kvb, kvb, kvb, kvb, expand)


def _smp_cmp_kernel(q_ref, ck_ref, cv_ref, o_ref, imp_ref):
    nblk = ck_ref.shape[0]
    scale = HEAD_DIM ** -0.5
    q = q_ref[0, 0]
    rows = q.shape[0]
    tok = lax.broadcasted_iota(jnp.int32, (rows, 1), 0) % DEC_SEQ
    pos = PAST_LEN + tok
    blk = lax.broadcasted_iota(jnp.int32, (1, nblk), 1)
    vis = (blk + 1) * CMP_BLOCK - 1 <= pos
    p = _softmax_rows(jnp.where(vis, _dot_nt(q, ck_ref[...].astype(BF16)) * scale, NEG))
    p = p * jnp.where(pos >= CMP_BLOCK - 1, 1.0, 0.0)
    o_ref[0, 0] = _dot(p.astype(BF16), cv_ref[...].astype(BF16))
    imp = p[0:DEC_SEQ]
    for h in range(1, HPG):
        imp = imp + p[h * DEC_SEQ:(h + 1) * DEC_SEQ]
    imp_ref[0, 0] = imp


def _smp_cmp(qs, cmp_c, nblk):
    rows = HPG * DEC_SEQ
    return pl.pallas_call(
        _smp_cmp_kernel,
        grid=(DEC_BATCH, N_KV),
        in_specs=[
            pl.BlockSpec((1, 1, rows, HEAD_DIM), lambda b, g: (b, g, 0, 0)),
            pl.BlockSpec((nblk, HEAD_DIM), lambda b, g: (b, g)),
            pl.BlockSpec((nblk, HEAD_DIM), lambda b, g: (b, N_KV + g)),
        ],
        out_specs=[pl.BlockSpec((1, 1, rows, HEAD_DIM), lambda b, g: (b, g, 0, 0)),
                   pl.BlockSpec((1, 1, DEC_SEQ, nblk), lambda b, g: (b, g, 0, 0))],
        out_shape=[jax.ShapeDtypeStruct((DEC_BATCH, N_KV, rows, HEAD_DIM), F32),
                   jax.ShapeDtypeStruct((DEC_BATCH, N_KV, DEC_SEQ, nblk), F32)],
        compiler_params=_params("parallel", "parallel"),
        name="smp_cmp",
    )(qs, cmp_c, cmp_c)


def _smp_topk_kernel(imp_ref, idx_ref, *, n_sel_blocks):
    imp = imp_ref[...]
    rows, ncmp = imp.shape
    width = ncmp + LANES
    imp = jnp.concatenate([imp, jnp.zeros((rows, LANES), F32)], axis=1)
    j = lax.broadcasted_iota(jnp.int32, (rows, width), 1)
    tok = lax.broadcasted_iota(jnp.int32, (rows, 1), 0) % DEC_SEQ
    cur = (PAST_LEN + tok) // SEL_BLOCK
    forced = (j == 0) | (j == cur) | (j == cur - 1)
    score = jnp.where(forced, FORCED, jnp.where(j <= cur, imp, -1.0))
    score = jnp.where(j < n_sel_blocks, score, -jnp.inf)
    lane = lax.broadcasted_iota(jnp.int32, (rows, LANES), 1)
    out = jnp.zeros((rows, LANES), jnp.int32)
    jf = j.astype(F32)
    for n in range(min(TOP_N, n_sel_blocks)):
        m = jnp.max(score, axis=-1, keepdims=True)
        pick = jnp.min(jnp.where(score == m, jf, float(width)), axis=-1, keepdims=True)
        out = jnp.where(lane == n, pick.astype(jnp.int32), out)
        score = jnp.where(jf == pick, -jnp.inf, score)
    idx_ref[...] = out


def _smp_topk(imp, n_sel_blocks):
    rows = imp.shape[0]
    return pl.pallas_call(
        functools.partial(_smp_topk_kernel, n_sel_blocks=n_sel_blocks),
        out_shape=jax.ShapeDtypeStruct((rows, LANES), jnp.int32),
        name="smp_topk",
    )(imp)


def _smp_attn_kernel(idx_ref, pt_ref, q_ref, nsk_ref, nsv_ref, nwk_ref, nwv_ref, slc_hbm, win_hbm,
                     oslc_ref, owin_ref, kbuf, vbuf, wbuf, sem, *, n_pages, n_cache_blocks):
    n_sel = TOP_N
    per_page = PAGE_SIZE // SEL_BLOCK
    n_sel_keys = n_sel * SEL_BLOCK
    n_buf = wbuf.shape[2] - LANES
    s = pl.program_id(0)
    slot = s % 2

    def copies(step, slot_, for_wait):
        b, g = step // N_KV, step % N_KV
        out = []
        for t in range(DEC_SEQ):
            for n in range(n_sel):
                if for_wait:
                    row0 = 0
                else:
                    bid = jnp.minimum(idx_ref[(step * DEC_SEQ + t) * n_sel + n], n_cache_blocks - 1)
                    row0 = (pt_ref[b * n_pages + bid // per_page] * per_page + bid % per_page) * SEL_BLOCK
                dst = pl.ds(n * SEL_BLOCK, SEL_BLOCK)
                out.append(pltpu.make_async_copy(slc_hbm.at[pl.ds(row0, SEL_BLOCK), g, :],
                                                 kbuf.at[slot_, t, dst, :], sem.at[slot_]))
                out.append(pltpu.make_async_copy(slc_hbm.at[pl.ds(row0, SEL_BLOCK), N_KV + g, :],
                                                 vbuf.at[slot_, t, dst, :], sem.at[slot_]))
        for c in range(2):
            out.append(pltpu.make_async_copy(win_hbm.at[pl.ds(b * n_buf, n_buf), c * N_KV + g, :],
                                             wbuf.at[slot_, c, pl.ds(0, n_buf), :], sem.at[slot_]))
        return out

    @pl.when(s == 0)
    def _():
        for cp in copies(s, slot, False):
            cp.start()

    @pl.when(s + 1 < pl.num_programs(0))
    def _():
        for cp in copies(s + 1, 1 - slot, False):
            cp.start()

    for cp in copies(s, slot, True):
        cp.wait()

    scale = HEAD_DIM ** -0.5
    n_new = nsk_ref.shape[1]
    zeros = jnp.zeros((LANES, HEAD_DIM), F32)
    blk_lane = lax.broadcasted_iota(jnp.int32, (1, n_sel_keys + LANES), 1)
    in_cache = blk_lane < n_sel_keys
    off_new = blk_lane - n_sel_keys
    lane = lax.broadcasted_iota(jnp.int32, (1, n_buf + LANES), 1)
    kpos = jnp.where(lane < n_buf, PAST_LEN - n_buf + lane, PAST_LEN + lane - n_buf)
    real = (lane < n_buf) | (lane - n_buf < DEC_SEQ)

    wbuf[slot, 0, pl.ds(n_buf, LANES), :] = zeros
    wbuf[slot, 1, pl.ds(n_buf, LANES), :] = zeros
    wbuf[slot, 0, pl.ds(n_buf, n_new), :] = nwk_ref[0]
    wbuf[slot, 1, pl.ds(n_buf, n_new), :] = nwv_ref[0]
    kw = wbuf[slot, 0].astype(BF16)
    vw = wbuf[slot, 1].astype(BF16)

    for t in range(DEC_SEQ):
        pos = PAST_LEN + t
        q = q_ref[0, 0, t]
        ids = jnp.zeros((1, n_sel_keys + LANES), jnp.int32)
        has_new = jnp.zeros((1, 1), jnp.int32)
        for n in range(n_sel):
            bid = idx_ref[(s * DEC_SEQ + t) * n_sel + n]
            ids = jnp.where(blk_lane // SEL_BLOCK == n, bid, ids)
            has_new = jnp.maximum(has_new, jnp.where(bid == n_cache_blocks, 1, 0))
        kbuf[slot, t, pl.ds(n_sel_keys, LANES), :] = zeros
        vbuf[slot, t, pl.ds(n_sel_keys, LANES), :] = zeros
        kbuf[slot, t, pl.ds(n_sel_keys, n_new), :] = nsk_ref[0]
        vbuf[slot, t, pl.ds(n_sel_keys, n_new), :] = nsv_ref[0]
        tok = jnp.where(in_cache, ids * SEL_BLOCK + blk_lane % SEL_BLOCK, n_cache_blocks * SEL_BLOCK + off_new)
        src_ok = jnp.where(in_cache, jnp.where(ids != n_cache_blocks, 1, 0),
                           jnp.where(off_new < DEC_SEQ, 1, 0) * has_new)
        ok = (src_ok > 0) & (tok <= pos)
        ps = _softmax_rows(jnp.where(ok, _dot_nt(q, kbuf[slot, t].astype(BF16)) * scale, NEG))
        oslc_ref[0, 0, t] = _dot(ps.astype(BF16), vbuf[slot, t].astype(BF16))

        d = pos - kpos
        okw = real & (d >= 0) & (d < WINDOW)
        pw = _softmax_rows(jnp.where(okw, _dot_nt(q, kw) * scale, NEG))
        owin_ref[0, 0, t] = _dot(pw.astype(BF16), vw)


def _smp_attn(idx_flat, pt_flat, q8, cache_slc_rows, new_pad, cache_win_rows, n_pages, n_cache_blocks, n_buf):
    n_sel = TOP_N
    n_new = new_pad.shape[1]
    bg = lambda s, idx, pt: (s // N_KV, s % N_KV, 0, 0, 0)

    def new_spec(col0):
        return pl.BlockSpec((1, n_new, HEAD_DIM), lambda s, idx, pt: (s // N_KV, 0, col0 + s % N_KV))

    qo_spec = pl.BlockSpec((1, 1, DEC_SEQ, 8, HEAD_DIM), bg)
    grid_spec = pltpu.PrefetchScalarGridSpec(
        num_scalar_prefetch=2,
        grid=(DEC_BATCH * N_KV,),
        in_specs=[qo_spec, new_spec(2 * N_KV), new_spec(3 * N_KV), new_spec(4 * N_KV), new_spec(5 * N_KV),
                  pl.BlockSpec(memory_space=pl.ANY), pl.BlockSpec(memory_space=pl.ANY)],
        out_specs=[qo_spec, qo_spec],
        scratch_shapes=[pltpu.VMEM((2, DEC_SEQ, n_sel * SEL_BLOCK + LANES, HEAD_DIM), F32),
                        pltpu.VMEM((2, DEC_SEQ, n_sel * SEL_BLOCK + LANES, HEAD_DIM), F32),
                        pltpu.VMEM((2, 2, n_buf + LANES, HEAD_DIM), F32),
                        pltpu.SemaphoreType.DMA((2,))],
    )
    shape = jax.ShapeDtypeStruct((DEC_BATCH, N_KV, DEC_SEQ, 8, HEAD_DIM), F32)
    return pl.pallas_call(
        functools.partial(_smp_attn_kernel, n_pages=n_pages, n_cache_blocks=n_cache_blocks),
        grid_spec=grid_spec,
        out_shape=[shape, shape],
        compiler_params=_params("arbitrary"),
        name="smp_attn",
    )(idx_flat, pt_flat, q8, new_pad, new_pad, new_pad, new_pad, cache_slc_rows, cache_win_rows)


def _gate_combine_kernel(gt_ref, oc_ref, os_ref, ow_ref, o_ref):
    gt = gt_ref[...]
    for g in range(N_KV):
        for h in range(HPG):
            c0 = g * LANES + h * N_BRANCH
            lo = (g * HPG + h) * HEAD_DIM
            hi = lo + HEAD_DIM
            og = (gt[:, c0:c0 + 1] * oc_ref[:, lo:hi] + gt[:, c0 + 1:c0 + 2] * os_ref[:, lo:hi]
                  + gt[:, c0 + 2:c0 + 3] * ow_ref[:, lo:hi])
            o_ref[:, lo:hi] = og.astype(BF16)


def _gate_combine(gates, o_cmp, o_slc, o_win):
    return pl.pallas_call(
        _gate_combine_kernel,
        out_shape=jax.ShapeDtypeStruct(o_cmp.shape, BF16),
        name="gate_combine",
    )(gates, o_cmp, o_slc, o_win)


def _prep_weights(ln_g, ln_b, ffn_w_in, ffn_w_out, gmlp_w_in, gmlp_b_in, gmlp_ln_g, gmlp_ln_b, gmlp_w_s,
                  gmlp_b_s, gmlp_w_out, nsa_w_qg, nsa_w_o, w_kv, cmp_pos, w_cmp):
    nq = N_HEADS * HEAD_DIM
    wg = nsa_w_qg[:, :, nq:].reshape(N_B_LAYERS, D_MODEL, N_KV, GATE_COLS)
    wg = jnp.pad(wg, ((0, 0), (0, 0), (0, 0), (0, LANES - GATE_COLS))).reshape(N_B_LAYERS, D_MODEL, N_KV * LANES)
    pos_t = jnp.broadcast_to(cmp_pos[:, :, None, :], (CMP_BLOCK, 2, N_KV, HEAD_DIM)).reshape(CMP_BLOCK, KV_COLS)
    return dict(
        ln_g=ln_g.reshape(DEPTH, 3, 1, D_MODEL), ln_b=ln_b.reshape(DEPTH, 3, 1, D_MODEL),
        ffn_w_in=ffn_w_in, ffn_w_out=ffn_w_out,
        ffn_w_in0=ffn_w_in[0, 0].astype(BF16), ffn_w_out0=ffn_w_out[0, 0].astype(BF16),
        gmlp_w_in=gmlp_w_in.astype(BF16), gmlp_b_in=gmlp_b_in.reshape(N_A_LAYERS, 1, 2 * D_GATE),
        gmlp_ln_g=gmlp_ln_g.reshape(N_A_LAYERS, 1, D_GATE), gmlp_ln_b=gmlp_ln_b.reshape(N_A_LAYERS, 1, D_GATE),
        gmlp_w_s=gmlp_w_s, gmlp_b_s=gmlp_b_s, gmlp_w_out=gmlp_w_out.astype(BF16),
        w_q=nsa_w_qg[:, :, :nq].astype(BF16), w_g=wg.astype(BF16), w_o=nsa_w_o.astype(BF16),
        w_kv=w_kv.astype(BF16),
        pos8=pos_t.reshape(CMP_BLOCK, 2 * N_KV, HEAD_DIM),
        w_pair=jnp.concatenate([w_cmp[:, 0], w_cmp[:, 1]], axis=-1).astype(BF16).reshape(
            CMP_BLOCK // 2, 2 * HEAD_DIM, 2 * HEAD_DIM),
    )


def _trunks(xs, w, streams):
    xbs = [x.astype(BF16) for x in xs]
    ctxs = [None] * len(xs)
    v_rows = [[] for _ in xs]

    order = [(l, s) for l in range(DEPTH) for s in (0, 1)]
    ffn_w = [w["ffn_w_in0"], w["ffn_w_out0"]]

    def ffn(l, s):
        norm = 2 * s
        k = order.index((l, s))
        nxt = (w["ffn_w_in"], w["ffn_w_out"]) + order[k + 1] if k + 1 < len(order) else None
        out = _ffn(xs[0], xbs[0], xs[1], xbs[1], ffn_w[0], ffn_w[1], w["ln_g"][l, norm], w["ln_b"][l, norm], nxt)
        xs[0], xbs[0], xs[1], xbs[1] = out[:4]
        if nxt is not None:
            ffn_w[0], ffn_w[1] = out[4], out[5]

    for l in range(DEPTH):
        if l == N_A_LAYERS:
            for k, st in enumerate(streams):
                ctxs[k] = _kv_proj(xbs[k], w["w_kv"], st["tables"], st["pos_rows"])
        ffn(l, 0)
        for k, st in enumerate(streams):
            if l < N_A_LAYERS:
                rows, causal, w_sp, b_sp = st["spatial"](l)
                z = _gmlp_in(xbs[k], w["gmlp_w_in"][l], w["gmlp_b_in"][l], w["gmlp_ln_g"][l], w["gmlp_ln_b"][l])
                v_rows[k].append(z[:, D_GATE:])
                xs[k], xbs[k] = _gmlp_out(z, xs[k], w_sp, b_sp, w["gmlp_w_out"][l], w["ln_g"][l, 1],
                                          w["ln_b"][l, 1], rows, causal)
            else:
                o = st["attend"](l - N_A_LAYERS, xbs[k], ctxs[k])
                xs[k], xbs[k] = _proj_norm(o, xs[k], w["w_o"][l - N_A_LAYERS], w["ln_g"][l, 1], w["ln_b"][l, 1])
        ffn(l, 1)
    return [(xs[k], ctxs[k], v_rows[k]) for k in range(len(xs))]


def kernel(x_prompt, x_sample, cache_cmp_kv, cache_slc_kv, cache_win_kv, page_table, ln_g, ln_b, ffn_w_in,
           ffn_w_out, gmlp_w_in, gmlp_b_in, gmlp_ln_g, gmlp_ln_b, gmlp_w_s, gmlp_b_s, gmlp_w_out, nsa_w_qg,
           nsa_w_o, w_kv, cmp_pos, w_cmp):
    w = _prep_weights(ln_g, ln_b, ffn_w_in, ffn_w_out, gmlp_w_in, gmlp_b_in, gmlp_ln_g, gmlp_ln_b, gmlp_w_s,
                      gmlp_b_s, gmlp_w_out, nsa_w_qg, nsa_w_o, w_kv, cmp_pos, w_cmp)
    kv_shape = (2, N_KV, HEAD_DIM)

    mp = BATCH * SEQ
    tables_p = _rotary_tables(np.arange(SEQ))
    n_blk_p = SEQ // CMP_BLOCK
    key_blk = (np.arange(SEQ) // SEL_BLOCK).reshape(SEQ // KEY_CHUNK, 1, KEY_CHUNK)
    expand = jnp.asarray(key_blk == np.arange(n_blk_p)[None, :, None], BF16)
    cmp_p = {}

    def spatial_p(l):
        return CHUNK, CHUNK, w["gmlp_w_s"][l], w["gmlp_b_s"][l].T

    def attend_p(bl, xb, ctx):
        kvb, cmp_rows = ctx[0], ctx[1]
        if "c" not in cmp_p:
            slabs = jnp.arange(mp // PAGE_SIZE, dtype=jnp.int32)
            cmp_p["c"] = _compress_pages(cmp_rows, slabs, w["pos8"], w["w_pair"]).reshape(-1, KV_COLS)
        q, gates = _qg_proj(xb, w["w_q"][bl], w["w_g"][bl], tables_p, SEQ)
        return _nsa_prompt(q, gates, cmp_p["c"], kvb, expand, BATCH, SEQ)

    ms = DEC_BATCH * DEC_SEQ
    pos_s = PAST_LEN + np.arange(DEC_SEQ)
    tables_s = _rotary_tables(np.tile(pos_s, DEC_BATCH))
    n_pages = PAST_LEN // PAGE_SIZE
    n_cache_blocks = PAST_LEN // SEL_BLOCK
    n_sel_blocks = -(-(PAST_LEN + DEC_SEQ) // SEL_BLOCK)
    n_cmp_s = (PAST_LEN + DEC_SEQ) // CMP_BLOCK
    pt_flat = page_table.reshape(-1)
    cache_cmp_rows = cache_cmp_kv.reshape(-1, 2 * N_KV, HEAD_DIM)
    cache_slc_rows = cache_slc_kv.reshape(-1, 2 * N_KV, HEAD_DIM)
    cache_win_rows = cache_win_kv.reshape(-1, 2 * N_KV, HEAD_DIM)
    n_buf = cache_win_kv.shape[1]
    cmp_s = {}

    def spatial_s(l):
        c = min(DEC_SEQ, CHUNK)
        w_t = jnp.tile(w["gmlp_w_s"][l][:, :c, :c], (1, DEC_BATCH, DEC_BATCH))
        b_t = jnp.tile(w["gmlp_b_s"][l][:, :c].T, (DEC_BATCH, 1))
        return ms, c, w_t, b_t

    def attend_s(bl, xb, ctx):
        if "c" not in cmp_s:
            kv = jnp.concatenate([r.reshape(ms, KV_COLS) for r in ctx[1:]], axis=1)
            cmp_s["c"] = _compress_pages(cache_cmp_rows, pt_flat, w["pos8"], w["w_pair"]).reshape(-1, KV_COLS)
            new = kv.reshape(DEC_BATCH, DEC_SEQ, N_BRANCH * KV_COLS)
            cmp_s["new"] = jnp.pad(new, ((0, 0), (0, 8 - DEC_SEQ), (0, 0)))
        q, gates = _qg_proj(xb, w["w_q"][bl], w["w_g"][bl], tables_s, ms)
        q5 = q.reshape(DEC_BATCH, DEC_SEQ, N_KV, HPG, HEAD_DIM)
        qs = q5.transpose(0, 2, 3, 1, 4).reshape(DEC_BATCH, N_KV, HPG * DEC_SEQ, HEAD_DIM)
        o_cmp, imp = _smp_cmp(qs, cmp_s["c"], n_cmp_s)
        idx = _smp_topk(imp.reshape(DEC_BATCH * N_KV * DEC_SEQ, n_cmp_s), n_sel_blocks)[:, :TOP_N]
        q8 = jnp.pad(q5.transpose(0, 2, 1, 3, 4), ((0, 0), (0, 0), (0, 0), (0, 8 - HPG), (0, 0)))
        o_slc, o_win = _smp_attn(idx.reshape(-1), pt_flat, q8, cache_slc_rows, cmp_s["new"], cache_win_rows,
                                 n_pages, n_cache_blocks, n_buf)
        o_cmp = o_cmp.reshape(DEC_BATCH, N_KV, HPG, DEC_SEQ, HEAD_DIM).transpose(0, 3, 1, 2, 4).reshape(ms, D_MODEL)
        o_slc = o_slc[:, :, :, :HPG].transpose(0, 2, 1, 3, 4).reshape(ms, D_MODEL)
        o_win = o_win[:, :, :, :HPG].transpose(0, 2, 1, 3, 4).reshape(ms, D_MODEL)
        return _gate_combine(gates, o_cmp, o_slc, o_win)

    streams = [dict(tables=tables_p, pos_rows=SEQ, spatial=spatial_p, attend=attend_p),
               dict(tables=tables_s, pos_rows=ms, spatial=spatial_s, attend=attend_s)]
    (y_p, ctx_p, _), (y_s, ctx_s, v_s) = _trunks(
        [x_prompt.reshape(mp, D_MODEL), x_sample.reshape(ms, D_MODEL)], w, streams)
    p_cmp, p_slc, p_win = (r.reshape(BATCH, SEQ, *kv_shape) for r in ctx_p[1:])
    p_win = p_win[:, -min(WINDOW, SEQ):]
    s_cmp, s_slc, s_new = (r.reshape(DEC_BATCH, DEC_SEQ, *kv_shape) for r in ctx_s[1:])
    s_win = jnp.concatenate([cache_win_kv, s_new], axis=1)[:, -min(WINDOW, PAST_LEN + DEC_SEQ):]
    s_v = jnp.stack([v.reshape(DEC_BATCH, DEC_SEQ, D_GATE) for v in v_s])

    return (y_p.reshape(BATCH, SEQ, D_MODEL), y_s.reshape(DEC_BATCH, DEC_SEQ, D_MODEL), p_cmp, p_slc, p_win,
            s_cmp, s_slc, s_win, s_v)
```

```python
import functools
import math

import numpy as np
import jax
import jax.numpy as jnp
from jax import lax
from jax.experimental import pallas as pl
from jax.experimental.pallas import tpu as pltpu

D_MODEL = 2048
BATCH = 4
SEQ = 2048
DEPTH = 4
DEC_BATCH = 8
DEC_SEQ = 4
PAST_LEN = 16384
PAGE_SIZE = 128

N_A_LAYERS = DEPTH // 2
N_B_LAYERS = DEPTH - N_A_LAYERS
D_FF = 5632
D_GATE = D_MODEL
CHUNK = 128
N_GROUPS_A = 8
GROUP_DIM_A = D_GATE // N_GROUPS_A
HEAD_DIM = 128
N_HEADS = D_MODEL // HEAD_DIM
N_KV = 4
HPG = N_HEADS // N_KV
ROT_DIM = HEAD_DIM // 4
ROPE_THETA = 500000.0
CMP_BLOCK = 64
SEL_BLOCK = 64
TOP_N = 16
WINDOW = 512
N_BRANCH = 3
ALPHA = (2 * DEPTH) ** 0.25
LN_EPS = 1e-5
NEG = -1e30
FORCED = 1e4

F32 = jnp.float32
BF16 = jnp.bfloat16

V7X_VMEM_LIMIT_BYTES = 56 * 1024 * 1024
LOG2E = math.log2(math.e)
LANES = 128
KV_COLS = 2 * N_KV * HEAD_DIM
GATE_COLS = HPG * N_BRANCH
KEY_CHUNK = 512
PAGES_PER_STEP = 16
Q_TILE = 512
WIN_Q = 128


def _params(*sem):
    return pltpu.CompilerParams(dimension_semantics=sem, vmem_limit_bytes=V7X_VMEM_LIMIT_BYTES)


def _row_tile(m, cap=512):
    return m if m <= cap else cap


def _dot(a, b):
    return jnp.dot(a, b, preferred_element_type=F32)


def _dot_nt(a, b):
    return lax.dot_general(a, b, (((1,), (1,)), ((), ())), preferred_element_type=F32)


def _dot_tn(a, b):
    return lax.dot_general(a, b, (((0,), (0,)), ((), ())), preferred_element_type=F32)


def _layer_norm(x, g, b):
    mu = jnp.mean(x, axis=-1, keepdims=True)
    d = x - mu
    var = jnp.mean(d * d, axis=-1, keepdims=True)
    return d * lax.rsqrt(var + LN_EPS) * g + b


def _sigmoid(x):
    return 1.0 / (1.0 + jnp.exp(-x))


def _softmax_rows(s):
    m = jnp.max(s, axis=-1, keepdims=True)
    p = jnp.exp(s - m)
    return p / jnp.sum(p, axis=-1, keepdims=True)


def _rotate(x, cos_t, sin_lo, sin_hi):
    half = ROT_DIM // 2
    return (x * cos_t + pltpu.roll(x, LANES - half, 1) * sin_lo + pltpu.roll(x, half, 1) * sin_hi)


def _rotary_tables(pos):
    half = ROT_DIM // 2
    inv = ROPE_THETA ** (-jnp.arange(half, dtype=F32) / half)
    ang = jnp.asarray(pos).astype(F32)[:, None] * inv[None, :]
    cos, sin = jnp.cos(ang), jnp.sin(ang)
    rows = ang.shape[0]
    pad = jnp.zeros((rows, LANES - ROT_DIM), F32)
    zero = jnp.zeros((rows, half), F32)
    cos_t = jnp.concatenate([cos, cos, pad + 1.0], axis=1)
    sin_lo = jnp.concatenate([-sin, zero, pad], axis=1)
    sin_hi = jnp.concatenate([zero, sin, pad], axis=1)
    return cos_t, sin_lo, sin_hi


def _ffn_kernel(xb_ref, x_ref, xsb_ref, xs_ref, wg_ref, wu_ref, wo_ref, g_ref, b_ref, *refs, cast_next):
    i, f = pl.program_id(0), pl.program_id(1)
    tm = xb_ref.shape[0]
    last = pl.num_programs(1) - 1
    if cast_next:
        nwi_ref, nwo_ref, y_ref, yb_ref, ys_ref, ysb_ref, nwib_ref, nwob_ref, accs_ref = refs
    else:
        y_ref, yb_ref, ys_ref, ysb_ref, accs_ref = refs

    def swiglu(lhs):
        if cast_next:
            nwib_ref[...] = nwi_ref[...].astype(BF16)
            nwob_ref[...] = nwo_ref[...].astype(BF16)
        gate = _dot(lhs, wg_ref[...])
        up = _dot(lhs, wu_ref[...])
        return _dot((gate * _sigmoid(gate) * up).astype(BF16), wo_ref[...])

    @pl.when(f == 0)
    def _():
        y_ref[...] = jnp.zeros_like(y_ref)

    @pl.when(i == 0)
    def _():
        @pl.when(f == 0)
        def _():
            accs_ref[...] = jnp.zeros_like(accs_ref)

        part = swiglu(jnp.concatenate([xb_ref[...], xsb_ref[...]], axis=0))
        y_ref[...] += part[:tm]
        accs_ref[...] += part[tm:]

        @pl.when(f == last)
        def _():
            ys = _layer_norm(ALPHA * xs_ref[...] + 0.5 * accs_ref[...], g_ref[...], b_ref[...])
            ys_ref[...] = ys
            ysb_ref[...] = ys.astype(BF16)

    @pl.when(i > 0)
    def _():
        y_ref[...] += swiglu(xb_ref[...])

    @pl.when(f == last)
    def _():
        y = _layer_norm(ALPHA * x_ref[...] + 0.5 * y_ref[...], g_ref[...], b_ref[...])
        y_ref[...] = y
        yb_ref[...] = y.astype(BF16)


def _ffn(x, xb, xs, xsb, w_in_b, w_out_b, g, b, nxt=None):
    m, ms = x.shape[0], xs.shape[0]
    tm = _row_tile(m)
    tf = 512
    nf = D_FF // tf
    row = lambda i, f: (i, 0)
    fixed = lambda i, f: (0, 0)
    in_specs = [
        pl.BlockSpec((tm, D_MODEL), row),
        pl.BlockSpec((tm, D_MODEL), row),
        pl.BlockSpec((ms, D_MODEL), fixed),
        pl.BlockSpec((ms, D_MODEL), fixed),
        pl.BlockSpec((D_MODEL, tf), lambda i, f: (0, f)),
        pl.BlockSpec((D_MODEL, tf), lambda i, f: (0, nf + f)),
        pl.BlockSpec((tf, D_MODEL), lambda i, f: (f, 0)),
        pl.BlockSpec((1, D_MODEL), fixed),
        pl.BlockSpec((1, D_MODEL), fixed),
    ]
    out_specs = [pl.BlockSpec((tm, D_MODEL), row), pl.BlockSpec((tm, D_MODEL), row),
                 pl.BlockSpec((ms, D_MODEL), fixed), pl.BlockSpec((ms, D_MODEL), fixed)]
    out_shape = [jax.ShapeDtypeStruct((m, D_MODEL), F32), jax.ShapeDtypeStruct((m, D_MODEL), BF16),
                 jax.ShapeDtypeStruct((ms, D_MODEL), F32), jax.ShapeDtypeStruct((ms, D_MODEL), BF16)]
    args = [xb, x, xsb, xs, w_in_b, w_in_b, w_out_b, g, b]
    if nxt is not None:
        nw_in, nw_out, nl, ns = nxt
        steps = (m // tm) * nf
        ri = D_MODEL // 2
        ci, co = 2 * (2 * D_FF) // steps, D_FF // steps
        assert 2 * (2 * D_FF // ci) == steps and co * steps == D_FF and ci % LANES == 0 and co % 16 == 0
        in_slab = lambda i, f: ((i * nf + f) % 2, (i * nf + f) // 2)
        out_slab = lambda i, f: (i * nf + f, 0)
        in_specs += [pl.BlockSpec((None, None, ri, ci), lambda i, f: (nl, ns) + in_slab(i, f)),
                     pl.BlockSpec((None, None, co, D_MODEL), lambda i, f: (nl, ns) + out_slab(i, f))]
        out_specs += [pl.BlockSpec((ri, ci), in_slab), pl.BlockSpec((co, D_MODEL), out_slab)]
        out_shape += [jax.ShapeDtypeStruct((D_MODEL, 2 * D_FF), BF16), jax.ShapeDtypeStruct((D_FF, D_MODEL), BF16)]
        args += [nw_in, nw_out]
    return pl.pallas_call(
        functools.partial(_ffn_kernel, cast_next=nxt is not None),
        grid=(m // tm, nf),
        in_specs=in_specs,
        out_specs=out_specs,
        out_shape=out_shape,
        scratch_shapes=[pltpu.VMEM((ms, D_MODEL), F32)],
        compiler_params=_params("arbitrary", "arbitrary"),
        name="ffn",
    )(*args)


def _gmlp_in_kernel(xb_ref, w_ref, bias_ref, lg_ref, lb_ref, z_ref):
    n = pl.program_id(0)
    z = _dot(xb_ref[...], w_ref[...]) + bias_ref[...]
    z = 0.5 * z * (1.0 + lax.erf(z * (1.0 / math.sqrt(2.0))))

    @pl.when(n == 0)
    def _():
        z_ref[...] = z

    @pl.when(n == 1)
    def _():
        z_ref[...] = _layer_norm(z, lg_ref[...], lb_ref[...])


def _gmlp_in(xb, w, bias, lg, lb):
    m = xb.shape[0]
    tm = _row_tile(m)
    return pl.pallas_call(
        _gmlp_in_kernel,
        grid=(2, m // tm),
        in_specs=[
            pl.BlockSpec((tm, D_MODEL), lambda n, i: (i, 0)),
            pl.BlockSpec((D_MODEL, D_GATE), lambda n, i: (0, n)),
            pl.BlockSpec((1, D_GATE), lambda n, i: (0, n)),
            pl.BlockSpec((1, D_GATE), lambda n, i: (0, 0)),
            pl.BlockSpec((1, D_GATE), lambda n, i: (0, 0)),
        ],
        out_specs=pl.BlockSpec((tm, D_GATE), lambda n, i: (i, n)),
        out_shape=jax.ShapeDtypeStruct((m, 2 * D_GATE), F32),
        compiler_params=_params("arbitrary", "arbitrary"),
        name="gmlp_in",
    )(xb, w, bias, lg, lb)


def _gmlp_out_kernel(u_ref, v_ref, x_ref, ws_ref, bs_ref, wo_ref, g_ref, b_ref, y_ref, yb_ref, *, rows, causal):
    tm = u_ref.shape[0]
    r_i = lax.broadcasted_iota(jnp.int32, (rows, rows), 0)
    c_i = lax.broadcasted_iota(jnp.int32, (rows, rows), 1)
    keep = ((r_i // causal) == (c_i // causal)) & (c_i <= r_i)
    w_sp = [jnp.where(keep, ws_ref[g], 0.0).astype(BF16) for g in range(N_GROUPS_A)]
    bs = bs_ref[...]
    chunks = []
    for ch in range(tm // rows):
        u = u_ref[ch * rows:(ch + 1) * rows, :]
        v = v_ref[ch * rows:(ch + 1) * rows, :]
        parts = []
        for g in range(N_GROUPS_A):
            lo, hi = g * GROUP_DIM_A, (g + 1) * GROUP_DIM_A
            mixed = _dot(w_sp[g], v[:, lo:hi].astype(BF16)) + bs[:, g:g + 1]
            parts.append((u[:, lo:hi] * mixed).astype(BF16))
        chunks.append(jnp.concatenate(parts, axis=1))
    y = chunks[0] if len(chunks) == 1 else jnp.concatenate(chunks, axis=0)
    f = _dot(y, wo_ref[...])
    out = _layer_norm(ALPHA * x_ref[...] + f, g_ref[...], b_ref[...])
    y_ref[...] = out
    yb_ref[...] = out.astype(BF16)


def _gmlp_out(z, x, w_sp, b_sp, w_out, g, b, rows, causal):
    m = x.shape[0]
    tm = _row_tile(m)
    row = lambda i: (i, 0)
    kern = functools.partial(_gmlp_out_kernel, rows=rows, causal=causal)
    return pl.pallas_call(
        kern,
        grid=(m // tm,),
        in_specs=[
            pl.BlockSpec((tm, D_GATE), lambda i: (i, 0)),
            pl.BlockSpec((tm, D_GATE), lambda i: (i, 1)),
            pl.BlockSpec((tm, D_MODEL), row),
            pl.BlockSpec((N_GROUPS_A, rows, rows), lambda i: (0, 0, 0)),
            pl.BlockSpec((rows, N_GROUPS_A), lambda i: (0, 0)),
            pl.BlockSpec((D_GATE, D_MODEL), lambda i: (0, 0), pipeline_mode=pl.Buffered(1)),
            pl.BlockSpec((1, D_MODEL), lambda i: (0, 0)),
            pl.BlockSpec((1, D_MODEL), lambda i: (0, 0)),
        ],
        out_specs=[pl.BlockSpec((tm, D_MODEL), row), pl.BlockSpec((tm, D_MODEL), row)],
        out_shape=[jax.ShapeDtypeStruct((m, D_MODEL), F32), jax.ShapeDtypeStruct((m, D_MODEL), BF16)],
        compiler_params=_params("parallel"),
        name="gmlp_out",
    )(z, z, x, w_sp, b_sp, w_out, g, b)


def _proj_norm_kernel(a_ref, x_ref, w_ref, g_ref, b_ref, y_ref, yb_ref):
    out = _layer_norm(ALPHA * x_ref[...] + _dot(a_ref[...], w_ref[...]), g_ref[...], b_ref[...])
    y_ref[...] = out
    yb_ref[...] = out.astype(BF16)


def _proj_norm(a, x, w, g, b):
    m = x.shape[0]
    tm = _row_tile(m)
    row = lambda i: (i, 0)
    return pl.pallas_call(
        _proj_norm_kernel,
        grid=(m // tm,),
        in_specs=[
            pl.BlockSpec((tm, D_MODEL), row),
            pl.BlockSpec((tm, D_MODEL), row),
            pl.BlockSpec((D_MODEL, D_MODEL), lambda i: (0, 0), pipeline_mode=pl.Buffered(1)),
            pl.BlockSpec((1, D_MODEL), lambda i: (0, 0)),
            pl.BlockSpec((1, D_MODEL), lambda i: (0, 0)),
        ],
        out_specs=[pl.BlockSpec((tm, D_MODEL), row), pl.BlockSpec((tm, D_MODEL), row)],
        out_shape=[jax.ShapeDtypeStruct((m, D_MODEL), F32), jax.ShapeDtypeStruct((m, D_MODEL), BF16)],
        compiler_params=_params("parallel"),
        name="proj_norm",
    )(a, x, w, g, b)


def _kv_proj_kernel(xb_ref, w_ref, cos_ref, slo_ref, shi_ref, kvb_ref, cmp_hbm, slc_hbm, win_hbm, stage, sem):
    i, n = pl.program_id(0), pl.program_id(1)
    tm = xb_ref.shape[0]
    step = i * pl.num_programs(1) + n
    last = pl.num_programs(0) * pl.num_programs(1) - 1
    slot = step % 2
    outs = (cmp_hbm, slc_hbm, win_hbm)

    def copies(out_hbm, c, slot_):
        return [pltpu.make_async_copy(stage.at[slot_, :, pl.ds(g * HEAD_DIM, HEAD_DIM)],
                                      out_hbm.at[pl.ds(i * tm, tm), c * N_KV + g, :], sem.at[slot_])
                for g in range(N_KV)]

    @pl.when(step >= 2)
    def _():
        for cp in copies(cmp_hbm, 0, slot):
            cp.wait()

    acc = _dot(xb_ref[...], w_ref[...])
    for col in range(2 * N_BRANCH):
        @pl.when(n == col)
        def _(col=col):
            br, c = divmod(col, 2)
            if c == 0:
                cos_t, s_lo, s_hi = cos_ref[...], slo_ref[...], shi_ref[...]
                val = jnp.concatenate([_rotate(acc[:, h * HEAD_DIM:(h + 1) * HEAD_DIM], cos_t, s_lo, s_hi)
                                       for h in range(N_KV)], axis=1)
            else:
                val = acc
            stage[slot] = val
            kvb_ref[...] = val.astype(BF16)
            for cp in copies(outs[br], c, slot):
                cp.start()

    @pl.when(step == last)
    def _():
        for cp in copies(cmp_hbm, 0, 1 - slot) + copies(cmp_hbm, 0, slot):
            cp.wait()


def _kv_proj(xb, w, tables, pos_rows):
    m = xb.shape[0]
    tm = _row_tile(m, 1024)
    half = N_KV * HEAD_DIM
    ncol = N_BRANCH * 2
    nt = pos_rows // tm
    tab = lambda i, n: (i % nt, 0)
    rows = jax.ShapeDtypeStruct((m, 2 * N_KV, HEAD_DIM), F32)
    hbm = pl.BlockSpec(memory_space=pl.ANY)
    return pl.pallas_call(
        _kv_proj_kernel,
        grid=(m // tm, ncol),
        in_specs=[
            pl.BlockSpec((tm, D_MODEL), lambda i, n: (i, 0)),
            pl.BlockSpec((D_MODEL, half), lambda i, n: (0, n)),
            pl.BlockSpec((tm, LANES), tab),
            pl.BlockSpec((tm, LANES), tab),
            pl.BlockSpec((tm, LANES), tab),
        ],
        out_specs=[pl.BlockSpec((tm, half), lambda i, n: (i, n)), hbm, hbm, hbm],
        out_shape=[jax.ShapeDtypeStruct((m, ncol * half), BF16), rows, rows, rows],
        scratch_shapes=[pltpu.VMEM((2, tm, half), F32), pltpu.SemaphoreType.DMA((2,))],
        compiler_params=_params("arbitrary", "arbitrary"),
        name="kv_proj",
    )(xb, w, *tables)


def _qg_proj_kernel(xb_ref, wq_ref, wg_ref, cos_ref, slo_ref, shi_ref, q_ref, gt_ref):
    xb = xb_ref[...]
    acc = _dot(xb, wq_ref[...])
    cos_t, s_lo, s_hi = cos_ref[...], slo_ref[...], shi_ref[...]
    for h in range(N_HEADS):
        lo, hi = h * HEAD_DIM, (h + 1) * HEAD_DIM
        q_ref[:, lo:hi] = _rotate(acc[:, lo:hi], cos_t, s_lo, s_hi).astype(BF16)
    gt_ref[...] = _sigmoid(_dot(xb, wg_ref[...]))


def _qg_proj(xb, wq, wg, tables, pos_rows):
    m = xb.shape[0]
    tm = _row_tile(m)
    nt = pos_rows // tm
    tab = lambda i: (i % nt, 0)
    gcols = N_KV * LANES
    return pl.pallas_call(
        _qg_proj_kernel,
        grid=(m // tm,),
        in_specs=[
            pl.BlockSpec((tm, D_MODEL), lambda i: (i, 0)),
            pl.BlockSpec((D_MODEL, D_MODEL), lambda i: (0, 0), pipeline_mode=pl.Buffered(1)),
            pl.BlockSpec((D_MODEL, gcols), lambda i: (0, 0)),
            pl.BlockSpec((tm, LANES), tab),
            pl.BlockSpec((tm, LANES), tab),
            pl.BlockSpec((tm, LANES), tab),
        ],
        out_specs=[pl.BlockSpec((tm, D_MODEL), lambda i: (i, 0)), pl.BlockSpec((tm, gcols), lambda i: (i, 0))],
        out_shape=[jax.ShapeDtypeStruct((m, D_MODEL), BF16), jax.ShapeDtypeStruct((m, gcols), F32)],
        compiler_params=_params("parallel"),
        name="qg_proj",
    )(xb, wq, wg, *tables)


def _compress_pages_kernel(tbl_ref, *refs):
    del tbl_ref
    npg = PAGES_PER_STEP
    pages = refs[:npg]
    pos_ref, w_ref, out_ref = refs[npg:]
    per_page = PAGE_SIZE // CMP_BLOCK
    rows = npg * per_page * 2 * N_KV
    acc = jnp.zeros((rows, 2 * HEAD_DIM), F32)
    for j in range(CMP_BLOCK // 2):
        halves = []
        for l in (2 * j, 2 * j + 1):
            pos_l = pos_ref[l]
            tiles = [pages[p][l + CMP_BLOCK * h] + pos_l for p in range(npg) for h in range(per_page)]
            halves.append(jnp.concatenate(tiles, axis=0).astype(BF16))
        acc = acc + _dot(jnp.concatenate(halves, axis=1), w_ref[j])
    is_k = (lax.broadcasted_iota(jnp.int32, (rows, HEAD_DIM), 0) % (2 * N_KV)) < N_KV
    out_ref[...] = jnp.where(is_k, acc[:, :HEAD_DIM], acc[:, HEAD_DIM:])


def _compress_pages(cache, table, pos8, w_pair):
    npg = PAGES_PER_STEP
    rows = npg * (PAGE_SIZE // CMP_BLOCK) * 2 * N_KV

    def page_spec(p):
        return pl.BlockSpec((PAGE_SIZE, 2 * N_KV, HEAD_DIM), lambda i, tbl: (tbl[i * npg + p], 0, 0))

    grid_spec = pltpu.PrefetchScalarGridSpec(
        num_scalar_prefetch=1,
        grid=(table.shape[0] // npg,),
        in_specs=[page_spec(p) for p in range(npg)] + [
            pl.BlockSpec((CMP_BLOCK, 2 * N_KV, HEAD_DIM), lambda i, tbl: (0, 0, 0)),
            pl.BlockSpec((CMP_BLOCK // 2, 2 * HEAD_DIM, 2 * HEAD_DIM), lambda i, tbl: (0, 0, 0)),
        ],
        out_specs=pl.BlockSpec((rows, HEAD_DIM), lambda i, tbl: (i, 0)),
    )
    return pl.pallas_call(
        _compress_pages_kernel,
        grid_spec=grid_spec,
        out_shape=jax.ShapeDtypeStruct((table.shape[0] // npg * rows, HEAD_DIM), F32),
        compiler_params=_params("arbitrary"),
        name="compress_pages",
    )(table, *([cache] * npg), pos8, w_pair)


def _rank_select(score_t, n_blocks):
    idx = lax.broadcasted_iota(jnp.int32, score_t.shape, 0)
    rank = jnp.zeros(score_t.shape, F32)
    for i in range(n_blocks):
        row = score_t[i:i + 1, :]
        rank = rank + jnp.where(idx > i, jnp.where(row >= score_t, 1.0, 0.0), jnp.where(row > score_t, 1.0, 0.0))
    return jnp.where(rank < float(min(TOP_N, n_blocks)), 1.0, 0.0)


def _nsa_prompt_kernel(q_ref, gt_ref, ck_ref, cv_ref, sk_ref, sv_ref, wk_ref, wv_ref, e_ref, o_ref):
    tq = q_ref.shape[0]
    t_len = sk_ref.shape[0]
    nblk = ck_ref.shape[0]
    i = pl.program_id(2)
    scale = HEAD_DIM ** -0.5
    q = q_ref[...]
    q4 = jnp.concatenate([q[:, h * HEAD_DIM:(h + 1) * HEAD_DIM] for h in range(HPG)], axis=0)
    pos_c = i * tq + lax.broadcasted_iota(jnp.int32, (tq, 1), 0)
    pos_r = i * tq + lax.broadcasted_iota(jnp.int32, (1, tq), 1)
    pos4_r = jnp.concatenate([pos_r] * HPG, axis=1)

    c2 = scale * LOG2E

    ck = ck_ref[...].astype(BF16)
    cv = cv_ref[...].astype(BF16)
    blk_c = lax.broadcasted_iota(jnp.int32, (nblk, 1), 0)
    vis_t = (blk_c + 1) * CMP_BLOCK - 1 <= pos4_r
    s_t = jnp.where(vis_t, _dot_nt(ck, q4), NEG)
    p_t = jnp.exp2((s_t - jnp.max(s_t, axis=0, keepdims=True)) * c2)
    p_t = p_t * (1.0 / jnp.sum(p_t, axis=0, keepdims=True))
    p_t = p_t * jnp.where(pos4_r >= CMP_BLOCK - 1, 1.0, 0.0)
    o_cmp = _dot_tn(p_t.astype(BF16), cv)
    imp_t = p_t[:, 0:tq]
    for h in range(1, HPG):
        imp_t = imp_t + p_t[:, h * tq:(h + 1) * tq]
    cur = pos_r // SEL_BLOCK
    forced = (blk_c == 0) | (blk_c == cur) | (blk_c == cur - 1)
    score_t = jnp.where(forced, FORCED, jnp.where(blk_c <= cur, imp_t, -1.0))
    sel_t = _rank_select(score_t, nblk)
    sel_b = sel_t.astype(BF16)
    key_r = lax.broadcasted_iota(jnp.int32, (1, KEY_CHUNK), 1)

    def body(kc, carry):
        m, l, acc = carry
        start = pl.multiple_of(kc * KEY_CHUNK, KEY_CHUNK)
        k = sk_ref[pl.ds(start, KEY_CHUNK), :]
        v = sv_ref[pl.ds(start, KEY_CHUNK), :]
        member = _dot_tn(sel_b, e_ref[kc])
        bias = jnp.where(start + key_r <= pos_c, (member - 1.0) * (-NEG), NEG)
        s = _dot_nt(q4, k) + jnp.concatenate([bias] * HPG, axis=0)
        m_new = jnp.maximum(m, jnp.max(s, axis=-1, keepdims=True))
        a = jnp.exp2((m - m_new) * c2)
        pr = jnp.exp2((s - m_new) * c2)
        l = a * l + jnp.sum(pr, axis=-1, keepdims=True)
        acc = a * acc + _dot(pr.astype(BF16), v)
        return m_new, l, acc

    n_chunks = (i * tq + tq + KEY_CHUNK - 1) // KEY_CHUNK
    init = (jnp.full((HPG * tq, 1), NEG, F32), jnp.zeros((HPG * tq, 1), F32), jnp.zeros((HPG * tq, HEAD_DIM), F32))
    _, l_s, acc_s = lax.fori_loop(0, n_chunks, body, init)
    o_slc = acc_s * (1.0 / l_s)

    n_win = WINDOW + WIN_Q
    o_win_parts = []
    for sub in range(tq // WIN_Q):
        q0 = i * tq + sub * WIN_Q
        start = pl.multiple_of(jnp.maximum(q0 - WINDOW, 0), WIN_Q)
        kw = wk_ref[pl.ds(start, n_win), :]
        vw = wv_ref[pl.ds(start, n_win), :]
        d = (q0 + lax.broadcasted_iota(jnp.int32, (WIN_Q, 1), 0)) - (start + lax.broadcasted_iota(jnp.int32, (1, n_win), 1))
        bias_w = jnp.where((d >= 0) & (d < WINDOW), 0.0, NEG)
        qs = jnp.concatenate([q[sub * WIN_Q:(sub + 1) * WIN_Q, h * HEAD_DIM:(h + 1) * HEAD_DIM] for h in range(HPG)],
                             axis=0)
        sw = _dot_nt(qs, kw) + jnp.concatenate([bias_w] * HPG, axis=0)
        pw = jnp.exp2((sw - jnp.max(sw, axis=-1, keepdims=True)) * c2)
        o_win_parts.append(_dot(pw.astype(BF16), vw) * (1.0 / jnp.sum(pw, axis=-1, keepdims=True)))

    gt = gt_ref[...]
    for h in range(HPG):
        r0, r1 = h * tq, (h + 1) * tq
        c0 = h * N_BRANCH
        o_win = jnp.concatenate([part[h * WIN_Q:(h + 1) * WIN_Q] for part in o_win_parts], axis=0)
        og = (gt[:, c0:c0 + 1] * o_cmp[r0:r1] + gt[:, c0 + 1:c0 + 2] * o_slc[r0:r1]
              + gt[:, c0 + 2:c0 + 3] * o_win)
        o_ref[:, h * HEAD_DIM:(h + 1) * HEAD_DIM] = og.astype(BF16)


def _nsa_prompt(q, gates, cmp_c, kvb, expand, batch, t_len):
    tq = Q_TILE
    nq = t_len // tq
    nblk = t_len // CMP_BLOCK
    gw = HPG * HEAD_DIM
    qrow = lambda b, g, i: (b * nq + i, g)

    def kv_spec(col0):
        return pl.BlockSpec((t_len, HEAD_DIM), lambda b, g, i: (b, col0 + g))

    return pl.pallas_call(
        _nsa_prompt_kernel,
        grid=(batch, N_KV, nq),
        in_specs=[
            pl.BlockSpec((tq, gw), qrow),
            pl.BlockSpec((tq, LANES), qrow),
            pl.BlockSpec((nblk, HEAD_DIM), lambda b, g, i: (b, g)),
            pl.BlockSpec((nblk, HEAD_DIM), lambda b, g, i: (b, N_KV + g)),
            kv_spec(2 * N_KV), kv_spec(3 * N_KV), kv_spec(4 * N_KV), kv_spec(5 * N_KV),
            pl.BlockSpec((t_len // KEY_CHUNK, nblk, KEY_CHUNK), lambda b, g, i: (0, 0, 0)),
        ],
        out_specs=pl.BlockSpec((tq, gw), qrow),
        out_shape=jax.ShapeDtypeStruct((batch * t_len, D_MODEL), BF16),
        compiler_params=_params("parallel", "parallel", "arbitrary"),
        name="nsa_prompt",
    )(q, gates, cmp_c, cmp_c, kvb, kvb, kvb, kvb, expand)


def _smp_cmp_kernel(q_ref, ck_ref, cv_ref, o_ref, imp_ref):
    nblk = ck_ref.shape[0]
    scale = HEAD_DIM ** -0.5
    q = q_ref[0, 0]
    rows = q.shape[0]
    tok = lax.broadcasted_iota(jnp.int32, (rows, 1), 0) % DEC_SEQ
    pos = PAST_LEN + tok
    blk = lax.broadcasted_iota(jnp.int32, (1, nblk), 1)
    vis = (blk + 1) * CMP_BLOCK - 1 <= pos
    p = _softmax_rows(jnp.where(vis, _dot_nt(q, ck_ref[...].astype(BF16)) * scale, NEG))
    p = p * jnp.where(pos >= CMP_BLOCK - 1, 1.0, 0.0)
    o_ref[0, 0] = _dot(p.astype(BF16), cv_ref[...].astype(BF16))
    imp = p[0:DEC_SEQ]
    for h in range(1, HPG):
        imp = imp + p[h * DEC_SEQ:(h + 1) * DEC_SEQ]
    imp_ref[0, 0] = imp


def _smp_cmp(qs, cmp_c, nblk):
    rows = HPG * DEC_SEQ
    return pl.pallas_call(
        _smp_cmp_kernel,
        grid=(DEC_BATCH, N_KV),
        in_specs=[
            pl.BlockSpec((1, 1, rows, HEAD_DIM), lambda b, g: (b, g, 0, 0)),
            pl.BlockSpec((nblk, HEAD_DIM), lambda b, g: (b, g)),
            pl.BlockSpec((nblk, HEAD_DIM), lambda b, g: (b, N_KV + g)),
        ],
        out_specs=[pl.BlockSpec((1, 1, rows, HEAD_DIM), lambda b, g: (b, g, 0, 0)),
                   pl.BlockSpec((1, 1, DEC_SEQ, nblk), lambda b, g: (b, g, 0, 0))],
        out_shape=[jax.ShapeDtypeStruct((DEC_BATCH, N_KV, rows, HEAD_DIM), F32),
                   jax.ShapeDtypeStruct((DEC_BATCH, N_KV, DEC_SEQ, nblk), F32)],
        compiler_params=_params("parallel", "parallel"),
        name="smp_cmp",
    )(qs, cmp_c, cmp_c)


def _smp_topk_kernel(imp_ref, idx_ref, *, n_sel_blocks):
    imp = imp_ref[...]
    rows, ncmp = imp.shape
    width = ncmp + LANES
    imp = jnp.concatenate([imp, jnp.zeros((rows, LANES), F32)], axis=1)
    j = lax.broadcasted_iota(jnp.int32, (rows, width), 1)
    tok = lax.broadcasted_iota(jnp.int32, (rows, 1), 0) % DEC_SEQ
    cur = (PAST_LEN + tok) // SEL_BLOCK
    forced = (j == 0) | (j == cur) | (j == cur - 1)
    score = jnp.where(forced, FORCED, jnp.where(j <= cur, imp, -1.0))
    score = jnp.where(j < n_sel_blocks, score, -jnp.inf)
    lane = lax.broadcasted_iota(jnp.int32, (rows, LANES), 1)
    out = jnp.zeros((rows, LANES), jnp.int32)
    jf = j.astype(F32)
    for n in range(min(TOP_N, n_sel_blocks)):
        m = jnp.max(score, axis=-1, keepdims=True)
        pick = jnp.min(jnp.where(score == m, jf, float(width)), axis=-1, keepdims=True)
        out = jnp.where(lane == n, pick.astype(jnp.int32), out)
        score = jnp.where(jf == pick, -jnp.inf, score)
    idx_ref[...] = out


def _smp_topk(imp, n_sel_blocks):
    rows = imp.shape[0]
    return pl.pallas_call(
        functools.partial(_smp_topk_kernel, n_sel_blocks=n_sel_blocks),
        out_shape=jax.ShapeDtypeStruct((rows, LANES), jnp.int32),
        name="smp_topk",
    )(imp)


def _smp_attn_kernel(idx_ref, pt_ref, q_ref, nsk_ref, nsv_ref, nwk_ref, nwv_ref, slc_hbm, win_hbm,
                     oslc_ref, owin_ref, kbuf, vbuf, wbuf, sem, *, n_pages, n_cache_blocks):
    n_sel = TOP_N
    per_page = PAGE_SIZE // SEL_BLOCK
    n_sel_keys = n_sel * SEL_BLOCK
    n_buf = wbuf.shape[2] - LANES
    s = pl.program_id(0)
    slot = s % 2

    def copies(step, slot_, for_wait):
        b, g = step // N_KV, step % N_KV
        out = []
        for t in range(DEC_SEQ):
            for n in range(n_sel):
                if for_wait:
                    row0 = 0
                else:
                    bid = jnp.minimum(idx_ref[(step * DEC_SEQ + t) * n_sel + n], n_cache_blocks - 1)
                    row0 = (pt_ref[b * n_pages + bid // per_page] * per_page + bid % per_page) * SEL_BLOCK
                dst = pl.ds(n * SEL_BLOCK, SEL_BLOCK)
                out.append(pltpu.make_async_copy(slc_hbm.at[pl.ds(row0, SEL_BLOCK), g, :],
                                                 kbuf.at[slot_, t, dst, :], sem.at[slot_]))
                out.append(pltpu.make_async_copy(slc_hbm.at[pl.ds(row0, SEL_BLOCK), N_KV + g, :],
                                                 vbuf.at[slot_, t, dst, :], sem.at[slot_]))
        for c in range(2):
            out.append(pltpu.make_async_copy(win_hbm.at[pl.ds(b * n_buf, n_buf), c * N_KV + g, :],
                                             wbuf.at[slot_, c, pl.ds(0, n_buf), :], sem.at[slot_]))
        return out

    @pl.when(s == 0)
    def _():
        for k, cp in enumerate(copies(s, slot, False)):
            cp.start(priority=k % 2)

    @pl.when(s + 1 < pl.num_programs(0))
    def _():
        for k, cp in enumerate(copies(s + 1, 1 - slot, False)):
            cp.start(priority=k % 2)

    for cp in copies(s, slot, True):
        cp.wait()

    scale = HEAD_DIM ** -0.5
    n_new = nsk_ref.shape[1]
    zeros = jnp.zeros((LANES, HEAD_DIM), F32)
    blk_lane = lax.broadcasted_iota(jnp.int32, (1, n_sel_keys + LANES), 1)
    in_cache = blk_lane < n_sel_keys
    off_new = blk_lane - n_sel_keys
    lane = lax.broadcasted_iota(jnp.int32, (1, n_buf + LANES), 1)
    kpos = jnp.where(lane < n_buf, PAST_LEN - n_buf + lane, PAST_LEN + lane - n_buf)
    real = (lane < n_buf) | (lane - n_buf < DEC_SEQ)

    wbuf[slot, 0, pl.ds(n_buf, LANES), :] = zeros
    wbuf[slot, 1, pl.ds(n_buf, LANES), :] = zeros
    wbuf[slot, 0, pl.ds(n_buf, n_new), :] = nwk_ref[0]
    wbuf[slot, 1, pl.ds(n_buf, n_new), :] = nwv_ref[0]
    kw = wbuf[slot, 0].astype(BF16)
    vw = wbuf[slot, 1].astype(BF16)

    for t in range(DEC_SEQ):
        pos = PAST_LEN + t
        q = q_ref[0, 0, t]
        ids = jnp.zeros((1, n_sel_keys + LANES), jnp.int32)
        has_new = jnp.zeros((1, 1), jnp.int32)
        for n in range(n_sel):
            bid = idx_ref[(s * DEC_SEQ + t) * n_sel + n]
            ids = jnp.where(blk_lane // SEL_BLOCK == n, bid, ids)
            has_new = jnp.maximum(has_new, jnp.where(bid == n_cache_blocks, 1, 0))
        kbuf[slot, t, pl.ds(n_sel_keys, LANES), :] = zeros
        vbuf[slot, t, pl.ds(n_sel_keys, LANES), :] = zeros
        kbuf[slot, t, pl.ds(n_sel_keys, n_new), :] = nsk_ref[0]
        vbuf[slot, t, pl.ds(n_sel_keys, n_new), :] = nsv_ref[0]
        tok = jnp.where(in_cache, ids * SEL_BLOCK + blk_lane % SEL_BLOCK, n_cache_blocks * SEL_BLOCK + off_new)
        src_ok = jnp.where(in_cache, jnp.where(ids != n_cache_blocks, 1, 0),
                           jnp.where(off_new < DEC_SEQ, 1, 0) * has_new)
        ok = (src_ok > 0) & (tok <= pos)
        ps = _softmax_rows(jnp.where(ok, _dot_nt(q, kbuf[slot, t].astype(BF16)) * scale, NEG))
        oslc_ref[0, 0, t] = _dot(ps.astype(BF16), vbuf[slot, t].astype(BF16))

        d = pos - kpos
        okw = real & (d >= 0) & (d < WINDOW)
        pw = _softmax_rows(jnp.where(okw, _dot_nt(q, kw) * scale, NEG))
        owin_ref[0, 0, t] = _dot(pw.astype(BF16), vw)


def _smp_attn(idx_flat, pt_flat, q8, cache_slc_rows, new_pad, cache_win_rows, n_pages, n_cache_blocks, n_buf):
    n_sel = TOP_N
    n_new = new_pad.shape[1]
    bg = lambda s, idx, pt: (s // N_KV, s % N_KV, 0, 0, 0)

    def new_spec(col0):
        return pl.BlockSpec((1, n_new, HEAD_DIM), lambda s, idx, pt: (s // N_KV, 0, col0 + s % N_KV))

    qo_spec = pl.BlockSpec((1, 1, DEC_SEQ, 8, HEAD_DIM), bg)
    grid_spec = pltpu.PrefetchScalarGridSpec(
        num_scalar_prefetch=2,
        grid=(DEC_BATCH * N_KV,),
        in_specs=[qo_spec, new_spec(2 * N_KV), new_spec(3 * N_KV), new_spec(4 * N_KV), new_spec(5 * N_KV),
                  pl.BlockSpec(memory_space=pl.ANY), pl.BlockSpec(memory_space=pl.ANY)],
        out_specs=[qo_spec, qo_spec],
        scratch_shapes=[pltpu.VMEM((2, DEC_SEQ, n_sel * SEL_BLOCK + LANES, HEAD_DIM), F32),
                        pltpu.VMEM((2, DEC_SEQ, n_sel * SEL_BLOCK + LANES, HEAD_DIM), F32),
                        pltpu.VMEM((2, 2, n_buf + LANES, HEAD_DIM), F32),
                        pltpu.SemaphoreType.DMA((2,))],
    )
    shape = jax.ShapeDtypeStruct((DEC_BATCH, N_KV, DEC_SEQ, 8, HEAD_DIM), F32)
    return pl.pallas_call(
        functools.partial(_smp_attn_kernel, n_pages=n_pages, n_cache_blocks=n_cache_blocks),
        grid_spec=grid_spec,
        out_shape=[shape, shape],
        compiler_params=_params("arbitrary"),
        name="smp_attn",
    )(idx_flat, pt_flat, q8, new_pad, new_pad, new_pad, new_pad, cache_slc_rows, cache_win_rows)


def _gate_combine_kernel(gt_ref, oc_ref, os_ref, ow_ref, o_ref):
    gt = gt_ref[...]
    for g in range(N_KV):
        for h in range(HPG):
            c0 = g * LANES + h * N_BRANCH
            lo = (g * HPG + h) * HEAD_DIM
            hi = lo + HEAD_DIM
            og = (gt[:, c0:c0 + 1] * oc_ref[:, lo:hi] + gt[:, c0 + 1:c0 + 2] * os_ref[:, lo:hi]
                  + gt[:, c0 + 2:c0 + 3] * ow_ref[:, lo:hi])
            o_ref[:, lo:hi] = og.astype(BF16)


def _gate_combine(gates, o_cmp, o_slc, o_win):
    return pl.pallas_call(
        _gate_combine_kernel,
        out_shape=jax.ShapeDtypeStruct(o_cmp.shape, BF16),
        name="gate_combine",
    )(gates, o_cmp, o_slc, o_win)


def _prep_weights(ln_g, ln_b, ffn_w_in, ffn_w_out, gmlp_w_in, gmlp_b_in, gmlp_ln_g, gmlp_ln_b, gmlp_w_s,
                  gmlp_b_s, gmlp_w_out, nsa_w_qg, nsa_w_o, w_kv, cmp_pos, w_cmp):
    nq = N_HEADS * HEAD_DIM
    wg = nsa_w_qg[:, :, nq:].reshape(N_B_LAYERS, D_MODEL, N_KV, GATE_COLS)
    wg = jnp.pad(wg, ((0, 0), (0, 0), (0, 0), (0, LANES - GATE_COLS))).reshape(N_B_LAYERS, D_MODEL, N_KV * LANES)
    pos_t = jnp.broadcast_to(cmp_pos[:, :, None, :], (CMP_BLOCK, 2, N_KV, HEAD_DIM)).reshape(CMP_BLOCK, KV_COLS)
    return dict(
        ln_g=ln_g.reshape(DEPTH, 3, 1, D_MODEL), ln_b=ln_b.reshape(DEPTH, 3, 1, D_MODEL),
        ffn_w_in=ffn_w_in, ffn_w_out=ffn_w_out,
        ffn_w_in0=ffn_w_in[0, 0].astype(BF16), ffn_w_out0=ffn_w_out[0, 0].astype(BF16),
        gmlp_w_in=gmlp_w_in.astype(BF16), gmlp_b_in=gmlp_b_in.reshape(N_A_LAYERS, 1, 2 * D_GATE),
        gmlp_ln_g=gmlp_ln_g.reshape(N_A_LAYERS, 1, D_GATE), gmlp_ln_b=gmlp_ln_b.reshape(N_A_LAYERS, 1, D_GATE),
        gmlp_w_s=gmlp_w_s, gmlp_b_s=gmlp_b_s, gmlp_w_out=gmlp_w_out.astype(BF16),
        w_q=nsa_w_qg[:, :, :nq].astype(BF16), w_g=wg.astype(BF16), w_o=nsa_w_o.astype(BF16),
        w_kv=w_kv.astype(BF16),
        pos8=pos_t.reshape(CMP_BLOCK, 2 * N_KV, HEAD_DIM),
        w_pair=jnp.concatenate([w_cmp[:, 0], w_cmp[:, 1]], axis=-1).astype(BF16).reshape(
            CMP_BLOCK // 2, 2 * HEAD_DIM, 2 * HEAD_DIM),
    )


def _trunks(xs, w, streams):
    xbs = [x.astype(BF16) for x in xs]
    ctxs = [None] * len(xs)
    v_rows = [[] for _ in xs]

    order = [(l, s) for l in range(DEPTH) for s in (0, 1)]
    ffn_w = [w["ffn_w_in0"], w["ffn_w_out0"]]

    def ffn(l, s):
        norm = 2 * s
        k = order.index((l, s))
        nxt = (w["ffn_w_in"], w["ffn_w_out"]) + order[k + 1] if k + 1 < len(order) else None
        out = _ffn(xs[0], xbs[0], xs[1], xbs[1], ffn_w[0], ffn_w[1], w["ln_g"][l, norm], w["ln_b"][l, norm], nxt)
        xs[0], xbs[0], xs[1], xbs[1] = out[:4]
        if nxt is not None:
            ffn_w[0], ffn_w[1] = out[4], out[5]

    for l in range(DEPTH):
        if l == N_A_LAYERS:
            for k, st in enumerate(streams):
                ctxs[k] = _kv_proj(xbs[k], w["w_kv"], st["tables"], st["pos_rows"])
        ffn(l, 0)
        for k, st in enumerate(streams):
            if l < N_A_LAYERS:
                rows, causal, w_sp, b_sp = st["spatial"](l)
                z = _gmlp_in(xbs[k], w["gmlp_w_in"][l], w["gmlp_b_in"][l], w["gmlp_ln_g"][l], w["gmlp_ln_b"][l])
                v_rows[k].append(z[:, D_GATE:])
                xs[k], xbs[k] = _gmlp_out(z, xs[k], w_sp, b_sp, w["gmlp_w_out"][l], w["ln_g"][l, 1],
                                          w["ln_b"][l, 1], rows, causal)
            else:
                o = st["attend"](l - N_A_LAYERS, xbs[k], ctxs[k])
                xs[k], xbs[k] = _proj_norm(o, xs[k], w["w_o"][l - N_A_LAYERS], w["ln_g"][l, 1], w["ln_b"][l, 1])
        ffn(l, 1)
    return [(xs[k], ctxs[k], v_rows[k]) for k in range(len(xs))]


def kernel(x_prompt, x_sample, cache_cmp_kv, cache_slc_kv, cache_win_kv, page_table, ln_g, ln_b, ffn_w_in,
           ffn_w_out, gmlp_w_in, gmlp_b_in, gmlp_ln_g, gmlp_ln_b, gmlp_w_s, gmlp_b_s, gmlp_w_out, nsa_w_qg,
           nsa_w_o, w_kv, cmp_pos, w_cmp):
    w = _prep_weights(ln_g, ln_b, ffn_w_in, ffn_w_out, gmlp_w_in, gmlp_b_in, gmlp_ln_g, gmlp_ln_b, gmlp_w_s,
                      gmlp_b_s, gmlp_w_out, nsa_w_qg, nsa_w_o, w_kv, cmp_pos, w_cmp)
    kv_shape = (2, N_KV, HEAD_DIM)

    mp = BATCH * SEQ
    tables_p = _rotary_tables(np.arange(SEQ))
    n_blk_p = SEQ // CMP_BLOCK
    key_blk = (np.arange(SEQ) // SEL_BLOCK).reshape(SEQ // KEY_CHUNK, 1, KEY_CHUNK)
    expand = jnp.asarray(key_blk == np.arange(n_blk_p)[None, :, None], BF16)
    cmp_p = {}

    def spatial_p(l):
        return CHUNK, CHUNK, w["gmlp_w_s"][l], w["gmlp_b_s"][l].T

    def attend_p(bl, xb, ctx):
        kvb, cmp_rows = ctx[0], ctx[1]
        if "c" not in cmp_p:
            slabs = jnp.arange(mp // PAGE_SIZE, dtype=jnp.int32)
            cmp_p["c"] = _compress_pages(cmp_rows, slabs, w["pos8"], w["w_pair"]).reshape(-1, KV_COLS)
        q, gates = _qg_proj(xb, w["w_q"][bl], w["w_g"][bl], tables_p, SEQ)
        return _nsa_prompt(q, gates, cmp_p["c"], kvb, expand, BATCH, SEQ)

    ms = DEC_BATCH * DEC_SEQ
    pos_s = PAST_LEN + np.arange(DEC_SEQ)
    tables_s = _rotary_tables(np.tile(pos_s, DEC_BATCH))
    n_pages = PAST_LEN // PAGE_SIZE
    n_cache_blocks = PAST_LEN // SEL_BLOCK
    n_sel_blocks = -(-(PAST_LEN + DEC_SEQ) // SEL_BLOCK)
    n_cmp_s = (PAST_LEN + DEC_SEQ) // CMP_BLOCK
    pt_flat = page_table.reshape(-1)
    cache_cmp_rows = cache_cmp_kv.reshape(-1, 2 * N_KV, HEAD_DIM)
    cache_slc_rows = cache_slc_kv.reshape(-1, 2 * N_KV, HEAD_DIM)
    cache_win_rows = cache_win_kv.reshape(-1, 2 * N_KV, HEAD_DIM)
    n_buf = cache_win_kv.shape[1]
    cmp_s = {}

    def spatial_s(l):
        c = min(DEC_SEQ, CHUNK)
        w_t = jnp.tile(w["gmlp_w_s"][l][:, :c, :c], (1, DEC_BATCH, DEC_BATCH))
        b_t = jnp.tile(w["gmlp_b_s"][l][:, :c].T, (DEC_BATCH, 1))
        return ms, c, w_t, b_t

    def attend_s(bl, xb, ctx):
        if "c" not in cmp_s:
            kv = jnp.concatenate([r.reshape(ms, KV_COLS) for r in ctx[1:]], axis=1)
            cmp_s["c"] = _compress_pages(cache_cmp_rows, pt_flat, w["pos8"], w["w_pair"]).reshape(-1, KV_COLS)
            new = kv.reshape(DEC_BATCH, DEC_SEQ, N_BRANCH * KV_COLS)
            cmp_s["new"] = jnp.pad(new, ((0, 0), (0, 8 - DEC_SEQ), (0, 0)))
        q, gates = _qg_proj(xb, w["w_q"][bl], w["w_g"][bl], tables_s, ms)
        q5 = q.reshape(DEC_BATCH, DEC_SEQ, N_KV, HPG, HEAD_DIM)
        qs = q5.transpose(0, 2, 3, 1, 4).reshape(DEC_BATCH, N_KV, HPG * DEC_SEQ, HEAD_DIM)
        o_cmp, imp = _smp_cmp(qs, cmp_s["c"], n_cmp_s)
        idx = _smp_topk(imp.reshape(DEC_BATCH * N_KV * DEC_SEQ, n_cmp_s), n_sel_blocks)[:, :TOP_N]
        q8 = jnp.pad(q5.transpose(0, 2, 1, 3, 4), ((0, 0), (0, 0), (0, 0), (0, 8 - HPG), (0, 0)))
        o_slc, o_win = _smp_attn(idx.reshape(-1), pt_flat, q8, cache_slc_rows, cmp_s["new"], cache_win_rows,
                                 n_pages, n_cache_blocks, n_buf)
        o_cmp = o_cmp.reshape(DEC_BATCH, N_KV, HPG, DEC_SEQ, HEAD_DIM).transpose(0, 3, 1, 2, 4).reshape(ms, D_MODEL)
        o_slc = o_slc[:, :, :, :HPG].transpose(0, 2, 1, 3, 4).reshape(ms, D_MODEL)
        o_win = o_win[:, :, :, :HPG].transpose(0, 2, 1, 3, 4).reshape(ms, D_MODEL)
        return _gate_combine(gates, o_cmp, o_slc, o_win)

    streams = [dict(tables=tables_p, pos_rows=SEQ, spatial=spatial_p, attend=attend_p),
               dict(tables=tables_s, pos_rows=ms, spatial=spatial_s, attend=attend_s)]
    (y_p, ctx_p, _), (y_s, ctx_s, v_s) = _trunks(
        [x_prompt.reshape(mp, D_MODEL), x_sample.reshape(ms, D_MODEL)], w, streams)
    p_cmp, p_slc, p_win = (r.reshape(BATCH, SEQ, *kv_shape) for r in ctx_p[1:])
    p_win = p_win[:, -min(WINDOW, SEQ):]
    s_cmp, s_slc, s_new = (r.reshape(DEC_BATCH, DEC_SEQ, *kv_shape) for r in ctx_s[1:])
    s_win = jnp.concatenate([cache_win_kv, s_new], axis=1)[:, -min(WINDOW, PAST_LEN + DEC_SEQ):]
    s_v = jnp.stack([v.reshape(DEC_BATCH, DEC_SEQ, D_GATE) for v in v_s])

    return (y_p.reshape(BATCH, SEQ, D_MODEL), y_s.reshape(DEC_BATCH, DEC_SEQ, D_MODEL), p_cmp, p_slc, p_win,
            s_cmp, s_slc, s_win, s_v)
```
